```python
import jax, jax.numpy as jnp
from jax import lax
import numpy as np

D_MODEL = 4096
BATCH = 4
SEQ = 2048
DEPTH = 4

MEM_LEN = 256
HEAD_DIM = 128
ROPE_THETA = 10000.0
EPS = 1e-6
NEG = -1e30

SC_WIDTH = D_MODEL // 2
SC_KERNEL = 3

NSA_HEADS = 16
NSA_KV_GROUPS = 4
NSA_WIDTH = NSA_HEADS * HEAD_DIM
NSA_BRANCHES = 3
CMP_BLOCK = 32
CMP_STRIDE = 16
SLC_BLOCK = 64
SLC_TOP_N = 16
SLC_Q_BLOCK = 64
WINDOW = 512
Q_BLOCK = 128

X_HEADS = 4
X_WIDTH = X_HEADS * HEAD_DIM

N_BRANCHES = 3
IN_SPLITS = (SC_WIDTH, SC_WIDTH, SC_WIDTH, SC_WIDTH,
             NSA_WIDTH,
             NSA_BRANCHES * 2 * NSA_KV_GROUPS * HEAD_DIM,
             NSA_BRANCHES * NSA_HEADS,
             NSA_WIDTH,
             X_WIDTH, X_WIDTH,
             N_BRANCHES * D_MODEL)
IN_WIDTH = sum(IN_SPLITS)

kernel_name = 'hybrid_conv_nsa_mem_gated_merge'


def rmsnorm(x, g):
    xf = x.astype(jnp.float32)
    y = xf * lax.rsqrt(jnp.mean(xf * xf, axis=-1, keepdims=True) + EPS)
    return (y * g.astype(jnp.float32)).astype(x.dtype)


def rope(x, pos):
    half = HEAD_DIM // 2
    inv_freq = ROPE_THETA ** (-jnp.arange(half, dtype=jnp.float32) / half)
    ang = pos.astype(jnp.float32)[:, None] * inv_freq[None, :]
    cos = jnp.cos(ang)[None, :, None, :]
    sin = jnp.sin(ang)[None, :, None, :]
    xf = x.astype(jnp.float32)
    x1, x2 = xf[..., :half], xf[..., half:]
    return jnp.concatenate([x1 * cos - x2 * sin, x2 * cos + x1 * sin], axis=-1).astype(x.dtype)


def masked_softmax(s, mask):
    p = jax.nn.softmax(jnp.where(mask, s, NEG), axis=-1)
    return jnp.where(mask, p, 0.0)


def short_conv_mixer(a_h, a_b, a_c, conv_w, conv_b):
    seq = a_h.shape[1]
    u = a_c * a_h
    up = jnp.pad(u, ((0, 0), (SC_KERNEL - 1, 0), (0, 0)))
    y = conv_b
    for k in range(SC_KERNEL):
        y = y + conv_w[k] * up[:, k:k + seq]
    return a_b * y


def nsa_mixer(q, kv, gate_logits, cmp_pos, cmp_w1, cmp_w2):
    bsz, seq = q.shape[0], q.shape[1]
    G, R, Dh = NSA_KV_GROUPS, NSA_HEADS // NSA_KV_GROUPS, HEAD_DIM
    scale = Dh ** -0.5
    pos = jnp.arange(seq, dtype=jnp.int32)
    kv = kv.reshape(bsz, seq, NSA_BRANCHES, 2, G, Dh)
    qg = q.reshape(bsz, seq, G, R, Dh)
    qg_rot = rope(q, pos).reshape(bsz, seq, G, R, Dh)

    n_cmp = (seq - CMP_BLOCK) // CMP_STRIDE + 1
    cmp_start = jnp.arange(n_cmp, dtype=jnp.int32) * CMP_STRIDE
    tok = cmp_start[:, None] + jnp.arange(CMP_BLOCK, dtype=jnp.int32)[None, :]
    blk = kv[:, :, 0][:, tok]
    blk = blk + jnp.transpose(cmp_pos, (1, 0, 2))[:, :, None, :]
    blk = jnp.transpose(blk, (0, 1, 3, 4, 2, 5)).reshape(bsz, n_cmp, 2, G, CMP_BLOCK * Dh)
    hid = jax.nn.silu(jnp.einsum('bncgf,cfd->bncgd', blk, cmp_w1))
    kvc = jnp.einsum('bncgd,cde->bncge', hid, cmp_w2)
    k_cmp, v_cmp = kvc[:, :, 0], kvc[:, :, 1]
    s_cmp = jnp.einsum('bsgrd,bngd->bgrsn', qg, k_cmp).astype(jnp.float32) * scale
    cmp_mask = (cmp_start + CMP_BLOCK - 1)[None, :] <= pos[:, None]
    p_cmp = masked_softmax(s_cmp, cmp_mask)
    o_cmp = jnp.einsum('bgrsn,bngd->bsgrd', p_cmp.astype(v_cmp.dtype), v_cmp)

    n_slc = seq // SLC_BLOCK
    slc_start = jnp.arange(n_slc, dtype=jnp.int32) * SLC_BLOCK
    overlap = ((cmp_start[:, None] < slc_start[None, :] + SLC_BLOCK)
               & (cmp_start[:, None] + CMP_BLOCK > slc_start[None, :])).astype(jnp.float32)
    imp = jnp.einsum('bgrsn,nj->bgsj', p_cmp, overlap)
    blk_id = jnp.arange(n_slc, dtype=jnp.int32)[None, :]
    cur = (pos // SLC_BLOCK)[:, None]
    future = slc_start[None, :] > pos[:, None]
    forced = (blk_id == 0) | (blk_id == cur) | (blk_id == cur - 1)
    imp = jnp.where(future, -jnp.inf, jnp.where(forced, jnp.inf, imp))
    n_top = min(SLC_TOP_N, n_slc)
    _, sel = lax.top_k(imp, n_top)

    k_slc = rope(kv[:, :, 1, 0], pos)
    v_slc = kv[:, :, 1, 1]
    ks = k_slc.reshape(bsz, n_slc, SLC_BLOCK, G, Dh).transpose(0, 3, 1, 2, 4)
    vs = v_slc.reshape(bsz, n_slc, SLC_BLOCK, G, Dh).transpose(0, 3, 1, 2, 4)
    nq = seq // SLC_Q_BLOCK
    q_chunks = qg_rot.reshape(bsz, nq, SLC_Q_BLOCK, G, R, Dh).transpose(1, 0, 2, 3, 4, 5)
    sel_chunks = sel.reshape(bsz, G, nq, SLC_Q_BLOCK, n_top).transpose(2, 0, 1, 3, 4)
    pos_chunks = pos.reshape(nq, SLC_Q_BLOCK)
    bi = jnp.arange(bsz)[:, None, None, None]
    gi = jnp.arange(G)[None, :, None, None]
    n_keys = n_top * SLC_BLOCK

    def slc_block(args):
        qc, selc, tc = args
        kc = ks[bi, gi, selc].reshape(bsz, G, SLC_Q_BLOCK, n_keys, Dh)
        vc = vs[bi, gi, selc].reshape(bsz, G, SLC_Q_BLOCK, n_keys, Dh)
        key_pos = (selc[..., None] * SLC_BLOCK + jnp.arange(SLC_BLOCK, dtype=jnp.int32)).reshape(
            bsz, G, SLC_Q_BLOCK, n_keys)
        mask = (key_pos <= tc[None, None, :, None])[:, :, None]
        s = jnp.einsum('bqgrd,bgqmd->bgrqm', qc, kc).astype(jnp.float32) * scale
        p = masked_softmax(s, mask)
        return jnp.einsum('bgrqm,bgqmd->bqgrd', p.astype(vc.dtype), vc)

    o_slc = lax.map(slc_block, (q_chunks, sel_chunks, pos_chunks))
    o_slc = o_slc.transpose(1, 0, 2, 3, 4, 5).reshape(bsz, seq, G, R, Dh)

    k_win = rope(kv[:, :, 2, 0], pos)
    v_win = kv[:, :, 2, 1]
    kp = jnp.pad(k_win, ((0, 0), (WINDOW, 0), (0, 0), (0, 0)))
    vp = jnp.pad(v_win, ((0, 0), (WINDOW, 0), (0, 0), (0, 0)))
    nw = seq // Q_BLOCK
    span = WINDOW + Q_BLOCK
    qw = qg_rot.reshape(bsz, nw, Q_BLOCK, G, R, Dh).transpose(1, 0, 2, 3, 4, 5)
    starts = jnp.arange(nw, dtype=jnp.int32) * Q_BLOCK

    def win_block(args):
        qc, c0 = args
        kc = lax.dynamic_slice_in_dim(kp, c0, span, axis=1)
        vc = lax.dynamic_slice_in_dim(vp, c0, span, axis=1)
        q_pos = c0 + jnp.arange(Q_BLOCK, dtype=jnp.int32)
        k_pos = c0 - WINDOW + jnp.arange(span, dtype=jnp.int32)
        d = q_pos[:, None] - k_pos[None, :]
        mask = (d >= 0) & (d < WINDOW) & (k_pos[None, :] >= 0)
        s = jnp.einsum('bqgrd,bkgd->bgrqk', qc, kc).astype(jnp.float32) * scale
        p = masked_softmax(s, mask)
        return jnp.einsum('bgrqk,bkgd->bqgrd', p.astype(vc.dtype), vc)

    o_win = lax.map(win_block, (qw, starts))
    o_win = o_win.transpose(1, 0, 2, 3, 4, 5).reshape(bsz, seq, G, R, Dh)

    g = jax.nn.sigmoid(gate_logits.astype(jnp.float32)).astype(q.dtype).reshape(
        bsz, seq, NSA_HEADS, NSA_BRANCHES, 1)
    o = (g[:, :, :, 0] * o_cmp.reshape(bsz, seq, NSA_HEADS, Dh)
         + g[:, :, :, 1] * o_slc.reshape(bsz, seq, NSA_HEADS, Dh)
         + g[:, :, :, 2] * o_win.reshape(bsz, seq, NSA_HEADS, Dh))
    return o.reshape(bsz, seq, NSA_WIDTH)


def memory_attention(q, mem_kv):
    k, v = mem_kv[:, :, 0], mem_kv[:, :, 1]
    s = jnp.einsum('bshd,bmhd->bhsm', q, k).astype(jnp.float32) * (HEAD_DIM ** -0.5)
    p = jax.nn.softmax(s, axis=-1)
    return jnp.einsum('bhsm,bmhd->bshd', p.astype(v.dtype), v)


def hybrid_layer(x, mem, norm_g, w_in, conv_w, conv_b, cmp_pos, cmp_w1, cmp_w2,
                 mem_norm_g, w_mem_kv, w_up_a, w_up_b, w_up_x, w_out):
    bsz, seq, _ = x.shape
    h = rmsnorm(x, norm_g)
    proj = h @ w_in
    offs = []
    acc = 0
    for sz in IN_SPLITS[:-1]:
        acc += sz
        offs.append(acc)
    (a_h, a_b, a_c, a_z, n_q, n_kv, n_g, n_z, x_q, x_z, m_g) = jnp.split(proj, offs, axis=-1)

    y_a = short_conv_mixer(a_h, a_b, a_c, conv_w, conv_b) * jax.nn.silu(a_z)

    y_b = nsa_mixer(n_q.reshape(bsz, seq, NSA_HEADS, HEAD_DIM), n_kv, n_g,
                    cmp_pos, cmp_w1, cmp_w2) * jax.nn.silu(n_z)

    mem_kv = (rmsnorm(mem, mem_norm_g) @ w_mem_kv).reshape(bsz, mem.shape[1], 2, X_HEADS, HEAD_DIM)
    y_x = memory_attention(x_q.reshape(bsz, seq, X_HEADS, HEAD_DIM), mem_kv).reshape(
        bsz, seq, X_WIDTH) * jax.nn.silu(x_z)

    gates = jax.nn.sigmoid(m_g.astype(jnp.float32)).astype(x.dtype).reshape(bsz, seq, N_BRANCHES, D_MODEL)
    u = (gates[:, :, 0] * (y_a @ w_up_a)
         + gates[:, :, 1] * (y_b @ w_up_b)
         + gates[:, :, 2] * (y_x @ w_up_x))
    return x + u @ w_out


def setup_inputs(seed: int = 0) -> dict:
    key = jax.random.key(seed)
    ks = jax.random.split(key, 16)
    f32 = jnp.float32
    nrm = lambda k, shape, sc: jax.random.normal(k, shape, f32) * sc
    return {
        'x': nrm(ks[0], (BATCH, SEQ, D_MODEL), 1.0),
        'mem': nrm(ks[1], (BATCH, MEM_LEN, D_MODEL), 1.0),
        'norm_g': 1.0 + nrm(ks[2], (DEPTH, D_MODEL), 0.02),
        'w_in': nrm(ks[3], (DEPTH, D_MODEL, IN_WIDTH), D_MODEL ** -0.5),
        'conv_w': nrm(ks[4], (DEPTH, SC_KERNEL, SC_WIDTH), SC_KERNEL ** -0.5),
        'conv_b': nrm(ks[5], (DEPTH, SC_WIDTH), 0.02),
        'cmp_pos': nrm(ks[6], (DEPTH, 2, CMP_BLOCK, HEAD_DIM), 0.1),
        'cmp_w1': nrm(ks[7], (DEPTH, 2, CMP_BLOCK * HEAD_DIM, HEAD_DIM), (CMP_BLOCK * HEAD_DIM) ** -0.5),
        'cmp_w2': nrm(ks[8], (DEPTH, 2, HEAD_DIM, HEAD_DIM), HEAD_DIM ** -0.5),
        'mem_norm_g': 1.0 + nrm(ks[9], (DEPTH, D_MODEL), 0.02),
        'w_mem_kv': nrm(ks[10], (DEPTH, D_MODEL, 2 * X_WIDTH), D_MODEL ** -0.5),
        'w_up_a': nrm(ks[11], (DEPTH, SC_WIDTH, D_MODEL), SC_WIDTH ** -0.5),
        'w_up_b': nrm(ks[12], (DEPTH, NSA_WIDTH, D_MODEL), NSA_WIDTH ** -0.5),
        'w_up_x': nrm(ks[13], (DEPTH, X_WIDTH, D_MODEL), X_WIDTH ** -0.5),
        'w_out': nrm(ks[14], (DEPTH, D_MODEL, D_MODEL), D_MODEL ** -0.5),
        'final_g': 1.0 + nrm(ks[15], (D_MODEL,), 0.02),
    }


def reference(x, mem, norm_g, w_in, conv_w, conv_b, cmp_pos, cmp_w1, cmp_w2,
              mem_norm_g, w_mem_kv, w_up_a, w_up_b, w_up_x, w_out, final_g):
    for l in range(DEPTH):
        x = hybrid_layer(x, mem, norm_g[l], w_in[l], conv_w[l], conv_b[l], cmp_pos[l],
                         cmp_w1[l], cmp_w2[l], mem_norm_g[l], w_mem_kv[l],
                         w_up_a[l], w_up_b[l], w_up_x[l], w_out[l])
    return rmsnorm(x, final_g)
```

```python
import functools

import numpy as np
import jax
import jax.numpy as jnp
from jax import lax
from jax.experimental import pallas as pl
from jax.experimental.pallas import tpu as pltpu

F32 = jnp.float32
BF16 = jnp.bfloat16

HEAD_DIM = 128
ROPE_THETA = 10000.0
EPS = 1e-6
NEG = -1e30
SC_KERNEL = 3
NSA_HEADS = 16
NSA_KV_GROUPS = 4
NSA_REP = NSA_HEADS // NSA_KV_GROUPS
NSA_BRANCHES = 3
CMP_BLOCK = 32
CMP_STRIDE = 16
SLC_BLOCK = 64
SLC_TOP_N = 16
WINDOW = 512
X_HEADS = 4
N_BRANCHES = 3
SCALE = HEAD_DIM ** -0.5

VMEM_LIMIT_BYTES = 48 * 1024 * 1024
LANES = 128


def _cparams(*sem):
    return pltpu.CompilerParams(dimension_semantics=sem, vmem_limit_bytes=VMEM_LIMIT_BYTES)


def _silu(x):
    return x / (1.0 + jnp.exp(-x))


def _sigmoid(x):
    return 1.0 / (1.0 + jnp.exp(-x))


def _rmsnorm_kernel(x_ref, g_ref, o_ref):
    x = x_ref[...]
    ms = jnp.mean(x * x, axis=-1, keepdims=True)
    o_ref[...] = (x * lax.rsqrt(ms + EPS) * g_ref[...]).astype(o_ref.dtype)


def _rmsnorm(x, g3, l, out_dtype, tm=256):
    m, d = x.shape
    return pl.pallas_call(
        _rmsnorm_kernel,
        grid=(m // tm,),
        in_specs=[pl.BlockSpec((tm, d), lambda i: (i, 0)),
                  pl.BlockSpec((None, 1, d), lambda i: (l, 0, 0))],
        out_specs=pl.BlockSpec((tm, d), lambda i: (i, 0)),
        out_shape=jax.ShapeDtypeStruct((m, d), out_dtype),
        compiler_params=_cparams("parallel"),
        name="rmsnorm",
    )(x, g3)


def _mm_kernel(a_ref, b_ref, o_ref, acc_ref, *, nk):
    k = pl.program_id(2)

    @pl.when(k == 0)
    def _():
        acc_ref[...] = jnp.zeros_like(acc_ref)

    acc_ref[...] += jnp.dot(a_ref[...], b_ref[...], preferred_element_type=F32)

    @pl.when(k == nk - 1)
    def _():
        o_ref[...] = acc_ref[...].astype(o_ref.dtype)


def _mm_res_kernel(a_ref, b_ref, r_ref, o_ref, acc_ref, *, nk):
    k = pl.program_id(2)

    @pl.when(k == 0)
    def _():
        acc_ref[...] = jnp.zeros_like(acc_ref)

    acc_ref[...] += jnp.dot(a_ref[...], b_ref[...], preferred_element_type=F32)

    @pl.when(k == nk - 1)
    def _():
        o_ref[...] = (r_ref[...] + acc_ref[...]).astype(o_ref.dtype)


def _matmul(a, b, l, out_dtype, tm, tn, tk, residual=None, name="matmul"):
    m, kdim = a.shape
    n = b.shape[-1]
    tm, tn, tk = min(tm, m), min(tn, n), min(tk, kdim)
    nk = kdim // tk
    in_specs = [pl.BlockSpec((tm, tk), lambda i, j, k: (i, k)),
                pl.BlockSpec((None, tk, tn), lambda i, j, k: (l, k, j))]
    args = [a, b]
    if residual is None:
        body = functools.partial(_mm_kernel, nk=nk)
    else:
        body = functools.partial(_mm_res_kernel, nk=nk)
        in_specs.append(pl.BlockSpec((tm, tn), lambda i, j, k: (i, j)))
        args.append(residual)
    return pl.pallas_call(
        body,
        grid=(m // tm, n // tn, nk),
        in_specs=in_specs,
        out_specs=pl.BlockSpec((tm, tn), lambda i, j, k: (i, j)),
        out_shape=jax.ShapeDtypeStruct((m, n), out_dtype),
        scratch_shapes=[pltpu.VMEM((tm, tn), F32)],
        compiler_params=_cparams("parallel", "parallel", "arbitrary"),
        name=name,
    )(*args)


def _conv_kernel(h_ref, b_ref, c_ref, z_ref, w_ref, cb_ref, o_ref):
    u = c_ref[...] * h_ref[...]
    row = lax.broadcasted_iota(jnp.int32, u.shape, 0)
    u1 = jnp.where(row >= 1, pltpu.roll(u, 1, 0), 0.0)
    u2 = jnp.where(row >= 2, pltpu.roll(u, 2, 0), 0.0)
    w = w_ref[...]
    y = cb_ref[...] + w[0:1, :] * u2
    y = y + w[1:2, :] * u1
    y = y + w[2:3, :] * u
    o_ref[...] = (b_ref[...] * y * _silu(z_ref[...])).astype(o_ref.dtype)


def _conv_branch(proj, conv_w, conv_b3, l, bsz, seq, width, tc=256):
    nb = width // tc
    blk = lambda off: pl.BlockSpec((seq, tc), lambda b, j: (b, off * nb + j))
    return pl.pallas_call(
        _conv_kernel,
        grid=(bsz, nb),
        in_specs=[blk(0), blk(1), blk(2), blk(3),
                  pl.BlockSpec((None, SC_KERNEL, tc), lambda b, j: (l, 0, j)),
                  pl.BlockSpec((None, 1, tc), lambda b, j: (l, 0, j))],
        out_specs=pl.BlockSpec((seq, tc), lambda b, j: (b, j)),
        out_shape=jax.ShapeDtypeStruct((bsz * seq, width), BF16),
        compiler_params=_cparams("parallel", "parallel"),
        name="conv_branch",
    )(proj, proj, proj, proj, conv_w, conv_b3)


def _memattn_kernel(q_ref, k_ref, v_ref, z_ref, o_ref):
    q = q_ref[...].astype(BF16)
    k = k_ref[...].astype(BF16)
    v = v_ref[...].astype(BF16)
    s = lax.dot_general(q, k, (((1,), (1,)), ((), ())), preferred_element_type=F32) * SCALE
    m = jnp.max(s, axis=-1, keepdims=True)
    e = jnp.exp(s - m)
    p = e / jnp.sum(e, axis=-1, keepdims=True)
    o = jnp.dot(p.astype(BF16), v, preferred_element_type=F32)
    o_ref[...] = (o * _silu(z_ref[...])).astype(o_ref.dtype)


def _mem_attention(proj, memkv, bsz, seq, mem_len, q_blk, z_blk, tq=512):
    nq = seq // tq
    return pl.pallas_call(
        _memattn_kernel,
        grid=(bsz, X_HEADS, nq),
        in_specs=[pl.BlockSpec((tq, HEAD_DIM), lambda b, h, i: (b * nq + i, q_blk + h)),
                  pl.BlockSpec((mem_len, HEAD_DIM), lambda b, h, i: (b, h)),
                  pl.BlockSpec((mem_len, HEAD_DIM), lambda b, h, i: (b, X_HEADS + h)),
                  pl.BlockSpec((tq, HEAD_DIM), lambda b, h, i: (b * nq + i, z_blk + h))],
        out_specs=pl.BlockSpec((tq, HEAD_DIM), lambda b, h, i: (b * nq + i, h)),
        out_shape=jax.ShapeDtypeStruct((bsz * seq, X_HEADS * HEAD_DIM), BF16),
        compiler_params=_cparams("parallel", "parallel", "parallel"),
        name="mem_attention",
    )(proj, memkv, memkv, proj)


def _compress_kernel(kv_ref, pos_ref, w1_ref, w2_ref, o_ref, *, n_half):
    half = CMP_BLOCK // 2
    acc_a = jnp.zeros((n_half, HEAD_DIM), F32)
    acc_b = jnp.zeros((n_half, HEAD_DIM), F32)
    for t in range(half):
        x = kv_ref[pl.ds(t, n_half, stride=half), :]
        xa = (x + pos_ref[t:t + 1, :]).astype(BF16)
        xb = (x + pos_ref[half + t:half + t + 1, :]).astype(BF16)
        acc_a = acc_a + jnp.dot(xa, w1_ref[t * HEAD_DIM:(t + 1) * HEAD_DIM, :],
                                preferred_element_type=F32)
        acc_b = acc_b + jnp.dot(xb, w1_ref[(half + t) * HEAD_DIM:(half + t + 1) * HEAD_DIM, :],
                                preferred_element_type=F32)
    hid = _silu(acc_a + pltpu.roll(acc_b, n_half - 1, 0))
    o_ref[...] = jnp.dot(hid.astype(BF16), w2_ref[...], preferred_element_type=F32)


def _compress(proj, cmp_pos, cmp_w1, cmp_w2, l, bsz, seq, kv_blk):
    g = NSA_KV_GROUPS
    n_half = seq // (CMP_BLOCK // 2)
    return pl.pallas_call(
        functools.partial(_compress_kernel, n_half=n_half),
        grid=(bsz, 2, g),
        in_specs=[pl.BlockSpec((seq, HEAD_DIM), lambda b, c, gi: (b, kv_blk + c * g + gi)),
                  pl.BlockSpec((None, None, CMP_BLOCK, HEAD_DIM), lambda b, c, gi: (l, c, 0, 0)),
                  pl.BlockSpec((None, None, CMP_BLOCK * HEAD_DIM, HEAD_DIM),
                               lambda b, c, gi: (l, c, 0, 0)),
                  pl.BlockSpec((None, None, HEAD_DIM, HEAD_DIM), lambda b, c, gi: (l, c, 0, 0))],
        out_specs=pl.BlockSpec((None, None, None, n_half, HEAD_DIM),
                               lambda b, c, gi: (b, c, gi, 0, 0)),
        out_shape=jax.ShapeDtypeStruct((bsz, 2, g, n_half, HEAD_DIM), F32),
        compiler_params=_cparams("parallel", "parallel", "parallel"),
        name="nsa_compress",
    )(proj, cmp_pos, cmp_w1, cmp_w2)


def _rope_tile(x, cos, sin_signed):
    return x * cos + pltpu.roll(x, HEAD_DIM // 2, 1) * sin_signed


def _nsa_prep_kernel(q_ref, ks_ref, vs_ref, kw_ref, vw_ref, cos_ref, sin_ref,
                     qb_ref, qr_ref, kso_ref, vso_ref, kwo_ref, vwo_ref):
    cos = cos_ref[...]
    sin = sin_ref[...]
    for h in range(NSA_HEADS):
        sl = slice(h * HEAD_DIM, (h + 1) * HEAD_DIM)
        x = q_ref[:, sl]
        qb_ref[:, sl] = x.astype(BF16)
        qr_ref[:, sl] = _rope_tile(x, cos, sin).astype(BF16)
    for g in range(NSA_KV_GROUPS):
        sl = slice(g * HEAD_DIM, (g + 1) * HEAD_DIM)
        kso_ref[:, sl] = _rope_tile(ks_ref[:, sl], cos, sin).astype(BF16)
        kwo_ref[:, sl] = _rope_tile(kw_ref[:, sl], cos, sin).astype(BF16)
    vso_ref[...] = vs_ref[...].astype(BF16)
    vwo_ref[...] = vw_ref[...].astype(BF16)


def _nsa_prep(proj, cos, sin_signed, bsz, seq, q_off, kv_off, tr=256):
    nsa_w = NSA_HEADS * HEAD_DIM
    gw = NSA_KV_GROUPS * HEAD_DIM
    nr = seq // tr
    t = bsz * seq
    qblk = q_off // nsa_w
    kvb = kv_off // gw
    rowq = lambda w, cb: pl.BlockSpec((tr, w), lambda b, i: (b * nr + i, cb))
    tab = pl.BlockSpec((tr, HEAD_DIM), lambda b, i: (i, 0))
    return pl.pallas_call(
        _nsa_prep_kernel,
        grid=(bsz, nr),
        in_specs=[rowq(nsa_w, qblk), rowq(gw, kvb + 2), rowq(gw, kvb + 3),
                  rowq(gw, kvb + 4), rowq(gw, kvb + 5), tab, tab],
        out_specs=[rowq(nsa_w, 0), rowq(nsa_w, 0), rowq(gw, 0), rowq(gw, 0),
                   rowq(gw, 0), rowq(gw, 0)],
        out_shape=[jax.ShapeDtypeStruct((t, nsa_w), BF16), jax.ShapeDtypeStruct((t, nsa_w), BF16),
                   jax.ShapeDtypeStruct((t, gw), BF16), jax.ShapeDtypeStruct((t, gw), BF16),
                   jax.ShapeDtypeStruct((t, gw), BF16), jax.ShapeDtypeStruct((t, gw), BF16)],
        compiler_params=_cparams("parallel", "parallel"),
        name="nsa_prep",
    )(proj, proj, proj, proj, proj, cos, sin_signed)


def _nsa_kernel(qb_ref, qr_ref, kc_ref, vc_ref, ks_ref, vs_ref, kw_ref, vw_ref,
                ng_ref, nz_ref, ovl_ref, exp_ref, o_ref,
                m_ref, l_ref, acc_ref, *, tq, tk, n_slc):
    i = pl.program_id(2)
    q0 = i * tq
    pos_col = q0 + lax.broadcasted_iota(jnp.int32, (tq, 1), 0)

    kc = kc_ref[...].astype(BF16)
    vc = vc_ref[...].astype(BF16)
    n_cmp = kc.shape[0]
    cmp_end = CMP_STRIDE * lax.broadcasted_iota(jnp.int32, (1, n_cmp), 1) + (CMP_BLOCK - 1)
    cmask = cmp_end <= pos_col
    ovl = ovl_ref[...]
    imp = jnp.zeros((tq, LANES), F32)
    o_cmp = []
    for r in range(NSA_REP):
        q = qb_ref[:, r * HEAD_DIM:(r + 1) * HEAD_DIM]
        s = lax.dot_general(q, kc, (((1,), (1,)), ((), ())), preferred_element_type=F32) * SCALE
        s = jnp.where(cmask, s, NEG)
        e = jnp.exp(s - jnp.max(s, axis=-1, keepdims=True))
        p = e / jnp.sum(e, axis=-1, keepdims=True)
        p = jnp.where(cmask, p, 0.0).astype(BF16)
        o_cmp.append(jnp.dot(p, vc, preferred_element_type=F32))
        imp = imp + jnp.dot(p, ovl, preferred_element_type=F32)

    val = jnp.transpose(imp)[0:n_slc, :]
    pos_row = q0 + lax.broadcasted_iota(jnp.int32, (1, tq), 1)
    blk = lax.broadcasted_iota(jnp.int32, (n_slc, 1), 0)
    cur = pos_row // SLC_BLOCK
    future = blk * SLC_BLOCK > pos_row
    forced = (blk == 0) | (blk == cur) | (blk == cur - 1)
    val = jnp.where(future, -jnp.inf, jnp.where(forced, jnp.inf, val))
    rank = jnp.zeros((n_slc, tq), jnp.int32)
    for j in range(n_slc):
        vj = val[j:j + 1, :]
        beats = (vj > val) | ((vj == val) & (blk > j))
        rank = rank + beats.astype(jnp.int32)
    sel_t = (rank < min(SLC_TOP_N, n_slc)).astype(F32)
    sel_t = jnp.concatenate([sel_t, jnp.zeros((LANES - n_slc, tq), F32)], axis=0)
    sel = jnp.transpose(sel_t).astype(BF16)

    def flash(k_ref, v_ref, lo, hi, width, mask_fn):
        m_ref[...] = jnp.full(m_ref.shape, NEG, F32)
        l_ref[...] = jnp.zeros(l_ref.shape, F32)
        acc_ref[...] = jnp.zeros(acc_ref.shape, F32)

        def body(c, carry):
            start = pl.multiple_of(c * width, width)
            kt = k_ref[pl.ds(start, width), :]
            vt = v_ref[pl.ds(start, width), :]
            mask = mask_fn(c, start)
            for r in range(NSA_REP):
                q = qr_ref[:, r * HEAD_DIM:(r + 1) * HEAD_DIM]
                s = lax.dot_general(q, kt, (((1,), (1,)), ((), ())),
                                    preferred_element_type=F32) * SCALE
                s = jnp.where(mask, s, NEG)
                m_old = m_ref[r]
                m_new = jnp.maximum(m_old, jnp.max(s, axis=-1, keepdims=True))
                alpha = jnp.exp(m_old - m_new)
                e = jnp.where(mask, jnp.exp(s - m_new), 0.0)
                l_ref[r] = alpha * l_ref[r] + jnp.sum(e, axis=-1, keepdims=True)
                acc_ref[r] = alpha * acc_ref[r] + jnp.dot(e.astype(BF16), vt,
                                                          preferred_element_type=F32)
                m_ref[r] = m_new
            return carry

        lax.fori_loop(lo, hi, body, 0)
        return [acc_ref[r] / l_ref[r] for r in range(NSA_REP)]

    def slc_mask(c, start):
        chosen = jnp.dot(sel, exp_ref[c], preferred_element_type=F32) > 0.5
        kpos = start + lax.broadcasted_iota(jnp.int32, (1, tk), 1)
        return chosen & (kpos <= pos_col)

    def win_mask(c, start):
        kpos = start + lax.broadcasted_iota(jnp.int32, (1, tq), 1)
        d = pos_col - kpos
        return (d >= 0) & (d < WINDOW)

    o_slc = flash(ks_ref, vs_ref, 0, (q0 + tq + tk - 1) // tk, tk, slc_mask)
    o_win = flash(kw_ref, vw_ref, jnp.maximum(i - WINDOW // tq, 0), i + 1, tq, win_mask)

    gate = _sigmoid(ng_ref[...])
    for r in range(NSA_REP):
        g_cmp = gate[:, r:r + 1]
        g_slc = gate[:, NSA_REP + r:NSA_REP + r + 1]
        g_win = gate[:, 2 * NSA_REP + r:2 * NSA_REP + r + 1]
        o = g_cmp * o_cmp[r] + g_slc * o_slc[r] + g_win * o_win[r]
        sl = slice(r * HEAD_DIM, (r + 1) * HEAD_DIM)
        o_ref[:, sl] = (o * _silu(nz_ref[:, sl])).astype(o_ref.dtype)


def _nsa_attention(qb, qr, cmpkv, ks, vs, kw, vw, ng, proj, ovl, expand,
                   bsz, seq, nz_blk, tq=256, tk=512):
    g = NSA_KV_GROUPS
    gw = NSA_REP * HEAD_DIM
    nq = seq // tq
    n_slc = seq // SLC_BLOCK
    n_half = cmpkv.shape[-2]
    rowq = lambda cb: pl.BlockSpec((tq, gw), lambda b, gi, i: (b * nq + i, cb + gi))
    kvs = pl.BlockSpec((seq, HEAD_DIM), lambda b, gi, i: (b, gi))
    cmps = lambda c: pl.BlockSpec((None, None, None, n_half, HEAD_DIM),
                                  lambda b, gi, i: (b, c, gi, 0, 0))
    return pl.pallas_call(
        functools.partial(_nsa_kernel, tq=tq, tk=tk, n_slc=n_slc),
        grid=(bsz, g, nq),
        in_specs=[rowq(0), rowq(0), cmps(0), cmps(1), kvs, kvs, kvs, kvs,
                  pl.BlockSpec((tq, LANES), lambda b, gi, i: (b * nq + i, gi)),
                  rowq(nz_blk),
                  pl.BlockSpec((n_half, LANES), lambda b, gi, i: (0, 0)),
                  pl.BlockSpec((seq // tk, LANES, tk), lambda b, gi, i: (0, 0, 0))],
        out_specs=rowq(0),
        out_shape=jax.ShapeDtypeStruct((bsz * seq, NSA_HEADS * HEAD_DIM), BF16),
        scratch_shapes=[pltpu.VMEM((NSA_REP, tq, 1), F32),
                        pltpu.VMEM((NSA_REP, tq, 1), F32),
                        pltpu.VMEM((NSA_REP, tq, HEAD_DIM), F32)],
        compiler_params=_cparams("parallel", "parallel", "arbitrary"),
        name="nsa_attention",
    )(qb, qr, cmpkv, cmpkv, ks, vs, kw, vw, ng, proj, ovl, expand)


def _merge_kernel(ya_ref, yb_ref, yx_ref, wa_ref, wb_ref, wx_ref, g0_ref, g1_ref, g2_ref, o_ref):
    u = _sigmoid(g0_ref[...]) * jnp.dot(ya_ref[...], wa_ref[...], preferred_element_type=F32)
    u = u + _sigmoid(g1_ref[...]) * jnp.dot(yb_ref[...], wb_ref[...], preferred_element_type=F32)
    u = u + _sigmoid(g2_ref[...]) * jnp.dot(yx_ref[...], wx_ref[...], preferred_element_type=F32)
    o_ref[...] = u.astype(o_ref.dtype)


def _merge(ya, yb, yx, wa, wb, wx, proj, l, mg_off, d_model, tm=512, tn=512):
    t = ya.shape[0]
    nj = d_model // tn
    mgb = mg_off // tn
    act = lambda w: pl.BlockSpec((tm, w), lambda i, j: (i, 0))
    wsp = lambda w: pl.BlockSpec((None, w, tn), lambda i, j: (l, 0, j))
    gsp = lambda br: pl.BlockSpec((tm, tn), lambda i, j: (i, mgb + br * nj + j))
    return pl.pallas_call(
        _merge_kernel,
        grid=(t // tm, nj),
        in_specs=[act(ya.shape[1]), act(yb.shape[1]), act(yx.shape[1]),
                  wsp(wa.shape[1]), wsp(wb.shape[1]), wsp(wx.shape[1]),
                  gsp(0), gsp(1), gsp(2)],
        out_specs=pl.BlockSpec((tm, tn), lambda i, j: (i, j)),
        out_shape=jax.ShapeDtypeStruct((t, d_model), BF16),
        compiler_params=_cparams("parallel", "parallel"),
        name="gated_merge",
    )(ya, yb, yx, wa, wb, wx, proj, proj, proj)


def _rope_tables(seq):
    half = HEAD_DIM // 2
    inv_freq = ROPE_THETA ** (-jnp.arange(half, dtype=F32) / half)
    ang = jnp.arange(seq, dtype=jnp.int32).astype(F32)[:, None] * inv_freq[None, :]
    cos, sin = jnp.cos(ang), jnp.sin(ang)
    return jnp.concatenate([cos, cos], axis=-1), jnp.concatenate([-sin, sin], axis=-1)


def _overlap_table(seq, n_half):
    n_slc = seq // SLC_BLOCK
    cmp_start = np.arange(n_half) * CMP_STRIDE
    slc_start = np.arange(n_slc) * SLC_BLOCK
    ovl = ((cmp_start[:, None] < slc_start[None, :] + SLC_BLOCK)
           & (cmp_start[:, None] + CMP_BLOCK > slc_start[None, :]))
    ovl = ovl & (cmp_start[:, None] + CMP_BLOCK <= seq)
    out = np.zeros((n_half, LANES), np.float32)
    out[:, :n_slc] = ovl
    return jnp.asarray(out, BF16)


def _expand_table(seq, tk):
    key_blk = np.arange(seq) // SLC_BLOCK
    e = (np.arange(LANES)[:, None] == key_blk[None, :]).astype(np.float32)
    e = e.reshape(LANES, seq // tk, tk).transpose(1, 0, 2)
    return jnp.asarray(e, BF16)


def kernel(x, mem, norm_g, w_in, conv_w, conv_b, cmp_pos, cmp_w1, cmp_w2, mem_norm_g, w_mem_kv,
           w_up_a, w_up_b, w_up_x, w_out, final_g):
    bsz, seq, d_model = x.shape
    mem_len = mem.shape[1]
    depth = w_in.shape[0]
    sc_w = conv_w.shape[-1]
    nsa_w = NSA_HEADS * HEAD_DIM
    kv_w = NSA_BRANCHES * 2 * NSA_KV_GROUPS * HEAD_DIM
    ng_w = NSA_BRANCHES * NSA_HEADS
    x_w = X_HEADS * HEAD_DIM
    assert seq % 512 == 0 and seq // SLC_BLOCK <= LANES and sc_w % 256 == 0

    q_off = 4 * sc_w
    kv_off = q_off + nsa_w
    ng_src = kv_off + kv_w
    nz_off = kv_off + kv_w
    xq_off = nz_off + nsa_w
    xz_off = xq_off + x_w
    mg_off = xz_off + x_w

    w_main = jnp.concatenate([w_in[..., :ng_src], w_in[..., ng_src + ng_w:]], axis=-1).astype(BF16)
    src = np.zeros((NSA_KV_GROUPS * LANES,), np.int32)
    valid = np.zeros((NSA_KV_GROUPS * LANES,), bool)
    for gi in range(NSA_KV_GROUPS):
        for br in range(NSA_BRANCHES):
            for r in range(NSA_REP):
                col = gi * LANES + br * NSA_REP + r
                src[col] = ng_src + (gi * NSA_REP + r) * NSA_BRANCHES + br
                valid[col] = True
    w_ng = jnp.where(jnp.asarray(valid)[None, None, :], jnp.take(w_in, jnp.asarray(src), axis=-1),
                     0.0).astype(BF16)
    w_memkv_b = w_mem_kv.astype(BF16)
    cmp_w1_b = cmp_w1.astype(BF16)
    cmp_w2_b = cmp_w2.astype(BF16)
    w_up_a_b = w_up_a.astype(BF16)
    w_up_b_b = w_up_b.astype(BF16)
    w_up_x_b = w_up_x.astype(BF16)
    w_out_b = w_out.astype(BF16)
    norm_g3 = norm_g.reshape(depth, 1, d_model)
    mem_norm_g3 = mem_norm_g.reshape(depth, 1, d_model)
    conv_b3 = conv_b.reshape(depth, 1, sc_w)

    cos, sin_signed = _rope_tables(seq)
    n_half = seq // CMP_STRIDE
    tk_slc = 512
    ovl = _overlap_table(seq, n_half)
    expand = _expand_table(seq, tk_slc)

    t = bsz * seq
    xf = x.reshape(t, d_model)
    memf = mem.reshape(bsz * mem_len, d_model)

    for l in range(depth):
        h = _rmsnorm(xf, norm_g3, l, BF16)
        proj = _matmul(h, w_main, l, F32, 1024, 1024, 1024, name="in_proj")
        ng = _matmul(h, w_ng, l, F32, 1024, 512, 1024, name="gate_proj")

        y_a = _conv_branch(proj, conv_w, conv_b3, l, bsz, seq, sc_w)

        hm = _rmsnorm(memf, mem_norm_g3, l, BF16)
        memkv = _matmul(hm, w_memkv_b, l, F32, 512, 1024, 1024, name="mem_kv_proj")
        y_x = _mem_attention(proj, memkv, bsz, seq, mem_len, xq_off // HEAD_DIM, xz_off // HEAD_DIM)

        cmpkv = _compress(proj, cmp_pos, cmp_w1_b, cmp_w2_b, l, bsz, seq, kv_off // HEAD_DIM)
        qb, qr, ks, vs, kw, vw = _nsa_prep(proj, cos, sin_signed, bsz, seq, q_off, kv_off)
        y_b = _nsa_attention(qb, qr, cmpkv, ks, vs, kw, vw, ng, proj, ovl, expand,
                             bsz, seq, nz_off // (NSA_REP * HEAD_DIM), tk=tk_slc)

        u = _merge(y_a, y_b, y_x, w_up_a_b, w_up_b_b, w_up_x_b, proj, l, mg_off, d_model)
        xf = _matmul(u, w_out_b, l, F32, 1024, 1024, 1024, residual=xf, name="out_proj")

    out = _rmsnorm(xf, final_g.reshape(1, 1, d_model), 0, F32)
    return out.reshape(bsz, seq, d_model)
```

```python
import functools

import numpy as np
import jax
import jax.numpy as jnp
from jax import lax
from jax.experimental import pallas as pl
from jax.experimental.pallas import tpu as pltpu

F32 = jnp.float32
BF16 = jnp.bfloat16

HEAD_DIM = 128
ROPE_THETA = 10000.0
EPS = 1e-6
NEG = -1e30
SC_KERNEL = 3
NSA_HEADS = 16
NSA_KV_GROUPS = 4
NSA_REP = NSA_HEADS // NSA_KV_GROUPS
NSA_BRANCHES = 3
CMP_BLOCK = 32
CMP_STRIDE = 16
SLC_BLOCK = 64
SLC_TOP_N = 16
WINDOW = 512
X_HEADS = 4
N_BRANCHES = 3
SCALE = HEAD_DIM ** -0.5
LOG2E = 1.4426950408889634

V7X_VMEM_BYTES = 64 * 1024 * 1024
VMEM_LIMIT_BYTES = V7X_VMEM_BYTES - 8 * 1024 * 1024
LANES = 128

NSA_TQ = 256
NSA_TK = 512


def _cparams(*sem):
    return pltpu.CompilerParams(dimension_semantics=sem, vmem_limit_bytes=VMEM_LIMIT_BYTES)


def _silu(x):
    return x / (1.0 + jnp.exp(-x))


def _sigmoid(x):
    return 1.0 / (1.0 + jnp.exp(-x))


def _cast_kernel(x_ref, o_ref):
    o_ref[...] = x_ref[...].astype(o_ref.dtype)


def _stage_aligned(w, n_cols, tr=512, tc=1024):
    depth, rows, _ = w.shape
    return pl.pallas_call(
        _cast_kernel,
        grid=(depth, rows // tr, n_cols // tc),
        in_specs=[pl.BlockSpec((None, tr, tc), lambda l, i, j: (l, i, j))],
        out_specs=pl.BlockSpec((None, tr, tc), lambda l, i, j: (l, i, j)),
        out_shape=jax.ShapeDtypeStruct((depth, rows, n_cols), BF16),
        compiler_params=_cparams("parallel", "parallel", "parallel"),
        name="stage_w_lo",
    )(w)


def _shift_cast_kernel(a_ref, b_ref, o_ref, *, shift, tc):
    lane = lax.broadcasted_iota(jnp.int32, (a_ref.shape[0], LANES), 1)
    n = tc // LANES
    for c in range(n):
        cur = a_ref[:, c * LANES:(c + 1) * LANES]
        nxt = a_ref[:, (c + 1) * LANES:(c + 2) * LANES] if c + 1 < n else b_ref[...]
        merged = jnp.where(lane >= shift, cur, nxt)
        o_ref[:, c * LANES:(c + 1) * LANES] = pltpu.roll(merged, LANES - shift, 1).astype(o_ref.dtype)


def _stage_shifted(w, col0, shift, n_cols, tr=512, tc=1024):
    depth, rows, _ = w.shape
    jb = col0 // tc
    per = tc // LANES
    return pl.pallas_call(
        functools.partial(_shift_cast_kernel, shift=shift, tc=tc),
        grid=(depth, rows // tr, n_cols // tc),
        in_specs=[pl.BlockSpec((None, tr, tc), lambda l, i, j: (l, i, jb + j)),
                  pl.BlockSpec((None, tr, LANES), lambda l, i, j: (l, i, (jb + j + 1) * per))],
        out_specs=pl.BlockSpec((None, tr, tc), lambda l, i, j: (l, i, j)),
        out_shape=jax.ShapeDtypeStruct((depth, rows, n_cols), BF16),
        compiler_params=_cparams("parallel", "parallel", "parallel"),
        name="stage_w_hi",
    )(w, w)


def _rmsnorm_kernel(x_ref, g_ref, o_ref):
    x = x_ref[...]
    ms = jnp.mean(x * x, axis=-1, keepdims=True)
    o_ref[...] = (x * lax.rsqrt(ms + EPS) * g_ref[...]).astype(o_ref.dtype)


def _rmsnorm(x, g3, l, out_dtype, tm=256):
    m, d = x.shape
    return pl.pallas_call(
        _rmsnorm_kernel,
        grid=(m // tm,),
        in_specs=[pl.BlockSpec((tm, d), lambda i: (i, 0)),
                  pl.BlockSpec((None, 1, d), lambda i: (l, 0, 0))],
        out_specs=pl.BlockSpec((tm, d), lambda i: (i, 0)),
        out_shape=jax.ShapeDtypeStruct((m, d), out_dtype),
        compiler_params=_cparams("parallel"),
        name="rmsnorm",
    )(x, g3)


def _mm_kernel(a_ref, b_ref, o_ref):
    o_ref[...] = jnp.dot(a_ref[...], b_ref[...], preferred_element_type=F32).astype(o_ref.dtype)


def _mm_res_kernel(a_ref, b_ref, r_ref, o_ref):
    acc = jnp.dot(a_ref[...], b_ref[...], preferred_element_type=F32)
    o_ref[...] = (r_ref[...] + acc).astype(o_ref.dtype)


def _matmul(a, b, l, out_dtype, tm, tn, residual=None, name="matmul"):
    m, kdim = a.shape
    n = b.shape[-1]
    tm, tn = min(tm, m), min(tn, n)
    in_specs = [pl.BlockSpec((tm, kdim), lambda i, j: (i, 0)),
                pl.BlockSpec((None, kdim, tn), lambda i, j: (l, 0, j))]
    args = [a, b]
    body = _mm_kernel
    if residual is not None:
        body = _mm_res_kernel
        in_specs.append(pl.BlockSpec((tm, tn), lambda i, j: (i, j)))
        args.append(residual)
    return pl.pallas_call(
        body,
        grid=(m // tm, n // tn),
        in_specs=in_specs,
        out_specs=pl.BlockSpec((tm, tn), lambda i, j: (i, j)),
        out_shape=jax.ShapeDtypeStruct((m, n), out_dtype),
        compiler_params=_cparams("parallel", "arbitrary"),
        name=name,
    )(*args)


def _conv_kernel(h_ref, b_ref, c_ref, z_ref, w_ref, cb_ref, o_ref):
    u = c_ref[...] * h_ref[...]
    row = lax.broadcasted_iota(jnp.int32, u.shape, 0)
    u1 = jnp.where(row >= 1, pltpu.roll(u, 1, 0), 0.0)
    u2 = jnp.where(row >= 2, pltpu.roll(u, 2, 0), 0.0)
    w = w_ref[...]
    y = cb_ref[...] + w[0:1, :] * u2
    y = y + w[1:2, :] * u1
    y = y + w[2:3, :] * u
    o_ref[...] = (b_ref[...] * y * _silu(z_ref[...])).astype(o_ref.dtype)


def _conv_branch(proj, conv_w, conv_b3, l, bsz, seq, width, tc=256):
    nb = width // tc
    blk = lambda off: pl.BlockSpec((seq, tc), lambda b, j: (b, off * nb + j))
    return pl.pallas_call(
        _conv_kernel,
        grid=(bsz, nb),
        in_specs=[blk(0), blk(1), blk(2), blk(3),
                  pl.BlockSpec((None, SC_KERNEL, tc), lambda b, j: (l, 0, j)),
                  pl.BlockSpec((None, 1, tc), lambda b, j: (l, 0, j))],
        out_specs=pl.BlockSpec((seq, tc), lambda b, j: (b, j)),
        out_shape=jax.ShapeDtypeStruct((bsz * seq, width), BF16),
        compiler_params=_cparams("parallel", "parallel"),
        name="conv_branch",
    )(proj, proj, proj, proj, conv_w, conv_b3)


def _memattn_kernel(q_ref, k_ref, v_ref, z_ref, o_ref):
    q = q_ref[...].astype(BF16)
    k = k_ref[...].astype(BF16)
    v = v_ref[...].astype(BF16)
    s = lax.dot_general(q, k, (((1,), (1,)), ((), ())), preferred_element_type=F32) * SCALE
    m = jnp.max(s, axis=-1, keepdims=True)
    e = jnp.exp(s - m)
    p = e / jnp.sum(e, axis=-1, keepdims=True)
    o = jnp.dot(p.astype(BF16), v, preferred_element_type=F32)
    o_ref[...] = (o * _silu(z_ref[...])).astype(o_ref.dtype)


def _mem_attention(proj, memkv, bsz, seq, mem_len, q_blk, z_blk, tq=512):
    nq = seq // tq
    return pl.pallas_call(
        _memattn_kernel,
        grid=(bsz, X_HEADS, nq),
        in_specs=[pl.BlockSpec((tq, HEAD_DIM), lambda b, h, i: (b * nq + i, q_blk + h)),
                  pl.BlockSpec((mem_len, HEAD_DIM), lambda b, h, i: (b, h)),
                  pl.BlockSpec((mem_len, HEAD_DIM), lambda b, h, i: (b, X_HEADS + h)),
                  pl.BlockSpec((tq, HEAD_DIM), lambda b, h, i: (b * nq + i, z_blk + h))],
        out_specs=pl.BlockSpec((tq, HEAD_DIM), lambda b, h, i: (b * nq + i, h)),
        out_shape=jax.ShapeDtypeStruct((bsz * seq, X_HEADS * HEAD_DIM), BF16),
        compiler_params=_cparams("parallel", "parallel", "parallel"),
        name="mem_attention",
    )(proj, memkv, memkv, proj)


def _compress_kernel(kv_ref, pos_ref, w1_ref, w2_ref, o_ref, *, n_half):
    half = CMP_BLOCK // 2
    acc_a = jnp.zeros((n_half, HEAD_DIM), F32)
    acc_b = jnp.zeros((n_half, HEAD_DIM), F32)
    for t in range(half):
        x = kv_ref[pl.ds(t, n_half, stride=half), :]
        xa = (x + pos_ref[t:t + 1, :]).astype(BF16)
        xb = (x + pos_ref[half + t:half + t + 1, :]).astype(BF16)
        acc_a = acc_a + jnp.dot(xa, w1_ref[t * HEAD_DIM:(t + 1) * HEAD_DIM, :],
                                preferred_element_type=F32)
        acc_b = acc_b + jnp.dot(xb, w1_ref[(half + t) * HEAD_DIM:(half + t + 1) * HEAD_DIM, :],
                                preferred_element_type=F32)
    hid = _silu(acc_a + pltpu.roll(acc_b, n_half - 1, 0))
    o_ref[...] = jnp.dot(hid.astype(BF16), w2_ref[...], preferred_element_type=F32)


def _compress(proj, cmp_pos, cmp_w1, cmp_w2, l, bsz, seq, kv_blk):
    g = NSA_KV_GROUPS
    n_half = seq // (CMP_BLOCK // 2)
    return pl.pallas_call(
        functools.partial(_compress_kernel, n_half=n_half),
        grid=(bsz, 2, g),
        in_specs=[pl.BlockSpec((seq, HEAD_DIM), lambda b, c, gi: (b, kv_blk + c * g + gi)),
                  pl.BlockSpec((None, None, CMP_BLOCK, HEAD_DIM), lambda b, c, gi: (l, c, 0, 0)),
                  pl.BlockSpec((None, None, CMP_BLOCK * HEAD_DIM, HEAD_DIM),
                               lambda b, c, gi: (l, c, 0, 0)),
                  pl.BlockSpec((None, None, HEAD_DIM, HEAD_DIM), lambda b, c, gi: (l, c, 0, 0))],
        out_specs=pl.BlockSpec((None, None, None, n_half, HEAD_DIM),
                               lambda b, c, gi: (b, c, gi, 0, 0)),
        out_shape=jax.ShapeDtypeStruct((bsz, 2, g, n_half, HEAD_DIM), F32),
        compiler_params=_cparams("parallel", "parallel", "parallel"),
        name="nsa_compress",
    )(proj, cmp_pos, cmp_w1, cmp_w2)


def _rope_tile(x, cos, sin_signed):
    return x * cos + pltpu.roll(x, HEAD_DIM // 2, 1) * sin_signed


def _nsa_prep_kernel(q_ref, ks_ref, vs_ref, kw_ref, vw_ref, cos_ref, sin_ref,
                     qb_ref, qr_ref, kso_ref, vst_ref, kwo_ref, vwt_ref):
    cos = cos_ref[...]
    sin = sin_ref[...]
    for h in range(NSA_HEADS):
        sl = slice(h * HEAD_DIM, (h + 1) * HEAD_DIM)
        x = q_ref[:, sl]
        qb_ref[:, sl] = x.astype(BF16)
        qr_ref[:, sl] = _rope_tile(x, cos, sin).astype(BF16)
    n_win = vwt_ref.shape[0]
    w_win = vwt_ref.shape[-1]
    for g in range(NSA_KV_GROUPS):
        sl = slice(g * HEAD_DIM, (g + 1) * HEAD_DIM)
        kso_ref[:, sl] = _rope_tile(ks_ref[:, sl], cos, sin).astype(BF16)
        kwo_ref[:, sl] = _rope_tile(kw_ref[:, sl], cos, sin).astype(BF16)
        vst_ref[sl, :] = jnp.transpose(vs_ref[:, sl]).astype(BF16)
        vw_t = jnp.transpose(vw_ref[:, sl]).astype(BF16)
        for c in range(n_win):
            vwt_ref[c, sl, :] = vw_t[:, c * w_win:(c + 1) * w_win]


def _nsa_prep(proj, cos, sin_signed, bsz, seq, q_off, kv_off):
    tr = NSA_TK
    nsa_w = NSA_HEADS * HEAD_DIM
    gw = NSA_KV_GROUPS * HEAD_DIM
    nr = seq // tr
    t = bsz * seq
    qblk = q_off // nsa_w
    kvb = kv_off // gw
    n_win = tr // NSA_TQ
    rowq = lambda w, cb: pl.BlockSpec((tr, w), lambda b, i: (b * nr + i, cb))
    tab = pl.BlockSpec((tr, HEAD_DIM), lambda b, i: (i, 0))
    return pl.pallas_call(
        _nsa_prep_kernel,
        grid=(bsz, nr),
        in_specs=[rowq(nsa_w, qblk), rowq(gw, kvb + 2), rowq(gw, kvb + 3),
                  rowq(gw, kvb + 4), rowq(gw, kvb + 5), tab, tab],
        out_specs=[rowq(nsa_w, 0), rowq(nsa_w, 0), rowq(gw, 0),
                   pl.BlockSpec((None, None, gw, tr), lambda b, i: (b, i, 0, 0)),
                   rowq(gw, 0),
                   pl.BlockSpec((None, n_win, gw, NSA_TQ), lambda b, i: (b, i, 0, 0))],
        out_shape=[jax.ShapeDtypeStruct((t, nsa_w), BF16), jax.ShapeDtypeStruct((t, nsa_w), BF16),
                   jax.ShapeDtypeStruct((t, gw), BF16),
                   jax.ShapeDtypeStruct((bsz, nr, gw, tr), BF16),
                   jax.ShapeDtypeStruct((t, gw), BF16),
                   jax.ShapeDtypeStruct((bsz, seq // NSA_TQ, gw, NSA_TQ), BF16)],
        compiler_params=_cparams("parallel", "parallel"),
        name="nsa_prep",
    )(proj, proj, proj, proj, proj, cos, sin_signed)


def _col_reduce(x, op):
    rows = x.shape[0]
    while rows > 8:
        rows //= 2
        x = op(x[:rows], x[rows:])
    return x


def _col_max(x):
    return jnp.max(_col_reduce(x, jnp.maximum), axis=0, keepdims=True)


def _col_sum(x):
    return jnp.sum(_col_reduce(x, jnp.add), axis=0, keepdims=True)


def _nsa_kernel(qb_ref, qr_ref, kc_ref, vc_ref, ks_ref, vst_ref, kw_ref, vwt_ref,
                ng_ref, nz_ref, ovl_ref, exp_ref, o_ref,
                qa_ref, m_ref, l_ref, acc_ref, ocmp_ref, oslc_ref, ngt_ref, *, tq, tk, n_slc):
    gi = pl.program_id(1)
    i = pl.program_id(2)
    q0 = i * tq
    nt = (((1,), (1,)), ((), ()))
    pos_row = q0 + lax.broadcasted_iota(jnp.int32, (1, tq), 1)
    pos_all = jnp.concatenate([pos_row] * NSA_REP, axis=1)
    heads = [slice(r * HEAD_DIM, (r + 1) * HEAD_DIM) for r in range(NSA_REP)]
    cols = [slice(r * tq, (r + 1) * tq) for r in range(NSA_REP)]

    qb = jnp.concatenate([qb_ref[:, h] for h in heads], axis=0)
    kc = kc_ref[...].astype(BF16)
    vc_t = jnp.transpose(vc_ref[...]).astype(BF16)
    n_cmp = kc.shape[0]
    cmp_end = CMP_STRIDE * lax.broadcasted_iota(jnp.int32, (n_cmp, 1), 0) + (CMP_BLOCK - 1)
    cmask = cmp_end <= pos_all
    s = lax.dot_general(kc, qb, nt, preferred_element_type=F32) * SCALE
    s = jnp.where(cmask, s, NEG)
    e = jnp.exp(s - _col_max(s))
    p = e * (1.0 / _col_sum(e))
    p = jnp.where(cmask, p, 0.0).astype(BF16)
    ocmp_ref[...] = jnp.dot(vc_t, p, preferred_element_type=F32)
    imp_all = jnp.dot(ovl_ref[...], p, preferred_element_type=F32)
    imp = imp_all[:, cols[0]]
    for c in cols[1:]:
        imp = imp + imp_all[:, c]

    val = imp[0:n_slc, :]
    blk = lax.broadcasted_iota(jnp.int32, (n_slc, 1), 0)
    cur = pos_row // SLC_BLOCK
    future = blk * SLC_BLOCK > pos_row
    forced = (blk == 0) | (blk == cur) | (blk == cur - 1)
    val = jnp.where(future, -jnp.inf, jnp.where(forced, jnp.inf, val))
    rank = jnp.zeros((n_slc, tq), jnp.int32)
    for j in range(n_slc):
        vj = val[j:j + 1, :]
        beats = (vj > val) | ((vj == val) & (blk > j))
        rank = rank + beats.astype(jnp.int32)
    sel_bias = jnp.where(rank < min(SLC_TOP_N, n_slc), 0.0, NEG)
    sel_bias = jnp.concatenate([sel_bias, jnp.zeros((LANES - n_slc, tq), F32)], axis=0)
    sel_bias = jnp.transpose(sel_bias).astype(BF16)
    qa_ref[:, 0:HEAD_DIM] = jnp.concatenate([qr_ref[:, h] for h in heads], axis=0)
    qa_ref[:, HEAD_DIM:] = jnp.concatenate([sel_bias] * NSA_REP, axis=0)

    def reset():
        m_ref[...] = jnp.full(m_ref.shape, NEG, F32)
        l_ref[...] = jnp.zeros(l_ref.shape, F32)
        acc_ref[...] = jnp.zeros(acc_ref.shape, F32)

    def update(keys, queries, vt, visible):
        t = lax.dot_general(keys, queries, nt, preferred_element_type=F32) * (SCALE * LOG2E)
        if visible is not None:
            t = jnp.where(visible, t, NEG)
        m_old = m_ref[...]
        m_new = jnp.maximum(m_old, _col_max(t))
        alpha = jnp.exp2(m_old - m_new)
        e = jnp.exp2(t - m_new)
        l_ref[...] = alpha * l_ref[...] + _col_sum(e)
        acc_ref[...] = alpha * acc_ref[...] + jnp.dot(vt, e.astype(BF16),
                                                      preferred_element_type=F32)
        m_ref[...] = m_new

    def key_pos(c, width):
        return c * width + lax.broadcasted_iota(jnp.int32, (width, 1), 0)

    def slc_tile(c, causal):
        start = pl.multiple_of(c * tk, tk)
        keys = jnp.concatenate([ks_ref[pl.ds(start, tk), :], exp_ref[c]], axis=1)
        update(keys, qa_ref[...], vst_ref[c], (key_pos(c, tk) <= pos_all) if causal else None)

    n_tiles = (q0 + tq + tk - 1) // tk
    reset()
    lax.fori_loop(0, n_tiles - 1, lambda c, carry: (slc_tile(c, False), carry)[1], 0)
    slc_tile(n_tiles - 1, True)
    oslc_ref[...] = acc_ref[...] * (1.0 / l_ref[...])

    def win_tile(c, visible):
        start = pl.multiple_of(c * tq, tq)
        update(kw_ref[pl.ds(start, tq), :], qa_ref[:, 0:HEAD_DIM], vwt_ref[c], visible(c))

    assert WINDOW == 2 * tq
    reset()
    win_tile(i, lambda c: key_pos(c, tq) <= pos_all)

    @pl.when(i >= 1)
    def _():
        win_tile(i - 1, lambda c: None)

    @pl.when(i >= 2)
    def _():
        win_tile(i - 2, lambda c: key_pos(c, tq) > pos_all - WINDOW)

    ngt_ref[...] = jnp.transpose(ng_ref[...])
    inv_l = 1.0 / l_ref[...]
    for r in range(NSA_REP):
        row = (gi * NSA_REP + r) * NSA_BRANCHES
        g_cmp = _sigmoid(ngt_ref[pl.ds(row, 1), :])
        g_slc = _sigmoid(ngt_ref[pl.ds(row + 1, 1), :])
        g_win = _sigmoid(ngt_ref[pl.ds(row + 2, 1), :])
        o_win = acc_ref[:, cols[r]] * inv_l[:, cols[r]]
        o_t = g_cmp * ocmp_ref[:, cols[r]] + g_slc * oslc_ref[:, cols[r]] + g_win * o_win
        o_ref[:, heads[r]] = (jnp.transpose(o_t) * _silu(nz_ref[:, heads[r]])).astype(o_ref.dtype)


def _nsa_attention(qb, qr, cmpkv, ks, vst, kw, vwt, ng, proj, ovl, expand, bsz, seq, nz_blk):
    tq, tk = NSA_TQ, NSA_TK
    g = NSA_KV_GROUPS
    gw = NSA_REP * HEAD_DIM
    nq = seq // tq
    n_slc = seq // SLC_BLOCK
    n_half = cmpkv.shape[-2]
    rowq = lambda cb: pl.BlockSpec((tq, gw), lambda b, gi, i: (b * nq + i, cb + gi))
    kvs = pl.BlockSpec((seq, HEAD_DIM), lambda b, gi, i: (b, gi))
    cmps = lambda c: pl.BlockSpec((None, None, None, n_half, HEAD_DIM),
                                  lambda b, gi, i: (b, c, gi, 0, 0))
    vts = lambda n, w: pl.BlockSpec((None, n, HEAD_DIM, w), lambda b, gi, i: (b, 0, gi, 0))
    return pl.pallas_call(
        functools.partial(_nsa_kernel, tq=tq, tk=tk, n_slc=n_slc),
        grid=(bsz, g, nq),
        in_specs=[rowq(0), rowq(0), cmps(0), cmps(1),
                  kvs, vts(seq // tk, tk), kvs, vts(seq // tq, tq),
                  pl.BlockSpec((tq, LANES), lambda b, gi, i: (b * nq + i, 0)),
                  rowq(nz_blk),
                  pl.BlockSpec((LANES, n_half), lambda b, gi, i: (0, 0)),
                  pl.BlockSpec((seq // tk, tk, LANES), lambda b, gi, i: (0, 0, 0))],
        out_specs=rowq(0),
        out_shape=jax.ShapeDtypeStruct((bsz * seq, NSA_HEADS * HEAD_DIM), BF16),
        scratch_shapes=[pltpu.VMEM((NSA_REP * tq, 2 * HEAD_DIM), BF16),
                        pltpu.VMEM((1, NSA_REP * tq), F32),
                        pltpu.VMEM((1, NSA_REP * tq), F32),
                        pltpu.VMEM((HEAD_DIM, NSA_REP * tq), F32),
                        pltpu.VMEM((HEAD_DIM, NSA_REP * tq), F32),
                        pltpu.VMEM((HEAD_DIM, NSA_REP * tq), F32),
                        pltpu.VMEM((LANES, tq), F32)],
        compiler_params=_cparams("parallel", "parallel", "arbitrary"),
        name="nsa_attention",
    )(qb, qr, cmpkv, cmpkv, ks, vst, kw, vwt, ng, proj, ovl, expand)


def _merge_kernel(ya_ref, yb_ref, yx_ref, wa_ref, wb_ref, wx_ref, g0_ref, g1_ref, g2_ref, o_ref):
    u = _sigmoid(g0_ref[...]) * jnp.dot(ya_ref[...], wa_ref[...], preferred_element_type=F32)
    u = u + _sigmoid(g1_ref[...]) * jnp.dot(yb_ref[...], wb_ref[...], preferred_element_type=F32)
    u = u + _sigmoid(g2_ref[...]) * jnp.dot(yx_ref[...], wx_ref[...], preferred_element_type=F32)
    o_ref[...] = u.astype(o_ref.dtype)


def _merge(ya, yb, yx, wa, wb, wx, proj, l, mg_off, d_model, tm=512, tn=512):
    t = ya.shape[0]
    nj = d_model // tn
    mgb = mg_off // tn
    act = lambda w: pl.BlockSpec((tm, w), lambda i, j: (i, 0))
    wsp = lambda w: pl.BlockSpec((None, w, tn), lambda i, j: (l, 0, j))
    gsp = lambda br: pl.BlockSpec((tm, tn), lambda i, j: (i, mgb + br * nj + j))
    return pl.pallas_call(
        _merge_kernel,
        grid=(t // tm, nj),
        in_specs=[act(ya.shape[1]), act(yb.shape[1]), act(yx.shape[1]),
                  wsp(wa.shape[1]), wsp(wb.shape[1]), wsp(wx.shape[1]),
                  gsp(0), gsp(1), gsp(2)],
        out_specs=pl.BlockSpec((tm, tn), lambda i, j: (i, j)),
        out_shape=jax.ShapeDtypeStruct((t, d_model), BF16),
        compiler_params=_cparams("parallel", "parallel"),
        name="gated_merge",
    )(ya, yb, yx, wa, wb, wx, proj, proj, proj)


def _rope_tables(seq):
    half = HEAD_DIM // 2
    inv_freq = ROPE_THETA ** (-jnp.arange(half, dtype=F32) / half)
    ang = jnp.arange(seq, dtype=jnp.int32).astype(F32)[:, None] * inv_freq[None, :]
    cos, sin = jnp.cos(ang), jnp.sin(ang)
    return jnp.concatenate([cos, cos], axis=-1), jnp.concatenate([-sin, sin], axis=-1)


def _overlap_table(seq, n_half):
    n_slc = seq // SLC_BLOCK
    cmp_start = np.arange(n_half) * CMP_STRIDE
    slc_start = np.arange(n_slc) * SLC_BLOCK
    ovl = ((cmp_start[None, :] < slc_start[:, None] + SLC_BLOCK)
           & (cmp_start[None, :] + CMP_BLOCK > slc_start[:, None]))
    ovl = ovl & (cmp_start[None, :] + CMP_BLOCK <= seq)
    out = np.zeros((LANES, n_half), np.float32)
    out[:n_slc, :] = ovl
    return jnp.asarray(out, BF16)


def _expand_table(seq, tk):
    key_blk = np.arange(seq) // SLC_BLOCK
    e = (key_blk[:, None] == np.arange(LANES)[None, :]).astype(np.float32)
    return jnp.asarray(e.reshape(seq // tk, tk, LANES), BF16)


def kernel(x, mem, norm_g, w_in, conv_w, conv_b, cmp_pos, cmp_w1, cmp_w2, mem_norm_g, w_mem_kv,
           w_up_a, w_up_b, w_up_x, w_out, final_g):
    bsz, seq, d_model = x.shape
    mem_len = mem.shape[1]
    depth = w_in.shape[0]
    sc_w = conv_w.shape[-1]
    nsa_w = NSA_HEADS * HEAD_DIM
    kv_w = NSA_BRANCHES * 2 * NSA_KV_GROUPS * HEAD_DIM
    ng_w = NSA_BRANCHES * NSA_HEADS
    x_w = X_HEADS * HEAD_DIM
    stage_tc = 1024

    q_off = 4 * sc_w
    kv_off = q_off + nsa_w
    lo_w = kv_off + kv_w
    nz_off = 0
    xq_off = nz_off + nsa_w
    xz_off = xq_off + x_w
    mg_off = xz_off + x_w
    hi_w = mg_off + N_BRANCHES * d_model
    assert seq % NSA_TK == 0 and seq // SLC_BLOCK <= LANES and sc_w % 256 == 0
    assert lo_w % stage_tc == 0 and hi_w % stage_tc == 0 and 0 < ng_w < LANES
    assert lo_w + ng_w + hi_w == w_in.shape[-1]

    w_lo = _stage_aligned(w_in, lo_w, tc=stage_tc)
    w_hi = _stage_shifted(w_in, lo_w, ng_w, hi_w, tc=stage_tc)
    w_ng = w_in[..., lo_w:lo_w + LANES].astype(BF16)
    w_memkv_b = w_mem_kv.astype(BF16)
    cmp_w1_b = cmp_w1.astype(BF16)
    cmp_w2_b = cmp_w2.astype(BF16)
    w_up_a_b = w_up_a.astype(BF16)
    w_up_b_b = w_up_b.astype(BF16)
    w_up_x_b = w_up_x.astype(BF16)
    w_out_b = w_out.astype(BF16)
    norm_g3 = norm_g.reshape(depth, 1, d_model)
    mem_norm_g3 = mem_norm_g.reshape(depth, 1, d_model)
    conv_b3 = conv_b.reshape(depth, 1, sc_w)

    cos, sin_signed = _rope_tables(seq)
    n_half = seq // CMP_STRIDE
    ovl = _overlap_table(seq, n_half)
    expand = _expand_table(seq, NSA_TK)

    t = bsz * seq
    xf = x.reshape(t, d_model)
    memf = mem.reshape(bsz * mem_len, d_model)

    for l in range(depth):
        h = _rmsnorm(xf, norm_g3, l, BF16)
        proj_lo = _matmul(h, w_lo, l, F32, 1024, 1024, name="in_proj_lo")
        proj_hi = _matmul(h, w_hi, l, F32, 1024, 1024, name="in_proj_hi")
        ng = _matmul(h, w_ng, l, F32, 1024, LANES, name="gate_proj")

        y_a = _conv_branch(proj_lo, conv_w, conv_b3, l, bsz, seq, sc_w)

        hm = _rmsnorm(memf, mem_norm_g3, l, BF16)
        memkv = _matmul(hm, w_memkv_b, l, F32, 512, 1024, name="mem_kv_proj")
        y_x = _mem_attention(proj_hi, memkv, bsz, seq, mem_len, xq_off // HEAD_DIM, xz_off // HEAD_DIM)

        cmpkv = _compress(proj_lo, cmp_pos, cmp_w1_b, cmp_w2_b, l, bsz, seq, kv_off // HEAD_DIM)
        qb, qr, ks, vst, kw, vwt = _nsa_prep(proj_lo, cos, sin_signed, bsz, seq, q_off, kv_off)
        y_b = _nsa_attention(qb, qr, cmpkv, ks, vst, kw, vwt, ng, proj_hi, ovl, expand,
                             bsz, seq, nz_off // (NSA_REP * HEAD_DIM))

        u = _merge(y_a, y_b, y_x, w_up_a_b, w_up_b_b, w_up_x_b, proj_hi, l, mg_off, d_model)
        xf = _matmul(u, w_out_b, l, F32, 1024, 1024, residual=xf, name="out_proj")

    out = _rmsnorm(xf, final_g.reshape(1, 1, d_model), 0, F32)
    return out.reshape(bsz, seq, d_model)
```

```python
import functools

import numpy as np
import jax
import jax.numpy as jnp
from jax import lax
from jax.experimental import pallas as pl
from jax.experimental.pallas import tpu as pltpu

F32 = jnp.float32
BF16 = jnp.bfloat16

HEAD_DIM = 128
ROPE_THETA = 10000.0
EPS = 1e-6
NEG = -1e30
SC_KERNEL = 3
NSA_HEADS = 16
NSA_KV_GROUPS = 4
NSA_REP = NSA_HEADS // NSA_KV_GROUPS
NSA_BRANCHES = 3
CMP_BLOCK = 32
CMP_STRIDE = 16
SLC_BLOCK = 64
SLC_TOP_N = 16
WINDOW = 512
X_HEADS = 4
N_BRANCHES = 3
SCALE = HEAD_DIM ** -0.5
LOG2E = 1.4426950408889634

V7X_VMEM_BYTES = 64 * 1024 * 1024
VMEM_LIMIT_BYTES = V7X_VMEM_BYTES - 8 * 1024 * 1024
LANES = 128

NSA_TQ = 512
NSA_TK = 256


def _cparams(*sem):
    return pltpu.CompilerParams(dimension_semantics=sem, vmem_limit_bytes=VMEM_LIMIT_BYTES)


def _silu(x):
    return x / (1.0 + jnp.exp(-x))


def _sigmoid(x):
    return 1.0 / (1.0 + jnp.exp(-x))


def _rmsnorm_kernel(x_ref, g_ref, o_ref):
    x = x_ref[...]
    ms = jnp.mean(x * x, axis=-1, keepdims=True)
    o_ref[...] = (x * lax.rsqrt(ms + EPS) * g_ref[...]).astype(o_ref.dtype)


def _rmsnorm(x, g3, l, out_dtype, tm=256):
    m, d = x.shape
    return pl.pallas_call(
        _rmsnorm_kernel,
        grid=(m // tm,),
        in_specs=[pl.BlockSpec((tm, d), lambda i: (i, 0)),
                  pl.BlockSpec((None, 1, d), lambda i: (l, 0, 0))],
        out_specs=pl.BlockSpec((tm, d), lambda i: (i, 0)),
        out_shape=jax.ShapeDtypeStruct((m, d), out_dtype),
        compiler_params=_cparams("parallel"),
        name="rmsnorm",
    )(x, g3)


def _mm_kernel(a_ref, b_ref, o_ref):
    o_ref[...] = jnp.dot(a_ref[...], b_ref[...], preferred_element_type=F32).astype(o_ref.dtype)


def _mm_res_kernel(a_ref, b_ref, r_ref, o_ref):
    acc = jnp.dot(a_ref[...], b_ref[...], preferred_element_type=F32)
    o_ref[...] = (r_ref[...] + acc).astype(o_ref.dtype)


def _matmul(a, b, l, out_dtype, tm, tn, residual=None, name="matmul"):
    m, kdim = a.shape
    n = b.shape[-1]
    tm, tn = min(tm, m), min(tn, n)
    in_specs = [pl.BlockSpec((tm, kdim), lambda i, j: (i, 0)),
                pl.BlockSpec((None, kdim, tn), lambda i, j: (l, 0, j))]
    args = [a, b]
    body = _mm_kernel
    if residual is not None:
        body = _mm_res_kernel
        in_specs.append(pl.BlockSpec((tm, tn), lambda i, j: (i, j)))
        args.append(residual)
    return pl.pallas_call(
        body,
        grid=(m // tm, n // tn),
        in_specs=in_specs,
        out_specs=pl.BlockSpec((tm, tn), lambda i, j: (i, j)),
        out_shape=jax.ShapeDtypeStruct((m, n), out_dtype),
        compiler_params=_cparams("parallel", "arbitrary"),
        name=name,
    )(*args)


def _mm_ws_kernel(a_ref, w_ref, o_ref, wb_ref):
    @pl.when(pl.program_id(1) == 0)
    def _():
        wb_ref[...] = w_ref[0].astype(BF16)

    o_ref[...] = lax.dot_general(a_ref[...], wb_ref[...], (((1,), (1,)), ((), ())),
                                 preferred_element_type=F32).astype(o_ref.dtype)


def _matmul_ws(a, wt, l, row0, n, out_dtype, tm, tn, name):
    m, kdim = a.shape
    w_spec = pl.BlockSpec((pl.Element(1), pl.Element(tn), pl.Element(kdim)),
                          lambda j, i: (l, pl.multiple_of(row0 + j * tn, 16), 0),
                          pipeline_mode=pl.Buffered(1))
    return pl.pallas_call(
        _mm_ws_kernel,
        grid=(n // tn, m // tm),
        in_specs=[pl.BlockSpec((tm, kdim), lambda j, i: (i, 0)), w_spec],
        out_specs=pl.BlockSpec((tm, tn), lambda j, i: (i, j)),
        out_shape=jax.ShapeDtypeStruct((m, n), out_dtype),
        scratch_shapes=[pltpu.VMEM((tn, kdim), BF16)],
        compiler_params=_cparams("arbitrary", "arbitrary"),
        name=name,
    )(a, wt)


def _conv_kernel(h_ref, b_ref, c_ref, z_ref, w_ref, cb_ref, o_ref):
    u = c_ref[...] * h_ref[...]
    row = lax.broadcasted_iota(jnp.int32, u.shape, 0)
    u1 = jnp.where(row >= 1, pltpu.roll(u, 1, 0), 0.0)
    u2 = jnp.where(row >= 2, pltpu.roll(u, 2, 0), 0.0)
    w = w_ref[...]
    y = cb_ref[...] + w[0:1, :] * u2
    y = y + w[1:2, :] * u1
    y = y + w[2:3, :] * u
    o_ref[...] = (b_ref[...] * y * _silu(z_ref[...])).astype(o_ref.dtype)


def _conv_branch(proj, conv_w, conv_b3, l, bsz, seq, width, tc=256):
    nb = width // tc
    blk = lambda off: pl.BlockSpec((seq, tc), lambda b, j: (b, off * nb + j))
    return pl.pallas_call(
        _conv_kernel,
        grid=(bsz, nb),
        in_specs=[blk(0), blk(1), blk(2), blk(3),
                  pl.BlockSpec((None, SC_KERNEL, tc), lambda b, j: (l, 0, j)),
                  pl.BlockSpec((None, 1, tc), lambda b, j: (l, 0, j))],
        out_specs=pl.BlockSpec((seq, tc), lambda b, j: (b, j)),
        out_shape=jax.ShapeDtypeStruct((bsz * seq, width), BF16),
        compiler_params=_cparams("parallel", "parallel"),
        name="conv_branch",
    )(proj, proj, proj, proj, conv_w, conv_b3)


def _memattn_kernel(q_ref, k_ref, v_ref, z_ref, o_ref):
    q = q_ref[...].astype(BF16)
    k = k_ref[...].astype(BF16)
    v = v_ref[...].astype(BF16)
    s = lax.dot_general(q, k, (((1,), (1,)), ((), ())), preferred_element_type=F32) * SCALE
    m = jnp.max(s, axis=-1, keepdims=True)
    e = jnp.exp(s - m)
    p = e / jnp.sum(e, axis=-1, keepdims=True)
    o = jnp.dot(p.astype(BF16), v, preferred_element_type=F32)
    o_ref[...] = (o * _silu(z_ref[...])).astype(o_ref.dtype)


def _mem_attention(proj, memkv, bsz, seq, mem_len, q_blk, z_blk, tq=512):
    nq = seq // tq
    return pl.pallas_call(
        _memattn_kernel,
        grid=(bsz, X_HEADS, nq),
        in_specs=[pl.BlockSpec((tq, HEAD_DIM), lambda b, h, i: (b * nq + i, q_blk + h)),
                  pl.BlockSpec((mem_len, HEAD_DIM), lambda b, h, i: (b, h)),
                  pl.BlockSpec((mem_len, HEAD_DIM), lambda b, h, i: (b, X_HEADS + h)),
                  pl.BlockSpec((tq, HEAD_DIM), lambda b, h, i: (b * nq + i, z_blk + h))],
        out_specs=pl.BlockSpec((tq, HEAD_DIM), lambda b, h, i: (b * nq + i, h)),
        out_shape=jax.ShapeDtypeStruct((bsz * seq, X_HEADS * HEAD_DIM), BF16),
        compiler_params=_cparams("parallel", "parallel", "parallel"),
        name="mem_attention",
    )(proj, memkv, memkv, proj)


def _compress_kernel(kv_ref, pos_ref, w1_ref, w2_ref, o_ref, *, n_half):
    half = CMP_BLOCK // 2
    acc_a = jnp.zeros((n_half, HEAD_DIM), F32)
    acc_b = jnp.zeros((n_half, HEAD_DIM), F32)
    for t in range(half):
        x = kv_ref[pl.ds(t, n_half, stride=half), :]
        xa = (x + pos_ref[t:t + 1, :]).astype(BF16)
        xb = (x + pos_ref[half + t:half + t + 1, :]).astype(BF16)
        acc_a = acc_a + jnp.dot(xa, w1_ref[t * HEAD_DIM:(t + 1) * HEAD_DIM, :],
                                preferred_element_type=F32)
        acc_b = acc_b + jnp.dot(xb, w1_ref[(half + t) * HEAD_DIM:(half + t + 1) * HEAD_DIM, :],
                                preferred_element_type=F32)
    hid = _silu(acc_a + pltpu.roll(acc_b, n_half - 1, 0))
    o_ref[...] = jnp.dot(hid.astype(BF16), w2_ref[...], preferred_element_type=F32)


def _compress(proj, cmp_pos, cmp_w1, cmp_w2, l, bsz, seq, kv_blk):
    g = NSA_KV_GROUPS
    n_half = seq // (CMP_BLOCK // 2)
    return pl.pallas_call(
        functools.partial(_compress_kernel, n_half=n_half),
        grid=(bsz, 2, g),
        in_specs=[pl.BlockSpec((seq, HEAD_DIM), lambda b, c, gi: (b, kv_blk + c * g + gi)),
                  pl.BlockSpec((None, None, CMP_BLOCK, HEAD_DIM), lambda b, c, gi: (l, c, 0, 0)),
                  pl.BlockSpec((None, None, CMP_BLOCK * HEAD_DIM, HEAD_DIM),
                               lambda b, c, gi: (l, c, 0, 0)),
                  pl.BlockSpec((None, None, HEAD_DIM, HEAD_DIM), lambda b, c, gi: (l, c, 0, 0))],
        out_specs=pl.BlockSpec((None, None, None, n_half, HEAD_DIM),
                               lambda b, c, gi: (b, c, gi, 0, 0)),
        out_shape=jax.ShapeDtypeStruct((bsz, 2, g, n_half, HEAD_DIM), F32),
        compiler_params=_cparams("parallel", "parallel", "parallel"),
        name="nsa_compress",
    )(proj, cmp_pos, cmp_w1, cmp_w2)


def _rope_tile(x, cos, sin_signed):
    return x * cos + pltpu.roll(x, HEAD_DIM // 2, 1) * sin_signed


def _nsa_prep_kernel(q_ref, ks_ref, vs_ref, kw_ref, vw_ref, cos_ref, sin_ref,
                     qb_ref, qr_ref, kso_ref, vst_ref, kwo_ref, vwt_ref):
    cos = cos_ref[...]
    sin = sin_ref[...]
    for h in range(NSA_HEADS):
        sl = slice(h * HEAD_DIM, (h + 1) * HEAD_DIM)
        x = q_ref[:, sl]
        qb_ref[:, sl] = x.astype(BF16)
        qr_ref[:, sl] = (_rope_tile(x, cos, sin) * (SCALE * LOG2E)).astype(BF16)
    n_tiles, _, width = vst_ref.shape
    for g in range(NSA_KV_GROUPS):
        sl = slice(g * HEAD_DIM, (g + 1) * HEAD_DIM)
        kso_ref[:, sl] = _rope_tile(ks_ref[:, sl], cos, sin).astype(BF16)
        kwo_ref[:, sl] = _rope_tile(kw_ref[:, sl], cos, sin).astype(BF16)
        vs_t = jnp.transpose(vs_ref[:, sl]).astype(BF16)
        vw_t = jnp.transpose(vw_ref[:, sl]).astype(BF16)
        for c in range(n_tiles):
            vst_ref[c, sl, :] = vs_t[:, c * width:(c + 1) * width]
            vwt_ref[c, sl, :] = vw_t[:, c * width:(c + 1) * width]


def _nsa_prep(proj, cos, sin_signed, bsz, seq, q_off, kv_off):
    tr = NSA_TQ
    nsa_w = NSA_HEADS * HEAD_DIM
    gw = NSA_KV_GROUPS * HEAD_DIM
    nr = seq // tr
    t = bsz * seq
    qblk = q_off // nsa_w
    kvb = kv_off // gw
    n_tiles = tr // NSA_TK
    rowq = lambda w, cb: pl.BlockSpec((tr, w), lambda b, i: (b * nr + i, cb))
    tab = pl.BlockSpec((tr, HEAD_DIM), lambda b, i: (i, 0))
    vt_spec = pl.BlockSpec((None, n_tiles, gw, NSA_TK), lambda b, i: (b, i, 0, 0))
    vt_shape = jax.ShapeDtypeStruct((bsz, seq // NSA_TK, gw, NSA_TK), BF16)
    return pl.pallas_call(
        _nsa_prep_kernel,
        grid=(bsz, nr),
        in_specs=[rowq(nsa_w, qblk), rowq(gw, kvb + 2), rowq(gw, kvb + 3),
                  rowq(gw, kvb + 4), rowq(gw, kvb + 5), tab, tab],
        out_specs=[rowq(nsa_w, 0), rowq(nsa_w, 0), rowq(gw, 0), vt_spec, rowq(gw, 0), vt_spec],
        out_shape=[jax.ShapeDtypeStruct((t, nsa_w), BF16), jax.ShapeDtypeStruct((t, nsa_w), BF16),
                   jax.ShapeDtypeStruct((t, gw), BF16), vt_shape,
                   jax.ShapeDtypeStruct((t, gw), BF16), vt_shape],
        compiler_params=_cparams("parallel", "parallel"),
        name="nsa_prep",
    )(proj, proj, proj, proj, proj, cos, sin_signed)


def _col_reduce(x, op, ways=4):
    nblk = x.shape[0] // 8
    accs = [x[8 * k:8 * (k + 1)] for k in range(min(ways, nblk))]
    for k in range(len(accs), nblk):
        accs[k % ways] = op(accs[k % ways], x[8 * k:8 * (k + 1)])
    while len(accs) > 1:
        accs = [op(accs[a], accs[a + 1]) for a in range(0, len(accs), 2)]
    return accs[0]


def _col_max(x):
    return jnp.max(_col_reduce(x, jnp.maximum), axis=0, keepdims=True)


def _col_sum(x):
    return jnp.sum(_col_reduce(x, jnp.add), axis=0, keepdims=True)


def _nsa_kernel(qb_ref, qr_ref, kc_ref, vc_ref, ks_ref, vst_ref, kw_ref, vwt_ref,
                ng_ref, nz_ref, ovl_ref, exp_ref, o_ref,
                qa_ref, ts0_ref, ts1_ref, tw0_ref, tw1_ref, xs0_ref, xs1_ref, xw0_ref, xw1_ref,
                ms_ref, ls_ref, mw_ref, lw_ref, accs_ref, accw_ref, ocmp_ref, ngt_ref,
                *, tq, tk, n_slc):
    assert tq == 2 * tk and WINDOW == tq
    gi = pl.program_id(1)
    i = pl.program_id(2)
    q0 = i * tq
    nt = (((1,), (1,)), ((), ()))
    pos_row = q0 + lax.broadcasted_iota(jnp.int32, (1, tq), 1)
    pos_all = jnp.concatenate([pos_row] * NSA_REP, axis=1)
    heads = [slice(r * HEAD_DIM, (r + 1) * HEAD_DIM) for r in range(NSA_REP)]
    cols = [slice(r * tq, (r + 1) * tq) for r in range(NSA_REP)]

    qb = jnp.concatenate([qb_ref[:, h] for h in heads], axis=0)
    kc = kc_ref[...].astype(BF16)
    vc_t = jnp.transpose(vc_ref[...]).astype(BF16)
    n_cmp = kc.shape[0]
    cmp_end = CMP_STRIDE * lax.broadcasted_iota(jnp.int32, (n_cmp, 1), 0) + (CMP_BLOCK - 1)
    cmask = cmp_end <= pos_all
    s = lax.dot_general(kc, qb, nt, preferred_element_type=F32) * SCALE
    s = jnp.where(cmask, s, NEG)
    e = jnp.exp(s - _col_max(s))
    p = e * (1.0 / _col_sum(e))
    p = jnp.where(cmask, p, 0.0).astype(BF16)
    ocmp_ref[...] = jnp.dot(vc_t, p, preferred_element_type=F32)
    imp_all = jnp.dot(ovl_ref[...], p, preferred_element_type=F32)
    imp = imp_all[:, cols[0]]
    for c in cols[1:]:
        imp = imp + imp_all[:, c]

    val = imp[0:n_slc, :]
    blk = lax.broadcasted_iota(jnp.int32, (n_slc, 1), 0)
    cur = pos_row // SLC_BLOCK
    future = blk * SLC_BLOCK > pos_row
    forced = (blk == 0) | (blk == cur) | (blk == cur - 1)
    val = jnp.where(future, -jnp.inf, jnp.where(forced, jnp.inf, val))
    rank = jnp.zeros((n_slc, tq), jnp.int32)
    for j in range(n_slc):
        vj = val[j:j + 1, :]
        beats = (vj > val) | ((vj == val) & (blk > j))
        rank = rank + beats.astype(jnp.int32)
    sel_bias = jnp.where(rank < min(SLC_TOP_N, n_slc), 0.0, NEG)
    sel_bias = jnp.concatenate([sel_bias, jnp.zeros((LANES - n_slc, tq), F32)], axis=0)
    sel_bias = jnp.transpose(sel_bias).astype(BF16)
    qa_ref[:, 0:HEAD_DIM] = jnp.concatenate([qr_ref[:, h] for h in heads], axis=0)
    qa_ref[:, HEAD_DIM:] = jnp.concatenate([sel_bias] * NSA_REP, axis=0)

    def scores(t_ref, x_ref, keys, queries, visible):
        t = lax.dot_general(keys, queries, nt, preferred_element_type=F32)
        t = jnp.where(visible, t, NEG)
        t_ref[...] = t
        x_ref[...] = _col_max(t)

    def absorb(t_ref, x_ref, vt, m_ref, l_ref, acc_ref):
        m_old = m_ref[...]
        m_new = jnp.maximum(m_old, x_ref[...])
        alpha = jnp.exp2(m_old - m_new)
        e = jnp.exp2(t_ref[...] - m_new)
        l_ref[...] = alpha * l_ref[...] + _col_sum(e)
        acc_ref[...] = alpha * acc_ref[...] + jnp.dot(vt, e.astype(BF16),
                                                      preferred_element_type=F32)
        m_ref[...] = m_new

    def reset(m_ref, l_ref, acc_ref):
        m_ref[...] = jnp.full(m_ref.shape, NEG, F32)
        l_ref[...] = jnp.zeros(l_ref.shape, F32)
        acc_ref[...] = jnp.zeros(acc_ref.shape, F32)

    def key_pos(c):
        return c * tk + lax.broadcasted_iota(jnp.int32, (tk, 1), 0)

    def slc_scores(t_ref, x_ref, c):
        start = pl.multiple_of(c * tk, tk)
        keys = jnp.concatenate([ks_ref[pl.ds(start, tk), :], exp_ref[c]], axis=1)
        scores(t_ref, x_ref, keys, qa_ref[...], key_pos(c) <= pos_all)

    def slc_absorb(t_ref, x_ref, c):
        absorb(t_ref, x_ref, vst_ref[c], ms_ref, ls_ref, accs_ref)

    reset(ms_ref, ls_ref, accs_ref)
    slc_scores(ts0_ref, xs0_ref, 0)

    def slc_pair(j, carry):
        slc_scores(ts1_ref, xs1_ref, 2 * j + 1)
        slc_absorb(ts0_ref, xs0_ref, 2 * j)
        slc_scores(ts0_ref, xs0_ref, 2 * j + 2)
        slc_absorb(ts1_ref, xs1_ref, 2 * j + 1)
        return carry

    lax.fori_loop(0, i, slc_pair, 0)
    slc_scores(ts1_ref, xs1_ref, 2 * i + 1)
    slc_absorb(ts0_ref, xs0_ref, 2 * i)
    slc_absorb(ts1_ref, xs1_ref, 2 * i + 1)

    low = jnp.maximum(pos_all - WINDOW, -1)

    def win_scores(t_ref, x_ref, c, causal):
        kpos = key_pos(c)
        start = pl.multiple_of(jnp.maximum(c, 0) * tk, tk)
        scores(t_ref, x_ref, kw_ref[pl.ds(start, tk), :], qa_ref[:, 0:HEAD_DIM],
               (kpos <= pos_all) if causal else (kpos > low))

    def win_absorb(t_ref, x_ref, c):
        absorb(t_ref, x_ref, vwt_ref[jnp.maximum(c, 0)], mw_ref, lw_ref, accw_ref)

    reset(mw_ref, lw_ref, accw_ref)
    win_scores(tw0_ref, xw0_ref, 2 * i, True)
    win_scores(tw1_ref, xw1_ref, 2 * i + 1, True)
    win_absorb(tw0_ref, xw0_ref, 2 * i)
    win_scores(tw0_ref, xw0_ref, 2 * i - 1, False)
    win_absorb(tw1_ref, xw1_ref, 2 * i + 1)
    win_scores(tw1_ref, xw1_ref, 2 * i - 2, False)
    win_absorb(tw0_ref, xw0_ref, 2 * i - 1)
    win_absorb(tw1_ref, xw1_ref, 2 * i - 2)

    ngt_ref[...] = jnp.transpose(ng_ref[...])
    inv_ls = 1.0 / ls_ref[...]
    inv_lw = 1.0 / lw_ref[...]
    for r in range(NSA_REP):
        row = (gi * NSA_REP + r) * NSA_BRANCHES
        g_cmp = _sigmoid(ngt_ref[pl.ds(row, 1), :])
        g_slc = _sigmoid(ngt_ref[pl.ds(row + 1, 1), :])
        g_win = _sigmoid(ngt_ref[pl.ds(row + 2, 1), :])
        o_t = (g_cmp * ocmp_ref[:, cols[r]]
               + (g_slc * inv_ls[:, cols[r]]) * accs_ref[:, cols[r]]
               + (g_win * inv_lw[:, cols[r]]) * accw_ref[:, cols[r]])
        o_ref[:, heads[r]] = (jnp.transpose(o_t) * _silu(nz_ref[:, heads[r]])).astype(o_ref.dtype)


def _nsa_attention(qb, qr, cmpkv, ks, vst, kw, vwt, ng, proj, ovl, expand, bsz, seq, nz_blk):
    tq, tk = NSA_TQ, NSA_TK
    g = NSA_KV_GROUPS
    gw = NSA_REP * HEAD_DIM
    nq = seq // tq
    n_slc = seq // SLC_BLOCK
    n_half = cmpkv.shape[-2]
    stacked = NSA_REP * tq
    rowq = lambda cb: pl.BlockSpec((tq, gw), lambda b, gi, i: (b * nq + i, cb + gi))
    kvs = pl.BlockSpec((seq, HEAD_DIM), lambda b, gi, i: (b, gi))
    cmps = lambda c: pl.BlockSpec((None, None, None, n_half, HEAD_DIM),
                                  lambda b, gi, i: (b, c, gi, 0, 0))
    vts = pl.BlockSpec((None, seq // tk, HEAD_DIM, tk), lambda b, gi, i: (b, 0, gi, 0))
    score_buf = pltpu.VMEM((tk, stacked), F32)
    stat_row = pltpu.VMEM((1, stacked), F32)
    out_acc = pltpu.VMEM((HEAD_DIM, stacked), F32)
    return pl.pallas_call(
        functools.partial(_nsa_kernel, tq=tq, tk=tk, n_slc=n_slc),
        grid=(bsz, g, nq),
        in_specs=[rowq(0), rowq(0), cmps(0), cmps(1), kvs, vts, kvs, vts,
                  pl.BlockSpec((tq, LANES), lambda b, gi, i: (b * nq + i, 0)),
                  rowq(nz_blk),
                  pl.BlockSpec((LANES, n_half), lambda b, gi, i: (0, 0)),
                  pl.BlockSpec((seq // tk, tk, LANES), lambda b, gi, i: (0, 0, 0))],
        out_specs=rowq(0),
        out_shape=jax.ShapeDtypeStruct((bsz * seq, NSA_HEADS * HEAD_DIM), BF16),
        scratch_shapes=[pltpu.VMEM((stacked, 2 * HEAD_DIM), BF16),
                        score_buf, score_buf, score_buf, score_buf,
                        stat_row, stat_row, stat_row, stat_row,
                        stat_row, stat_row, stat_row, stat_row,
                        out_acc, out_acc, out_acc,
                        pltpu.VMEM((LANES, tq), F32)],
        compiler_params=_cparams("parallel", "parallel", "arbitrary"),
        name="nsa_attention",
    )(qb, qr, cmpkv, cmpkv, ks, vst, kw, vwt, ng, proj, ovl, expand)


def _merge_kernel(ya_ref, yb_ref, yx_ref, wa_ref, wb_ref, wx_ref, g0_ref, g1_ref, g2_ref, o_ref):
    u = _sigmoid(g0_ref[...]) * jnp.dot(ya_ref[...], wa_ref[...], preferred_element_type=F32)
    u = u + _sigmoid(g1_ref[...]) * jnp.dot(yb_ref[...], wb_ref[...], preferred_element_type=F32)
    u = u + _sigmoid(g2_ref[...]) * jnp.dot(yx_ref[...], wx_ref[...], preferred_element_type=F32)
    o_ref[...] = u.astype(o_ref.dtype)


def _merge(ya, yb, yx, wa, wb, wx, proj, l, mg_off, d_model, tm=1024, tn=512):
    t = ya.shape[0]
    nj = d_model // tn
    mgb = mg_off // tn
    act = lambda w: pl.BlockSpec((tm, w), lambda i, j: (i, 0))
    wsp = lambda w: pl.BlockSpec((None, w, tn), lambda i, j: (l, 0, j))
    gsp = lambda br: pl.BlockSpec((tm, tn), lambda i, j: (i, mgb + br * nj + j))
    return pl.pallas_call(
        _merge_kernel,
        grid=(t // tm, nj),
        in_specs=[act(ya.shape[1]), act(yb.shape[1]), act(yx.shape[1]),
                  wsp(wa.shape[1]), wsp(wb.shape[1]), wsp(wx.shape[1]),
                  gsp(0), gsp(1), gsp(2)],
        out_specs=pl.BlockSpec((tm, tn), lambda i, j: (i, j)),
        out_shape=jax.ShapeDtypeStruct((t, d_model), BF16),
        compiler_params=_cparams("parallel", "parallel"),
        name="gated_merge",
    )(ya, yb, yx, wa, wb, wx, proj, proj, proj)


def _rope_tables(seq):
    half = HEAD_DIM // 2
    inv_freq = ROPE_THETA ** (-jnp.arange(half, dtype=F32) / half)
    ang = jnp.arange(seq, dtype=jnp.int32).astype(F32)[:, None] * inv_freq[None, :]
    cos, sin = jnp.cos(ang), jnp.sin(ang)
    return jnp.concatenate([cos, cos], axis=-1), jnp.concatenate([-sin, sin], axis=-1)


def _overlap_table(seq, n_half):
    n_slc = seq // SLC_BLOCK
    cmp_start = np.arange(n_half) * CMP_STRIDE
    slc_start = np.arange(n_slc) * SLC_BLOCK
    ovl = ((cmp_start[None, :] < slc_start[:, None] + SLC_BLOCK)
           & (cmp_start[None, :] + CMP_BLOCK > slc_start[:, None]))
    ovl = ovl & (cmp_start[None, :] + CMP_BLOCK <= seq)
    out = np.zeros((LANES, n_half), np.float32)
    out[:n_slc, :] = ovl
    return jnp.asarray(out, BF16)


def _expand_table(seq, tk):
    key_blk = np.arange(seq) // SLC_BLOCK
    e = (key_blk[:, None] == np.arange(LANES)[None, :]).astype(np.float32)
    return jnp.asarray(e.reshape(seq // tk, tk, LANES), BF16)


def kernel(x, mem, norm_g, w_in, conv_w, conv_b, cmp_pos, cmp_w1, cmp_w2, mem_norm_g, w_mem_kv,
           w_up_a, w_up_b, w_up_x, w_out, final_g):
    bsz, seq, d_model = x.shape
    mem_len = mem.shape[1]
    depth = w_in.shape[0]
    sc_w = conv_w.shape[-1]
    nsa_w = NSA_HEADS * HEAD_DIM
    kv_w = NSA_BRANCHES * 2 * NSA_KV_GROUPS * HEAD_DIM
    ng_w = NSA_BRANCHES * NSA_HEADS
    x_w = X_HEADS * HEAD_DIM
    proj_tn = 1024

    q_off = 4 * sc_w
    kv_off = q_off + nsa_w
    lo_w = kv_off + kv_w
    nz_off = 0
    xq_off = nz_off + nsa_w
    xz_off = xq_off + x_w
    mg_off = xz_off + x_w
    hi_w = mg_off + N_BRANCHES * d_model
    assert seq % NSA_TQ == 0 and seq // SLC_BLOCK <= LANES and sc_w % 256 == 0
    assert lo_w % proj_tn == 0 and hi_w % proj_tn == 0 and 0 < ng_w < LANES
    assert (lo_w + ng_w) % 16 == 0 and lo_w + ng_w + hi_w == w_in.shape[-1]

    w_in_t = jnp.swapaxes(w_in, 1, 2)
    w_ng = jnp.pad(w_in_t[:, lo_w:lo_w + ng_w, :], ((0, 0), (0, LANES - ng_w), (0, 0)))
    w_ng = jnp.swapaxes(w_ng, 1, 2).astype(BF16)
    w_memkv_b = w_mem_kv.astype(BF16)
    cmp_w1_b = cmp_w1.astype(BF16)
    cmp_w2_b = cmp_w2.astype(BF16)
    w_up_a_b = w_up_a.astype(BF16)
    w_up_b_b = w_up_b.astype(BF16)
    w_up_x_b = w_up_x.astype(BF16)
    w_out_b = w_out.astype(BF16)
    norm_g3 = norm_g.reshape(depth, 1, d_model)
    mem_norm_g3 = mem_norm_g.reshape(depth, 1, d_model)
    conv_b3 = conv_b.reshape(depth, 1, sc_w)

    cos, sin_signed = _rope_tables(seq)
    n_half = seq // CMP_STRIDE
    ovl = _overlap_table(seq, n_half)
    expand = _expand_table(seq, NSA_TK)

    t = bsz * seq
    xf = x.reshape(t, d_model)
    memf = mem.reshape(bsz * mem_len, d_model)

    for l in range(depth):
        h = _rmsnorm(xf, norm_g3, l, BF16)
        proj_lo = _matmul_ws(h, w_in_t, l, 0, lo_w, F32, 1024, proj_tn, "in_proj_lo")
        proj_hi = _matmul_ws(h, w_in_t, l, lo_w + ng_w, hi_w, F32, 1024, proj_tn, "in_proj_hi")
        ng = _matmul(h, w_ng, l, F32, 1024, LANES, name="gate_proj")

        y_a = _conv_branch(proj_lo, conv_w, conv_b3, l, bsz, seq, sc_w)

        hm = _rmsnorm(memf, mem_norm_g3, l, BF16)
        memkv = _matmul(hm, w_memkv_b, l, F32, 512, 1024, name="mem_kv_proj")
        y_x = _mem_attention(proj_hi, memkv, bsz, seq, mem_len, xq_off // HEAD_DIM, xz_off // HEAD_DIM)

        cmpkv = _compress(proj_lo, cmp_pos, cmp_w1_b, cmp_w2_b, l, bsz, seq, kv_off // HEAD_DIM)
        qb, qr, ks, vst, kw, vwt = _nsa_prep(proj_lo, cos, sin_signed, bsz, seq, q_off, kv_off)
        y_b = _nsa_attention(qb, qr, cmpkv, ks, vst, kw, vwt, ng, proj_hi, ovl, expand,
                             bsz, seq, nz_off // (NSA_REP * HEAD_DIM))

        u = _merge(y_a, y_b, y_x, w_up_a_b, w_up_b_b, w_up_x_b, proj_hi, l, mg_off, d_model)
        xf = _matmul(u, w_out_b, l, F32, 1024, 1024, residual=xf, name="out_proj")

    out = _rmsnorm(xf, final_g.reshape(1, 1, d_model), 0, F32)
    return out.reshape(bsz, seq, d_model)
```

```python
import functools

import numpy as np
import jax
import jax.numpy as jnp
from jax import lax
from jax.experimental import pallas as pl
from jax.experimental.pallas import tpu as pltpu

F32 = jnp.float32
BF16 = jnp.bfloat16

HEAD_DIM = 128
ROPE_THETA = 10000.0
EPS = 1e-6
NEG = -1e30
SC_KERNEL = 3
NSA_HEADS = 16
NSA_KV_GROUPS = 4
NSA_REP = NSA_HEADS // NSA_KV_GROUPS
NSA_BRANCHES = 3
CMP_BLOCK = 32
CMP_STRIDE = 16
SLC_BLOCK = 64
SLC_TOP_N = 16
WINDOW = 512
X_HEADS = 4
N_BRANCHES = 3
SCALE = HEAD_DIM ** -0.5
LOG2E = 1.4426950408889634

V7X_VMEM_BYTES = 64 * 1024 * 1024
VMEM_LIMIT_BYTES = V7X_VMEM_BYTES - 8 * 1024 * 1024
LANES = 128

NSA_TQ = 512
NSA_TK = 256
VT_PAD = 16


def _cparams(*sem):
    return pltpu.CompilerParams(dimension_semantics=sem, vmem_limit_bytes=VMEM_LIMIT_BYTES)


def _silu(x):
    return x / (1.0 + jnp.exp(-x))


def _sigmoid(x):
    return 1.0 / (1.0 + jnp.exp(-x))


def _rmsnorm_kernel(x_ref, g_ref, o_ref):
    x = x_ref[...]
    ms = jnp.mean(x * x, axis=-1, keepdims=True)
    o_ref[...] = (x * lax.rsqrt(ms + EPS) * g_ref[...]).astype(o_ref.dtype)


def _rmsnorm(x, g3, l, out_dtype, tm=256):
    m, d = x.shape
    return pl.pallas_call(
        _rmsnorm_kernel,
        grid=(m // tm,),
        in_specs=[pl.BlockSpec((tm, d), lambda i: (i, 0)),
                  pl.BlockSpec((None, 1, d), lambda i: (l, 0, 0))],
        out_specs=pl.BlockSpec((tm, d), lambda i: (i, 0)),
        out_shape=jax.ShapeDtypeStruct((m, d), out_dtype),
        compiler_params=_cparams("parallel"),
        name="rmsnorm",
    )(x, g3)


def _mm_kernel(a_ref, b_ref, o_ref):
    o_ref[...] = jnp.dot(a_ref[...], b_ref[...], preferred_element_type=F32).astype(o_ref.dtype)


def _matmul(a, b, l, out_dtype, tm, tn, name):
    m, kdim = a.shape
    n = b.shape[-1]
    tm, tn = min(tm, m), min(tn, n)
    return pl.pallas_call(
        _mm_kernel,
        grid=(m // tm, n // tn),
        in_specs=[pl.BlockSpec((tm, kdim), lambda i, j: (i, 0)),
                  pl.BlockSpec((None, kdim, tn), lambda i, j: (l, 0, j))],
        out_specs=pl.BlockSpec((tm, tn), lambda i, j: (i, j)),
        out_shape=jax.ShapeDtypeStruct((m, n), out_dtype),
        compiler_params=_cparams("parallel", "arbitrary"),
        name=name,
    )(a, b)


def _mm_ws_kernel(a_ref, w_ref, o_ref, wb_ref):
    @pl.when(pl.program_id(1) == 0)
    def _():
        wb_ref[...] = w_ref[0].astype(BF16)

    o_ref[...] = lax.dot_general(a_ref[...], wb_ref[...], (((1,), (1,)), ((), ())),
                                 preferred_element_type=F32).astype(o_ref.dtype)


def _matmul_ws(a, wt, l, row0, n, out_dtype, tm, tn, name):
    m, kdim = a.shape
    w_spec = pl.BlockSpec((pl.Element(1), pl.Element(tn), pl.Element(kdim)),
                          lambda j, i: (l, pl.multiple_of(row0 + j * tn, 16), 0))
    return pl.pallas_call(
        _mm_ws_kernel,
        grid=(n // tn, m // tm),
        in_specs=[pl.BlockSpec((tm, kdim), lambda j, i: (i, 0)), w_spec],
        out_specs=pl.BlockSpec((tm, tn), lambda j, i: (i, j)),
        out_shape=jax.ShapeDtypeStruct((m, n), out_dtype),
        scratch_shapes=[pltpu.VMEM((tn, kdim), BF16)],
        compiler_params=_cparams("arbitrary", "arbitrary"),
        name=name,
    )(a, wt)


def _mm_ws_res_kernel(a_ref, w_ref, r_ref, o_ref, wb_ref):
    @pl.when(pl.program_id(1) == 0)
    def _():
        wb_ref[...] = w_ref[...].astype(BF16)

    acc = jnp.dot(a_ref[...], wb_ref[...], preferred_element_type=F32)
    o_ref[...] = (r_ref[...] + acc).astype(o_ref.dtype)


def _matmul_ws_res(a, w, l, residual, out_dtype, tm, tn, name):
    m, kdim = a.shape
    n = w.shape[-1]
    return pl.pallas_call(
        _mm_ws_res_kernel,
        grid=(n // tn, m // tm),
        in_specs=[pl.BlockSpec((tm, kdim), lambda j, i: (i, 0)),
                  pl.BlockSpec((None, kdim, tn), lambda j, i: (l, 0, j)),
                  pl.BlockSpec((tm, tn), lambda j, i: (i, j))],
        out_specs=pl.BlockSpec((tm, tn), lambda j, i: (i, j)),
        out_shape=jax.ShapeDtypeStruct((m, n), out_dtype),
        scratch_shapes=[pltpu.VMEM((kdim, tn), BF16)],
        compiler_params=_cparams("arbitrary", "arbitrary"),
        name=name,
    )(a, w, residual)


def _conv_kernel(h_ref, b_ref, c_ref, z_ref, w_ref, cb_ref, o_ref):
    u = c_ref[...] * h_ref[...]
    row = lax.broadcasted_iota(jnp.int32, u.shape, 0)
    u1 = jnp.where(row >= 1, pltpu.roll(u, 1, 0), 0.0)
    u2 = jnp.where(row >= 2, pltpu.roll(u, 2, 0), 0.0)
    w = w_ref[...]
    y = cb_ref[...] + w[0:1, :] * u2
    y = y + w[1:2, :] * u1
    y = y + w[2:3, :] * u
    o_ref[...] = (b_ref[...] * y * _silu(z_ref[...])).astype(o_ref.dtype)


def _conv_branch(proj, conv_w, conv_b3, l, bsz, seq, width, tc=256):
    nb = width // tc
    blk = lambda off: pl.BlockSpec((seq, tc), lambda b, j: (b, off * nb + j))
    return pl.pallas_call(
        _conv_kernel,
        grid=(bsz, nb),
        in_specs=[blk(0), blk(1), blk(2), blk(3),
                  pl.BlockSpec((None, SC_KERNEL, tc), lambda b, j: (l, 0, j)),
                  pl.BlockSpec((None, 1, tc), lambda b, j: (l, 0, j))],
        out_specs=pl.BlockSpec((seq, tc), lambda b, j: (b, j)),
        out_shape=jax.ShapeDtypeStruct((bsz * seq, width), BF16),
        compiler_params=_cparams("parallel", "parallel"),
        name="conv_branch",
    )(proj, proj, proj, proj, conv_w, conv_b3)


def _memattn_kernel(q_ref, k_ref, v_ref, z_ref, o_ref):
    q = q_ref[...].astype(BF16)
    k = k_ref[...].astype(BF16)
    v = v_ref[...].astype(BF16)
    s = lax.dot_general(q, k, (((1,), (1,)), ((), ())), preferred_element_type=F32) * SCALE
    m = jnp.max(s, axis=-1, keepdims=True)
    e = jnp.exp(s - m)
    p = e / jnp.sum(e, axis=-1, keepdims=True)
    o = jnp.dot(p.astype(BF16), v, preferred_element_type=F32)
    o_ref[...] = (o * _silu(z_ref[...])).astype(o_ref.dtype)


def _mem_attention(proj, memkv, bsz, seq, mem_len, q_blk, z_blk, tq=512):
    nq = seq // tq
    return pl.pallas_call(
        _memattn_kernel,
        grid=(bsz, X_HEADS, nq),
        in_specs=[pl.BlockSpec((tq, HEAD_DIM), lambda b, h, i: (b * nq + i, q_blk + h)),
                  pl.BlockSpec((mem_len, HEAD_DIM), lambda b, h, i: (b, h)),
                  pl.BlockSpec((mem_len, HEAD_DIM), lambda b, h, i: (b, X_HEADS + h)),
                  pl.BlockSpec((tq, HEAD_DIM), lambda b, h, i: (b * nq + i, z_blk + h))],
        out_specs=pl.BlockSpec((tq, HEAD_DIM), lambda b, h, i: (b * nq + i, h)),
        out_shape=jax.ShapeDtypeStruct((bsz * seq, X_HEADS * HEAD_DIM), BF16),
        compiler_params=_cparams("parallel", "parallel", "parallel"),
        name="mem_attention",
    )(proj, memkv, memkv, proj)


def _compress_kernel(kv_ref, pos_ref, w1_ref, w2_ref, o_ref, *, n_half):
    half = CMP_BLOCK // 2
    acc_a = jnp.zeros((n_half, HEAD_DIM), F32)
    acc_b = jnp.zeros((n_half, HEAD_DIM), F32)
    for t in range(half):
        x = kv_ref[pl.ds(t, n_half, stride=half), :]
        xa = (x + pos_ref[t:t + 1, :]).astype(BF16)
        xb = (x + pos_ref[half + t:half + t + 1, :]).astype(BF16)
        acc_a = acc_a + jnp.dot(xa, w1_ref[t * HEAD_DIM:(t + 1) * HEAD_DIM, :],
                                preferred_element_type=F32)
        acc_b = acc_b + jnp.dot(xb, w1_ref[(half + t) * HEAD_DIM:(half + t + 1) * HEAD_DIM, :],
                                preferred_element_type=F32)
    hid = _silu(acc_a + pltpu.roll(acc_b, n_half - 1, 0))
    o_ref[...] = jnp.dot(hid.astype(BF16), w2_ref[...], preferred_element_type=F32)


def _compress(proj, cmp_pos, cmp_w1, cmp_w2, l, bsz, seq, kv_blk):
    g = NSA_KV_GROUPS
    n_half = seq // (CMP_BLOCK // 2)
    return pl.pallas_call(
        functools.partial(_compress_kernel, n_half=n_half),
        grid=(bsz, 2, g),
        in_specs=[pl.BlockSpec((seq, HEAD_DIM), lambda b, c, gi: (b, kv_blk + c * g + gi)),
                  pl.BlockSpec((None, None, CMP_BLOCK, HEAD_DIM), lambda b, c, gi: (l, c, 0, 0)),
                  pl.BlockSpec((None, None, CMP_BLOCK * HEAD_DIM, HEAD_DIM),
                               lambda b, c, gi: (l, c, 0, 0)),
                  pl.BlockSpec((None, None, HEAD_DIM, HEAD_DIM), lambda b, c, gi: (l, c, 0, 0))],
        out_specs=pl.BlockSpec((None, None, None, n_half, HEAD_DIM),
                               lambda b, c, gi: (b, c, gi, 0, 0)),
        out_shape=jax.ShapeDtypeStruct((bsz, 2, g, n_half, HEAD_DIM), F32),
        compiler_params=_cparams("parallel", "parallel", "parallel"),
        name="nsa_compress",
    )(proj, cmp_pos, cmp_w1, cmp_w2)


def _rope_tile(x, cos, sin_signed):
    return x * cos + pltpu.roll(x, HEAD_DIM // 2, 1) * sin_signed


def _nsa_prep_kernel(q_ref, ks_ref, vs_ref, kw_ref, vw_ref, cos_ref, sin_ref,
                     qb_ref, qr_ref, kso_ref, vst_ref, kwo_ref, vwt_ref):
    cos = cos_ref[...]
    sin = sin_ref[...]
    for h in range(NSA_HEADS):
        sl = slice(h * HEAD_DIM, (h + 1) * HEAD_DIM)
        x = q_ref[:, sl]
        qb_ref[:, sl] = x.astype(BF16)
        qr_ref[:, sl] = (_rope_tile(x, cos, sin) * (SCALE * LOG2E)).astype(BF16)
    n_tiles, _, width = vst_ref.shape
    for g in range(NSA_KV_GROUPS):
        sl = slice(g * HEAD_DIM, (g + 1) * HEAD_DIM)
        kso_ref[:, sl] = _rope_tile(ks_ref[:, sl], cos, sin).astype(BF16)
        kwo_ref[:, sl] = _rope_tile(kw_ref[:, sl], cos, sin).astype(BF16)
        vs_t = jnp.transpose(vs_ref[:, sl]).astype(BF16)
        vw_t = jnp.transpose(vw_ref[:, sl]).astype(BF16)
        for c in range(n_tiles):
            vst_ref[c, sl, :] = vs_t[:, c * width:(c + 1) * width]
            vwt_ref[c, sl, :] = vw_t[:, c * width:(c + 1) * width]


def _nsa_prep(proj, cos, sin_signed, bsz, seq, q_off, kv_off):
    tr = NSA_TQ
    nsa_w = NSA_HEADS * HEAD_DIM
    gw = NSA_KV_GROUPS * HEAD_DIM
    nr = seq // tr
    t = bsz * seq
    qblk = q_off // nsa_w
    kvb = kv_off // gw
    n_tiles = tr // NSA_TK
    rowq = lambda w, cb: pl.BlockSpec((tr, w), lambda b, i: (b * nr + i, cb))
    tab = pl.BlockSpec((tr, HEAD_DIM), lambda b, i: (i, 0))
    vt_spec = pl.BlockSpec((None, n_tiles, gw, NSA_TK), lambda b, i: (b, i, 0, 0))
    vt_shape = jax.ShapeDtypeStruct((bsz, seq // NSA_TK, gw, NSA_TK), BF16)
    return pl.pallas_call(
        _nsa_prep_kernel,
        grid=(bsz, nr),
        in_specs=[rowq(nsa_w, qblk), rowq(gw, kvb + 2), rowq(gw, kvb + 3),
                  rowq(gw, kvb + 4), rowq(gw, kvb + 5), tab, tab],
        out_specs=[rowq(nsa_w, 0), rowq(nsa_w, 0), rowq(gw, 0), vt_spec, rowq(gw, 0), vt_spec],
        out_shape=[jax.ShapeDtypeStruct((t, nsa_w), BF16), jax.ShapeDtypeStruct((t, nsa_w), BF16),
                   jax.ShapeDtypeStruct((t, gw), BF16), vt_shape,
                   jax.ShapeDtypeStruct((t, gw), BF16), vt_shape],
        compiler_params=_cparams("parallel", "parallel"),
        name="nsa_prep",
    )(proj, proj, proj, proj, proj, cos, sin_signed)


def _col_reduce(x, op, ways=4):
    nblk = x.shape[0] // 8
    accs = [x[8 * k:8 * (k + 1)] for k in range(min(ways, nblk))]
    for k in range(len(accs), nblk):
        accs[k % ways] = op(accs[k % ways], x[8 * k:8 * (k + 1)])
    while len(accs) > 1:
        accs = [op(accs[a], accs[a + 1]) for a in range(0, len(accs), 2)]
    return accs[0]


def _col_max(x):
    return jnp.max(_col_reduce(x, jnp.maximum), axis=0, keepdims=True)


def _col_sum(x):
    return jnp.sum(_col_reduce(x, jnp.add), axis=0, keepdims=True)


def _nsa_kernel(qb_ref, qr_ref, kc_ref, vc_ref, ks_ref, vst_ref, kw_ref, vwt_ref,
                ng_ref, nz_ref, ovl_ref, exp_ref, o_ref,
                qa_ref, ts0_ref, ts1_ref, tw0_ref, tw1_ref, xs0_ref, xs1_ref, xw0_ref, xw1_ref,
                ms_ref, mw_ref, accs_ref, accw_ref, ocmp_ref, ngt_ref,
                *, tq, tk, n_slc):
    assert tq == 2 * tk and WINDOW == tq
    gi = pl.program_id(1)
    i = pl.program_id(2)
    q0 = i * tq
    nt = (((1,), (1,)), ((), ()))
    pos_row = q0 + lax.broadcasted_iota(jnp.int32, (1, tq), 1)
    pos_all = jnp.concatenate([pos_row] * NSA_REP, axis=1)
    heads = [slice(r * HEAD_DIM, (r + 1) * HEAD_DIM) for r in range(NSA_REP)]
    cols = [slice(r * tq, (r + 1) * tq) for r in range(NSA_REP)]

    qb = jnp.concatenate([qb_ref[:, h] for h in heads], axis=0)
    kc = kc_ref[...].astype(BF16)
    vc_t = jnp.transpose(vc_ref[...]).astype(BF16)
    n_cmp = kc.shape[0]
    cmp_end = CMP_STRIDE * lax.broadcasted_iota(jnp.int32, (n_cmp, 1), 0) + (CMP_BLOCK - 1)
    cmask = cmp_end <= pos_all
    s = lax.dot_general(kc, qb, nt, preferred_element_type=F32) * SCALE
    s = jnp.where(cmask, s, NEG)
    e = jnp.exp(s - _col_max(s))
    p = e * (1.0 / _col_sum(e))
    p = jnp.where(cmask, p, 0.0).astype(BF16)
    ocmp_ref[...] = jnp.dot(vc_t, p, preferred_element_type=F32)
    imp_all = jnp.dot(ovl_ref[...], p, preferred_element_type=F32)
    imp = imp_all[:, cols[0]]
    for c in cols[1:]:
        imp = imp + imp_all[:, c]

    val = imp[0:n_slc, :]
    blk = lax.broadcasted_iota(jnp.int32, (n_slc, 1), 0)
    cur = pos_row // SLC_BLOCK
    future = blk * SLC_BLOCK > pos_row
    forced = (blk == 0) | (blk == cur) | (blk == cur - 1)
    val = jnp.where(future, -jnp.inf, jnp.where(forced, jnp.inf, val))
    rank = jnp.zeros((n_slc, tq), jnp.int32)
    for j in range(n_slc):
        vj = val[j:j + 1, :]
        beats = (vj > val) | ((vj == val) & (blk > j))
        rank = rank + beats.astype(jnp.int32)
    sel_bias = jnp.where(rank < min(SLC_TOP_N, n_slc), 0.0, NEG)
    sel_bias = jnp.concatenate([sel_bias, jnp.zeros((LANES - n_slc, tq), F32)], axis=0)
    sel_bias = jnp.transpose(sel_bias).astype(BF16)
    qa_ref[:, 0:HEAD_DIM] = jnp.concatenate([qr_ref[:, h] for h in heads], axis=0)
    qa_ref[:, HEAD_DIM:] = jnp.concatenate([sel_bias] * NSA_REP, axis=0)

    ones_rows = (lax.broadcasted_iota(jnp.int32, (VT_PAD, tk), 0) == 0).astype(BF16)

    def scores(t_ref, x_ref, keys, queries, visible):
        t = lax.dot_general(keys, queries, nt, preferred_element_type=F32)
        if visible is not None:
            t = jnp.where(visible, t, NEG)
        t_ref[...] = t
        x_ref[...] = _col_max(t)

    def absorb(t_ref, x_ref, vt, m_ref, acc_ref):
        m_old = m_ref[...]
        m_new = jnp.maximum(m_old, x_ref[...])
        alpha = jnp.exp2(m_old - m_new)
        e = jnp.exp2(t_ref[...] - m_new).astype(BF16)
        vt_ones = jnp.concatenate([vt, ones_rows], axis=0)
        acc_ref[...] = alpha * acc_ref[...] + jnp.dot(vt_ones, e, preferred_element_type=F32)
        m_ref[...] = m_new

    def key_pos(c):
        return c * tk + lax.broadcasted_iota(jnp.int32, (tk, 1), 0)

    def slc_scores(t_ref, x_ref, c, causal):
        start = pl.multiple_of(c * tk, tk)
        keys = jnp.concatenate([ks_ref[pl.ds(start, tk), :], exp_ref[c]], axis=1)
        scores(t_ref, x_ref, keys, qa_ref[...], (key_pos(c) <= pos_all) if causal else None)

    def slc_absorb(t_ref, x_ref, c):
        absorb(t_ref, x_ref, vst_ref[c], ms_ref, accs_ref)

    slc_scores(ts0_ref, xs0_ref, 2 * i, True)
    slc_scores(ts1_ref, xs1_ref, 2 * i + 1, True)
    ms_ref[...] = jnp.maximum(xs0_ref[...], xs1_ref[...])
    accs_ref[...] = jnp.zeros(accs_ref.shape, F32)
    slc_absorb(ts0_ref, xs0_ref, 2 * i)

    def slc_pair(j, carry):
        slc_scores(ts0_ref, xs0_ref, 2 * j, False)
        slc_absorb(ts1_ref, xs1_ref, jnp.where(j == 0, 2 * i + 1, 2 * j - 1))
        slc_scores(ts1_ref, xs1_ref, 2 * j + 1, False)
        slc_absorb(ts0_ref, xs0_ref, 2 * j)
        return carry

    lax.fori_loop(0, i, slc_pair, 0)
    slc_absorb(ts1_ref, xs1_ref, jnp.where(i == 0, 1, 2 * i - 1))

    low = jnp.maximum(pos_all - WINDOW, -1)

    def win_scores(t_ref, x_ref, c, causal):
        kpos = key_pos(c)
        start = pl.multiple_of(jnp.maximum(c, 0) * tk, tk)
        scores(t_ref, x_ref, kw_ref[pl.ds(start, tk), :], qa_ref[:, 0:HEAD_DIM],
               (kpos <= pos_all) if causal else (kpos > low))

    def win_absorb(t_ref, x_ref, c):
        absorb(t_ref, x_ref, vwt_ref[jnp.maximum(c, 0)], mw_ref, accw_ref)

    mw_ref[...] = jnp.full(mw_ref.shape, NEG, F32)
    accw_ref[...] = jnp.zeros(accw_ref.shape, F32)
    win_scores(tw0_ref, xw0_ref, 2 * i, True)
    win_scores(tw1_ref, xw1_ref, 2 * i + 1, True)
    win_absorb(tw0_ref, xw0_ref, 2 * i)
    win_scores(tw0_ref, xw0_ref, 2 * i - 1, False)
    win_absorb(tw1_ref, xw1_ref, 2 * i + 1)
    win_scores(tw1_ref, xw1_ref, 2 * i - 2, False)
    win_absorb(tw0_ref, xw0_ref, 2 * i - 1)
    win_absorb(tw1_ref, xw1_ref, 2 * i - 2)

    ngt_ref[...] = jnp.transpose(ng_ref[...])
    inv_ls = 1.0 / accs_ref[HEAD_DIM:HEAD_DIM + 1, :]
    inv_lw = 1.0 / accw_ref[HEAD_DIM:HEAD_DIM + 1, :]
    for r in range(NSA_REP):
        row = (gi * NSA_REP + r) * NSA_BRANCHES
        g_cmp = _sigmoid(ngt_ref[pl.ds(row, 1), :])
        g_slc = _sigmoid(ngt_ref[pl.ds(row + 1, 1), :])
        g_win = _sigmoid(ngt_ref[pl.ds(row + 2, 1), :])
        o_t = (g_cmp * ocmp_ref[:, cols[r]]
               + (g_slc * inv_ls[:, cols[r]]) * accs_ref[0:HEAD_DIM, cols[r]]
               + (g_win * inv_lw[:, cols[r]]) * accw_ref[0:HEAD_DIM, cols[r]])
        o_ref[:, heads[r]] = (jnp.transpose(o_t) * _silu(nz_ref[:, heads[r]])).astype(o_ref.dtype)


def _nsa_attention(qb, qr, cmpkv, ks, vst, kw, vwt, ng, proj, ovl, expand, bsz, seq, nz_blk):
    tq, tk = NSA_TQ, NSA_TK
    g = NSA_KV_GROUPS
    gw = NSA_REP * HEAD_DIM
    nq = seq // tq
    n_slc = seq // SLC_BLOCK
    n_half = cmpkv.shape[-2]
    stacked = NSA_REP * tq
    rowq = lambda cb: pl.BlockSpec((tq, gw), lambda b, gi, i: (b * nq + i, cb + gi))
    kvs = pl.BlockSpec((seq, HEAD_DIM), lambda b, gi, i: (b, gi))
    cmps = lambda c: pl.BlockSpec((None, None, None, n_half, HEAD_DIM),
                                  lambda b, gi, i: (b, c, gi, 0, 0))
    vts = pl.BlockSpec((None, seq // tk, HEAD_DIM, tk), lambda b, gi, i: (b, 0, gi, 0))
    score_buf = pltpu.VMEM((tk, stacked), F32)
    stat_row = pltpu.VMEM((1, stacked), F32)
    flash_acc = pltpu.VMEM((HEAD_DIM + VT_PAD, stacked), F32)
    return pl.pallas_call(
        functools.partial(_nsa_kernel, tq=tq, tk=tk, n_slc=n_slc),
        grid=(bsz, g, nq),
        in_specs=[rowq(0), rowq(0), cmps(0), cmps(1), kvs, vts, kvs, vts,
                  pl.BlockSpec((tq, LANES), lambda b, gi, i: (b * nq + i, 0)),
                  rowq(nz_blk),
                  pl.BlockSpec((LANES, n_half), lambda b, gi, i: (0, 0)),
                  pl.BlockSpec((seq // tk, tk, LANES), lambda b, gi, i: (0, 0, 0))],
        out_specs=rowq(0),
        out_shape=jax.ShapeDtypeStruct((bsz * seq, NSA_HEADS * HEAD_DIM), BF16),
        scratch_shapes=[pltpu.VMEM((stacked, 2 * HEAD_DIM), BF16),
                        score_buf, score_buf, score_buf, score_buf,
                        stat_row, stat_row, stat_row, stat_row,
                        stat_row, stat_row,
                        flash_acc, flash_acc, pltpu.VMEM((HEAD_DIM, stacked), F32),
                        pltpu.VMEM((LANES, tq), F32)],
        compiler_params=_cparams("parallel", "parallel", "arbitrary"),
        name="nsa_attention",
    )(qb, qr, cmpkv, cmpkv, ks, vst, kw, vwt, ng, proj, ovl, expand)


def _merge_kernel(ya_ref, yb_ref, yx_ref, wa_ref, wb_ref, wx_ref, g0_ref, g1_ref, g2_ref, o_ref,
                  wab_ref, wbb_ref, wxb_ref):
    @pl.when(pl.program_id(1) == 0)
    def _():
        wab_ref[...] = wa_ref[...].astype(BF16)
        wbb_ref[...] = wb_ref[...].astype(BF16)
        wxb_ref[...] = wx_ref[...].astype(BF16)

    u = _sigmoid(g0_ref[...]) * jnp.dot(ya_ref[...], wab_ref[...], preferred_element_type=F32)
    u = u + _sigmoid(g1_ref[...]) * jnp.dot(yb_ref[...], wbb_ref[...], preferred_element_type=F32)
    u = u + _sigmoid(g2_ref[...]) * jnp.dot(yx_ref[...], wxb_ref[...], preferred_element_type=F32)
    o_ref[...] = u.astype(o_ref.dtype)


def _merge(ya, yb, yx, wa, wb, wx, proj, l, mg_off, d_model, tm=1024, tn=512):
    t = ya.shape[0]
    nj = d_model // tn
    mgb = mg_off // tn
    act = lambda w: pl.BlockSpec((tm, w), lambda j, i: (i, 0))
    wsp = lambda w: pl.BlockSpec((None, w, tn), lambda j, i: (l, 0, j),
                                 pipeline_mode=pl.Buffered(1))
    gsp = lambda br: pl.BlockSpec((tm, tn), lambda j, i: (i, mgb + br * nj + j))
    return pl.pallas_call(
        _merge_kernel,
        grid=(nj, t // tm),
        in_specs=[act(ya.shape[1]), act(yb.shape[1]), act(yx.shape[1]),
                  wsp(wa.shape[1]), wsp(wb.shape[1]), wsp(wx.shape[1]),
                  gsp(0), gsp(1), gsp(2)],
        out_specs=pl.BlockSpec((tm, tn), lambda j, i: (i, j)),
        out_shape=jax.ShapeDtypeStruct((t, d_model), BF16),
        scratch_shapes=[pltpu.VMEM((wa.shape[1], tn), BF16), pltpu.VMEM((wb.shape[1], tn), BF16),
                        pltpu.VMEM((wx.shape[1], tn), BF16)],
        compiler_params=_cparams("arbitrary", "arbitrary"),
        name="gated_merge",
    )(ya, yb, yx, wa, wb, wx, proj, proj, proj)


def _rope_tables(seq):
    half = HEAD_DIM // 2
    inv_freq = ROPE_THETA ** (-jnp.arange(half, dtype=F32) / half)
    ang = jnp.arange(seq, dtype=jnp.int32).astype(F32)[:, None] * inv_freq[None, :]
    cos, sin = jnp.cos(ang), jnp.sin(ang)
    return jnp.concatenate([cos, cos], axis=-1), jnp.concatenate([-sin, sin], axis=-1)


def _overlap_table(seq, n_half):
    n_slc = seq // SLC_BLOCK
    cmp_start = np.arange(n_half) * CMP_STRIDE
    slc_start = np.arange(n_slc) * SLC_BLOCK
    ovl = ((cmp_start[None, :] < slc_start[:, None] + SLC_BLOCK)
           & (cmp_start[None, :] + CMP_BLOCK > slc_start[:, None]))
    ovl = ovl & (cmp_start[None, :] + CMP_BLOCK <= seq)
    out = np.zeros((LANES, n_half), np.float32)
    out[:n_slc, :] = ovl
    return jnp.asarray(out, BF16)


def _expand_table(seq, tk):
    key_blk = np.arange(seq) // SLC_BLOCK
    e = (key_blk[:, None] == np.arange(LANES)[None, :]).astype(np.float32)
    return jnp.asarray(e.reshape(seq // tk, tk, LANES), BF16)


def kernel(x, mem, norm_g, w_in, conv_w, conv_b, cmp_pos, cmp_w1, cmp_w2, mem_norm_g, w_mem_kv,
           w_up_a, w_up_b, w_up_x, w_out, final_g):
    bsz, seq, d_model = x.shape
    mem_len = mem.shape[1]
    depth = w_in.shape[0]
    sc_w = conv_w.shape[-1]
    nsa_w = NSA_HEADS * HEAD_DIM
    kv_w = NSA_BRANCHES * 2 * NSA_KV_GROUPS * HEAD_DIM
    ng_w = NSA_BRANCHES * NSA_HEADS
    x_w = X_HEADS * HEAD_DIM
    proj_tn = 512

    q_off = 4 * sc_w
    kv_off = q_off + nsa_w
    lo_w = kv_off + kv_w
    nz_off = 0
    xq_off = nz_off + nsa_w
    xz_off = xq_off + x_w
    mg_off = xz_off + x_w
    hi_w = mg_off + N_BRANCHES * d_model
    assert seq % NSA_TQ == 0 and seq // SLC_BLOCK <= LANES and sc_w % 256 == 0
    assert lo_w % proj_tn == 0 and hi_w % proj_tn == 0 and 0 < ng_w < LANES
    assert (lo_w + ng_w) % 16 == 0 and lo_w + ng_w + hi_w == w_in.shape[-1]

    w_in_t = jnp.swapaxes(w_in, 1, 2)
    w_ng = jnp.pad(w_in_t[:, lo_w:lo_w + ng_w, :], ((0, 0), (0, LANES - ng_w), (0, 0)))
    w_ng = jnp.swapaxes(w_ng, 1, 2).astype(BF16)
    w_memkv_b = w_mem_kv.astype(BF16)
    cmp_w1_b = cmp_w1.astype(BF16)
    cmp_w2_b = cmp_w2.astype(BF16)
    norm_g3 = norm_g.reshape(depth, 1, d_model)
    mem_norm_g3 = mem_norm_g.reshape(depth, 1, d_model)
    conv_b3 = conv_b.reshape(depth, 1, sc_w)

    cos, sin_signed = _rope_tables(seq)
    n_half = seq // CMP_STRIDE
    ovl = _overlap_table(seq, n_half)
    expand = _expand_table(seq, NSA_TK)

    t = bsz * seq
    xf = x.reshape(t, d_model)
    memf = mem.reshape(bsz * mem_len, d_model)

    for l in range(depth):
        h = _rmsnorm(xf, norm_g3, l, BF16)
        proj_lo = _matmul_ws(h, w_in_t, l, 0, lo_w, F32, 1024, proj_tn, "in_proj_lo")
        proj_hi = _matmul_ws(h, w_in_t, l, lo_w + ng_w, hi_w, F32, 1024, proj_tn, "in_proj_hi")
        ng = _matmul(h, w_ng, l, F32, 1024, LANES, name="gate_proj")

        y_a = _conv_branch(proj_lo, conv_w, conv_b3, l, bsz, seq, sc_w)

        hm = _rmsnorm(memf, mem_norm_g3, l, BF16)
        memkv = _matmul(hm, w_memkv_b, l, F32, 512, 1024, name="mem_kv_proj")
        y_x = _mem_attention(proj_hi, memkv, bsz, seq, mem_len, xq_off // HEAD_DIM, xz_off // HEAD_DIM)

        cmpkv = _compress(proj_lo, cmp_pos, cmp_w1_b, cmp_w2_b, l, bsz, seq, kv_off // HEAD_DIM)
        qb, qr, ks, vst, kw, vwt = _nsa_prep(proj_lo, cos, sin_signed, bsz, seq, q_off, kv_off)
        y_b = _nsa_attention(qb, qr, cmpkv, ks, vst, kw, vwt, ng, proj_hi, ovl, expand,
                             bsz, seq, nz_off // (NSA_REP * HEAD_DIM))

        u = _merge(y_a, y_b, y_x, w_up_a, w_up_b, w_up_x, proj_hi, l, mg_off, d_model)
        xf = _matmul_ws_res(u, w_out, l, xf, F32, 1024, 512, "out_proj")

    out = _rmsnorm(xf, final_g.reshape(1, 1, d_model), 0, F32)
    return out.reshape(bsz, seq, d_model)
```

```python
import functools

import numpy as np
import jax
import jax.numpy as jnp
from jax import lax
from jax.experimental import pallas as pl
from jax.experimental.pallas import tpu as pltpu

F32 = jnp.float32
BF16 = jnp.bfloat16

HEAD_DIM = 128
ROPE_THETA = 10000.0
EPS = 1e-6
NEG = -1e30
SC_KERNEL = 3
NSA_HEADS = 16
NSA_KV_GROUPS = 4
NSA_REP = NSA_HEADS // NSA_KV_GROUPS
NSA_BRANCHES = 3
CMP_BLOCK = 32
CMP_STRIDE = 16
SLC_BLOCK = 64
SLC_TOP_N = 16
WINDOW = 512
X_HEADS = 4
N_BRANCHES = 3
SCALE = HEAD_DIM ** -0.5
LOG2E = 1.4426950408889634

V7X_VMEM_BYTES = 64 * 1024 * 1024
VMEM_LIMIT_BYTES = V7X_VMEM_BYTES - 8 * 1024 * 1024
LANES = 128

NSA_TQ = 512
NSA_TK = 256
VT_PAD = 16


def _cparams(*sem):
    return pltpu.CompilerParams(dimension_semantics=sem, vmem_limit_bytes=VMEM_LIMIT_BYTES)


def _sigmoid(x):
    return 0.5 * jnp.tanh(0.5 * x) + 0.5


def _silu(x):
    return x * _sigmoid(x)


def _rmsnorm_kernel(x_ref, g_ref, o_ref):
    x = x_ref[...]
    ms = jnp.mean(x * x, axis=-1, keepdims=True)
    o_ref[...] = (x * lax.rsqrt(ms + EPS) * g_ref[...]).astype(o_ref.dtype)


def _rmsnorm(x, g3, l, out_dtype, tm=256):
    m, d = x.shape
    return pl.pallas_call(
        _rmsnorm_kernel,
        grid=(m // tm,),
        in_specs=[pl.BlockSpec((tm, d), lambda i: (i, 0)),
                  pl.BlockSpec((None, 1, d), lambda i: (l, 0, 0))],
        out_specs=pl.BlockSpec((tm, d), lambda i: (i, 0)),
        out_shape=jax.ShapeDtypeStruct((m, d), out_dtype),
        compiler_params=_cparams("parallel"),
        name="rmsnorm",
    )(x, g3)


def _mm_kernel(a_ref, b_ref, o_ref):
    o_ref[...] = jnp.dot(a_ref[...], b_ref[...], preferred_element_type=F32).astype(o_ref.dtype)


def _matmul(a, b, l, out_dtype, tm, tn, name):
    m, kdim = a.shape
    n = b.shape[-1]
    tm, tn = min(tm, m), min(tn, n)
    return pl.pallas_call(
        _mm_kernel,
        grid=(m // tm, n // tn),
        in_specs=[pl.BlockSpec((tm, kdim), lambda i, j: (i, 0)),
                  pl.BlockSpec((None, kdim, tn), lambda i, j: (l, 0, j))],
        out_specs=pl.BlockSpec((tm, tn), lambda i, j: (i, j)),
        out_shape=jax.ShapeDtypeStruct((m, n), out_dtype),
        compiler_params=_cparams("parallel", "arbitrary"),
        name=name,
    )(a, b)


def _mm_nt_kernel(a_ref, w_ref, o_ref):
    o_ref[...] = lax.dot_general(a_ref[...], w_ref[0].astype(BF16), (((1,), (1,)), ((), ())),
                                 preferred_element_type=F32).astype(o_ref.dtype)


def _matmul_nt(a, wt, l, row0, n, out_dtype, tm, tn, name):
    m, kdim = a.shape
    w_spec = pl.BlockSpec((pl.Element(1), pl.Element(tn), pl.Element(kdim)),
                          lambda i, j: (l, pl.multiple_of(row0 + j * tn, 16), 0))
    return pl.pallas_call(
        _mm_nt_kernel,
        grid=(m // tm, n // tn),
        in_specs=[pl.BlockSpec((tm, kdim), lambda i, j: (i, 0), pipeline_mode=pl.Buffered(1)),
                  w_spec],
        out_specs=pl.BlockSpec((tm, tn), lambda i, j: (i, j)),
        out_shape=jax.ShapeDtypeStruct((m, n), out_dtype),
        compiler_params=_cparams("parallel", "arbitrary"),
        name=name,
    )(a, wt)


def _mm_ws_res_kernel(a_ref, w_ref, r_ref, o_ref, wb_ref):
    @pl.when(pl.program_id(1) == 0)
    def _():
        wb_ref[...] = w_ref[...].astype(BF16)

    acc = jnp.dot(a_ref[...], wb_ref[...], preferred_element_type=F32)
    o_ref[...] = (r_ref[...] + acc).astype(o_ref.dtype)


def _matmul_ws_res(a, w, l, residual, out_dtype, tm, tn, name):
    m, kdim = a.shape
    n = w.shape[-1]
    return pl.pallas_call(
        _mm_ws_res_kernel,
        grid=(n // tn, m // tm),
        in_specs=[pl.BlockSpec((tm, kdim), lambda j, i: (i, 0)),
                  pl.BlockSpec((None, kdim, tn), lambda j, i: (l, 0, j)),
                  pl.BlockSpec((tm, tn), lambda j, i: (i, j))],
        out_specs=pl.BlockSpec((tm, tn), lambda j, i: (i, j)),
        out_shape=jax.ShapeDtypeStruct((m, n), out_dtype),
        scratch_shapes=[pltpu.VMEM((kdim, tn), BF16)],
        compiler_params=_cparams("arbitrary", "arbitrary"),
        name=name,
    )(a, w, residual)


def _conv_kernel(h_ref, b_ref, c_ref, z_ref, w_ref, cb_ref, o_ref):
    u = c_ref[...] * h_ref[...]
    row = lax.broadcasted_iota(jnp.int32, u.shape, 0)
    u1 = jnp.where(row >= 1, pltpu.roll(u, 1, 0), 0.0)
    u2 = jnp.where(row >= 2, pltpu.roll(u, 2, 0), 0.0)
    w = w_ref[...]
    y = cb_ref[...] + w[0:1, :] * u2
    y = y + w[1:2, :] * u1
    y = y + w[2:3, :] * u
    o_ref[...] = (b_ref[...] * y * _silu(z_ref[...])).astype(o_ref.dtype)


def _conv_branch(proj, conv_w, conv_b3, l, bsz, seq, width, tc=256):
    nb = width // tc
    blk = lambda off: pl.BlockSpec((seq, tc), lambda b, j: (b, off * nb + j))
    return pl.pallas_call(
        _conv_kernel,
        grid=(bsz, nb),
        in_specs=[blk(0), blk(1), blk(2), blk(3),
                  pl.BlockSpec((None, SC_KERNEL, tc), lambda b, j: (l, 0, j)),
                  pl.BlockSpec((None, 1, tc), lambda b, j: (l, 0, j))],
        out_specs=pl.BlockSpec((seq, tc), lambda b, j: (b, j)),
        out_shape=jax.ShapeDtypeStruct((bsz * seq, width), BF16),
        compiler_params=_cparams("parallel", "parallel"),
        name="conv_branch",
    )(proj, proj, proj, proj, conv_w, conv_b3)


def _memattn_kernel(q_ref, k_ref, v_ref, z_ref, o_ref):
    q = q_ref[...].astype(BF16)
    k = k_ref[...].astype(BF16)
    v = v_ref[...].astype(BF16)
    s = lax.dot_general(q, k, (((1,), (1,)), ((), ())), preferred_element_type=F32) * SCALE
    m = jnp.max(s, axis=-1, keepdims=True)
    e = jnp.exp(s - m)
    p = e / jnp.sum(e, axis=-1, keepdims=True)
    o = jnp.dot(p.astype(BF16), v, preferred_element_type=F32)
    o_ref[...] = (o * _silu(z_ref[...])).astype(o_ref.dtype)


def _mem_attention(proj, memkv, bsz, seq, mem_len, q_blk, z_blk, tq=512):
    nq = seq // tq
    return pl.pallas_call(
        _memattn_kernel,
        grid=(bsz, X_HEADS, nq),
        in_specs=[pl.BlockSpec((tq, HEAD_DIM), lambda b, h, i: (b * nq + i, q_blk + h)),
                  pl.BlockSpec((mem_len, HEAD_DIM), lambda b, h, i: (b, h)),
                  pl.BlockSpec((mem_len, HEAD_DIM), lambda b, h, i: (b, X_HEADS + h)),
                  pl.BlockSpec((tq, HEAD_DIM), lambda b, h, i: (b * nq + i, z_blk + h))],
        out_specs=pl.BlockSpec((tq, HEAD_DIM), lambda b, h, i: (b * nq + i, h)),
        out_shape=jax.ShapeDtypeStruct((bsz * seq, X_HEADS * HEAD_DIM), BF16),
        compiler_params=_cparams("parallel", "parallel", "parallel"),
        name="mem_attention",
    )(proj, memkv, memkv, proj)


def _compress_kernel(kv_ref, pos_ref, w1_ref, w2_ref, o_ref, *, n_half):
    half = CMP_BLOCK // 2
    acc_a = jnp.zeros((n_half, HEAD_DIM), F32)
    acc_b = jnp.zeros((n_half, HEAD_DIM), F32)
    for t in range(half):
        x = kv_ref[pl.ds(t, n_half, stride=half), :]
        xa = (x + pos_ref[t:t + 1, :]).astype(BF16)
        xb = (x + pos_ref[half + t:half + t + 1, :]).astype(BF16)
        acc_a = acc_a + jnp.dot(xa, w1_ref[t * HEAD_DIM:(t + 1) * HEAD_DIM, :],
                                preferred_element_type=F32)
        acc_b = acc_b + jnp.dot(xb, w1_ref[(half + t) * HEAD_DIM:(half + t + 1) * HEAD_DIM, :],
                                preferred_element_type=F32)
    hid = _silu(acc_a + pltpu.roll(acc_b, n_half - 1, 0))
    o_ref[...] = jnp.dot(hid.astype(BF16), w2_ref[...], preferred_element_type=F32)


def _compress(proj, cmp_pos, cmp_w1, cmp_w2, l, bsz, seq, kv_blk):
    g = NSA_KV_GROUPS
    n_half = seq // (CMP_BLOCK // 2)
    return pl.pallas_call(
        functools.partial(_compress_kernel, n_half=n_half),
        grid=(bsz, 2, g),
        in_specs=[pl.BlockSpec((seq, HEAD_DIM), lambda b, c, gi: (b, kv_blk + c * g + gi)),
                  pl.BlockSpec((None, None, CMP_BLOCK, HEAD_DIM), lambda b, c, gi: (l, c, 0, 0)),
                  pl.BlockSpec((None, None, CMP_BLOCK * HEAD_DIM, HEAD_DIM),
                               lambda b, c, gi: (l, c, 0, 0)),
                  pl.BlockSpec((None, None, HEAD_DIM, HEAD_DIM), lambda b, c, gi: (l, c, 0, 0))],
        out_specs=pl.BlockSpec((None, None, None, n_half, HEAD_DIM),
                               lambda b, c, gi: (b, c, gi, 0, 0)),
        out_shape=jax.ShapeDtypeStruct((bsz, 2, g, n_half, HEAD_DIM), F32),
        compiler_params=_cparams("parallel", "parallel", "parallel"),
        name="nsa_compress",
    )(proj, cmp_pos, cmp_w1, cmp_w2)


def _rope_tile(x, cos, sin_signed):
    return x * cos + pltpu.roll(x, HEAD_DIM // 2, 1) * sin_signed


def _nsa_prep_kernel(q_ref, ks_ref, vs_ref, kw_ref, vw_ref, cos_ref, sin_ref,
                     qb_ref, qr_ref, kso_ref, vst_ref, kwo_ref, vwt_ref):
    cos = cos_ref[...]
    sin = sin_ref[...]
    for h in range(NSA_HEADS):
        sl = slice(h * HEAD_DIM, (h + 1) * HEAD_DIM)
        x = q_ref[:, sl]
        qb_ref[:, sl] = x.astype(BF16)
        qr_ref[:, sl] = (_rope_tile(x, cos, sin) * (SCALE * LOG2E)).astype(BF16)
    n_tiles, _, width = vst_ref.shape
    for g in range(NSA_KV_GROUPS):
        sl = slice(g * HEAD_DIM, (g + 1) * HEAD_DIM)
        kso_ref[:, sl] = _rope_tile(ks_ref[:, sl], cos, sin).astype(BF16)
        kwo_ref[:, sl] = _rope_tile(kw_ref[:, sl], cos, sin).astype(BF16)
        vs_t = jnp.transpose(vs_ref[:, sl]).astype(BF16)
        vw_t = jnp.transpose(vw_ref[:, sl]).astype(BF16)
        for c in range(n_tiles):
            vst_ref[c, sl, :] = vs_t[:, c * width:(c + 1) * width]
            vwt_ref[c, sl, :] = vw_t[:, c * width:(c + 1) * width]


def _nsa_prep(proj, cos, sin_signed, bsz, seq, q_off, kv_off):
    tr = NSA_TQ
    nsa_w = NSA_HEADS * HEAD_DIM
    gw = NSA_KV_GROUPS * HEAD_DIM
    nr = seq // tr
    t = bsz * seq
    qblk = q_off // nsa_w
    kvb = kv_off // gw
    n_tiles = tr // NSA_TK
    rowq = lambda w, cb: pl.BlockSpec((tr, w), lambda b, i: (b * nr + i, cb))
    tab = pl.BlockSpec((tr, HEAD_DIM), lambda b, i: (i, 0))
    vt_spec = pl.BlockSpec((None, n_tiles, gw, NSA_TK), lambda b, i: (b, i, 0, 0))
    vt_shape = jax.ShapeDtypeStruct((bsz, seq // NSA_TK, gw, NSA_TK), BF16)
    return pl.pallas_call(
        _nsa_prep_kernel,
        grid=(bsz, nr),
        in_specs=[rowq(nsa_w, qblk), rowq(gw, kvb + 2), rowq(gw, kvb + 3),
                  rowq(gw, kvb + 4), rowq(gw, kvb + 5), tab, tab],
        out_specs=[rowq(nsa_w, 0), rowq(nsa_w, 0), rowq(gw, 0), vt_spec, rowq(gw, 0), vt_spec],
        out_shape=[jax.ShapeDtypeStruct((t, nsa_w), BF16), jax.ShapeDtypeStruct((t, nsa_w), BF16),
                   jax.ShapeDtypeStruct((t, gw), BF16), vt_shape,
                   jax.ShapeDtypeStruct((t, gw), BF16), vt_shape],
        compiler_params=_cparams("parallel", "parallel"),
        name="nsa_prep",
    )(proj, proj, proj, proj, proj, cos, sin_signed)


def _col_reduce(x, op, ways=4):
    nblk = x.shape[0] // 8
    accs = [x[8 * k:8 * (k + 1)] for k in range(min(ways, nblk))]
    for k in range(len(accs), nblk):
        accs[k % ways] = op(accs[k % ways], x[8 * k:8 * (k + 1)])
    while len(accs) > 1:
        accs = [op(accs[a], accs[a + 1]) for a in range(0, len(accs), 2)]
    return accs[0]


def _col_max(x):
    return jnp.max(_col_reduce(x, jnp.maximum), axis=0, keepdims=True)


def _col_sum(x):
    return jnp.sum(_col_reduce(x, jnp.add), axis=0, keepdims=True)


def _nsa_kernel(qb_ref, qr_ref, kc_ref, vc_ref, ks_ref, vst_ref, kw_ref, vwt_ref,
                ng_ref, nz_ref, ovl_ref, exp_ref, o_ref,
                qa_ref, ts0_ref, ts1_ref, tw0_ref, tw1_ref, xs0_ref, xs1_ref, xw0_ref, xw1_ref,
                ms_ref, mw_ref, accs_ref, accw_ref, ocmp_ref, ngt_ref,
                *, tq, tk, n_slc):
    assert tq == 2 * tk and WINDOW == tq
    gi = pl.program_id(1)
    i = pl.program_id(2)
    q0 = i * tq
    nt = (((1,), (1,)), ((), ()))
    pos_row = q0 + lax.broadcasted_iota(jnp.int32, (1, tq), 1)
    pos_all = jnp.concatenate([pos_row] * NSA_REP, axis=1)
    heads = [slice(r * HEAD_DIM, (r + 1) * HEAD_DIM) for r in range(NSA_REP)]
    cols = [slice(r * tq, (r + 1) * tq) for r in range(NSA_REP)]

    qb = jnp.concatenate([qb_ref[:, h] for h in heads], axis=0)
    kc = kc_ref[...].astype(BF16)
    vc_t = jnp.transpose(vc_ref[...]).astype(BF16)
    n_cmp = kc.shape[0]
    cmp_end = CMP_STRIDE * lax.broadcasted_iota(jnp.int32, (n_cmp, 1), 0) + (CMP_BLOCK - 1)
    cmask = cmp_end <= pos_all
    s = lax.dot_general(kc, qb, nt, preferred_element_type=F32) * SCALE
    s = jnp.where(cmask, s, NEG)
    e = jnp.exp(s - _col_max(s))
    p = e * (1.0 / _col_sum(e))
    p = jnp.where(cmask, p, 0.0).astype(BF16)
    ocmp_ref[...] = jnp.dot(vc_t, p, preferred_element_type=F32)
    imp_all = jnp.dot(ovl_ref[...], p, preferred_element_type=F32)
    imp = imp_all[:, cols[0]]
    for c in cols[1:]:
        imp = imp + imp_all[:, c]

    val = imp[0:n_slc, :]
    blk = lax.broadcasted_iota(jnp.int32, (n_slc, 1), 0)
    cur = pos_row // SLC_BLOCK
    future = blk * SLC_BLOCK > pos_row
    forced = (blk == 0) | (blk == cur) | (blk == cur - 1)
    val = jnp.where(future, -jnp.inf, jnp.where(forced, jnp.inf, val))
    rank = jnp.zeros((n_slc, tq), jnp.int32)
    for j in range(n_slc):
        vj = val[j:j + 1, :]
        beats = (vj > val) | ((vj == val) & (blk > j))
        rank = rank + beats.astype(jnp.int32)
    sel_bias = jnp.where(rank < min(SLC_TOP_N, n_slc), 0.0, NEG)
    sel_bias = jnp.concatenate([sel_bias, jnp.zeros((LANES - n_slc, tq), F32)], axis=0)
    sel_bias = jnp.transpose(sel_bias).astype(BF16)
    qa_ref[:, 0:HEAD_DIM] = jnp.concatenate([qr_ref[:, h] for h in heads], axis=0)
    qa_ref[:, HEAD_DIM:] = jnp.concatenate([sel_bias] * NSA_REP, axis=0)

    ones_rows = (lax.broadcasted_iota(jnp.int32, (VT_PAD, tk), 0) == 0).astype(BF16)

    def scores(t_ref, x_ref, keys, queries, visible):
        t = lax.dot_general(keys, queries, nt, preferred_element_type=F32)
        if visible is not None:
            t = jnp.where(visible, t, NEG)
        t_ref[...] = t
        x_ref[...] = _col_max(t)

    def absorb(t_ref, x_ref, vt, m_ref, acc_ref):
        m_old = m_ref[...]
        m_new = jnp.maximum(m_old, x_ref[...])
        alpha = jnp.exp2(m_old - m_new)
        e = jnp.exp2(t_ref[...] - m_new).astype(BF16)
        vt_ones = jnp.concatenate([vt, ones_rows], axis=0)
        acc_ref[...] = alpha * acc_ref[...] + jnp.dot(vt_ones, e, preferred_element_type=F32)
        m_ref[...] = m_new

    def key_pos(c):
        return c * tk + lax.broadcasted_iota(jnp.int32, (tk, 1), 0)

    def slc_scores(t_ref, x_ref, c, causal):
        start = pl.multiple_of(c * tk, tk)
        keys = jnp.concatenate([ks_ref[pl.ds(start, tk), :], exp_ref[c]], axis=1)
        scores(t_ref, x_ref, keys, qa_ref[...], (key_pos(c) <= pos_all) if causal else None)

    def slc_absorb(t_ref, x_ref, c):
        absorb(t_ref, x_ref, vst_ref[c], ms_ref, accs_ref)

    slc_scores(ts0_ref, xs0_ref, 2 * i, True)
    slc_scores(ts1_ref, xs1_ref, 2 * i + 1, True)
    ms_ref[...] = jnp.maximum(xs0_ref[...], xs1_ref[...])
    accs_ref[...] = jnp.zeros(accs_ref.shape, F32)
    slc_absorb(ts0_ref, xs0_ref, 2 * i)

    def slc_pair(j, carry):
        slc_scores(ts0_ref, xs0_ref, 2 * j, False)
        slc_absorb(ts1_ref, xs1_ref, jnp.where(j == 0, 2 * i + 1, 2 * j - 1))
        slc_scores(ts1_ref, xs1_ref, 2 * j + 1, False)
        slc_absorb(ts0_ref, xs0_ref, 2 * j)
        return carry

    lax.fori_loop(0, i, slc_pair, 0)
    slc_absorb(ts1_ref, xs1_ref, jnp.where(i == 0, 1, 2 * i - 1))

    low = jnp.maximum(pos_all - WINDOW, -1)

    def win_scores(t_ref, x_ref, c, causal):
        kpos = key_pos(c)
        start = pl.multiple_of(jnp.maximum(c, 0) * tk, tk)
        scores(t_ref, x_ref, kw_ref[pl.ds(start, tk), :], qa_ref[:, 0:HEAD_DIM],
               (kpos <= pos_all) if causal else (kpos > low))

    def win_absorb(t_ref, x_ref, c):
        absorb(t_ref, x_ref, vwt_ref[jnp.maximum(c, 0)], mw_ref, accw_ref)

    mw_ref[...] = jnp.full(mw_ref.shape, NEG, F32)
    accw_ref[...] = jnp.zeros(accw_ref.shape, F32)
    win_scores(tw0_ref, xw0_ref, 2 * i, True)
    win_scores(tw1_ref, xw1_ref, 2 * i + 1, True)
    win_absorb(tw0_ref, xw0_ref, 2 * i)
    win_scores(tw0_ref, xw0_ref, 2 * i - 1, False)
    win_absorb(tw1_ref, xw1_ref, 2 * i + 1)
    win_scores(tw1_ref, xw1_ref, 2 * i - 2, False)
    win_absorb(tw0_ref, xw0_ref, 2 * i - 1)
    win_absorb(tw1_ref, xw1_ref, 2 * i - 2)

    ngt_ref[...] = jnp.transpose(ng_ref[...])
    inv_ls = 1.0 / accs_ref[HEAD_DIM:HEAD_DIM + 1, :]
    inv_lw = 1.0 / accw_ref[HEAD_DIM:HEAD_DIM + 1, :]
    for r in range(NSA_REP):
        row = (gi * NSA_REP + r) * NSA_BRANCHES
        g_cmp = _sigmoid(ngt_ref[pl.ds(row, 1), :])
        g_slc = _sigmoid(ngt_ref[pl.ds(row + 1, 1), :])
        g_win = _sigmoid(ngt_ref[pl.ds(row + 2, 1), :])
        o_t = (g_cmp * ocmp_ref[:, cols[r]]
               + (g_slc * inv_ls[:, cols[r]]) * accs_ref[0:HEAD_DIM, cols[r]]
               + (g_win * inv_lw[:, cols[r]]) * accw_ref[0:HEAD_DIM, cols[r]])
        o_ref[:, heads[r]] = (jnp.transpose(o_t) * _silu(nz_ref[:, heads[r]])).astype(o_ref.dtype)


def _nsa_attention(qb, qr, cmpkv, ks, vst, kw, vwt, ng, proj, ovl, expand, bsz, seq, nz_blk):
    tq, tk = NSA_TQ, NSA_TK
    g = NSA_KV_GROUPS
    gw = NSA_REP * HEAD_DIM
    nq = seq // tq
    n_slc = seq // SLC_BLOCK
    n_half = cmpkv.shape[-2]
    stacked = NSA_REP * tq
    rowq = lambda cb: pl.BlockSpec((tq, gw), lambda b, gi, i: (b * nq + i, cb + gi))
    kvs = pl.BlockSpec((seq, HEAD_DIM), lambda b, gi, i: (b, gi))
    cmps = lambda c: pl.BlockSpec((None, None, None, n_half, HEAD_DIM),
                                  lambda b, gi, i: (b, c, gi, 0, 0))
    vts = pl.BlockSpec((None, seq // tk, HEAD_DIM, tk), lambda b, gi, i: (b, 0, gi, 0))
    score_buf = pltpu.VMEM((tk, stacked), F32)
    stat_row = pltpu.VMEM((1, stacked), F32)
    flash_acc = pltpu.VMEM((HEAD_DIM + VT_PAD, stacked), F32)
    return pl.pallas_call(
        functools.partial(_nsa_kernel, tq=tq, tk=tk, n_slc=n_slc),
        grid=(bsz, g, nq),
        in_specs=[rowq(0), rowq(0), cmps(0), cmps(1), kvs, vts, kvs, vts,
                  pl.BlockSpec((tq, LANES), lambda b, gi, i: (b * nq + i, 0)),
                  rowq(nz_blk),
                  pl.BlockSpec((LANES, n_half), lambda b, gi, i: (0, 0)),
                  pl.BlockSpec((seq // tk, tk, LANES), lambda b, gi, i: (0, 0, 0))],
        out_specs=rowq(0),
        out_shape=jax.ShapeDtypeStruct((bsz * seq, NSA_HEADS * HEAD_DIM), BF16),
        scratch_shapes=[pltpu.VMEM((stacked, 2 * HEAD_DIM), BF16),
                        score_buf, score_buf, score_buf, score_buf,
                        stat_row, stat_row, stat_row, stat_row,
                        stat_row, stat_row,
                        flash_acc, flash_acc, pltpu.VMEM((HEAD_DIM, stacked), F32),
                        pltpu.VMEM((LANES, tq), F32)],
        compiler_params=_cparams("parallel", "parallel", "arbitrary"),
        name="nsa_attention",
    )(qb, qr, cmpkv, cmpkv, ks, vst, kw, vwt, ng, proj, ovl, expand)


def _merge_kernel(ya_ref, yb_ref, yx_ref, wa_ref, wb_ref, wx_ref, g0_ref, g1_ref, g2_ref, o_ref,
                  wab_ref, wbb_ref, wxb_ref):
    @pl.when(pl.program_id(1) == 0)
    def _():
        wab_ref[...] = wa_ref[...].astype(BF16)
        wbb_ref[...] = wb_ref[...].astype(BF16)
        wxb_ref[...] = wx_ref[...].astype(BF16)

    u = _sigmoid(g0_ref[...]) * jnp.dot(ya_ref[...], wab_ref[...], preferred_element_type=F32)
    u = u + _sigmoid(g1_ref[...]) * jnp.dot(yb_ref[...], wbb_ref[...], preferred_element_type=F32)
    u = u + _sigmoid(g2_ref[...]) * jnp.dot(yx_ref[...], wxb_ref[...], preferred_element_type=F32)
    o_ref[...] = u.astype(o_ref.dtype)


def _merge(ya, yb, yx, wa, wb, wx, proj, l, mg_off, d_model, tm=1024, tn=512):
    t = ya.shape[0]
    nj = d_model // tn
    mgb = mg_off // tn
    act = lambda w: pl.BlockSpec((tm, w), lambda j, i: (i, 0))
    wsp = lambda w: pl.BlockSpec((None, w, tn), lambda j, i: (l, 0, j),
                                 pipeline_mode=pl.Buffered(1))
    gsp = lambda br: pl.BlockSpec((tm, tn), lambda j, i: (i, mgb + br * nj + j))
    return pl.pallas_call(
        _merge_kernel,
        grid=(nj, t // tm),
        in_specs=[act(ya.shape[1]), act(yb.shape[1]), act(yx.shape[1]),
                  wsp(wa.shape[1]), wsp(wb.shape[1]), wsp(wx.shape[1]),
                  gsp(0), gsp(1), gsp(2)],
        out_specs=pl.BlockSpec((tm, tn), lambda j, i: (i, j)),
        out_shape=jax.ShapeDtypeStruct((t, d_model), BF16),
        scratch_shapes=[pltpu.VMEM((wa.shape[1], tn), BF16), pltpu.VMEM((wb.shape[1], tn), BF16),
                        pltpu.VMEM((wx.shape[1], tn), BF16)],
        compiler_params=_cparams("arbitrary", "arbitrary"),
        name="gated_merge",
    )(ya, yb, yx, wa, wb, wx, proj, proj, proj)


def _rope_tables(seq):
    half = HEAD_DIM // 2
    inv_freq = ROPE_THETA ** (-jnp.arange(half, dtype=F32) / half)
    ang = jnp.arange(seq, dtype=jnp.int32).astype(F32)[:, None] * inv_freq[None, :]
    cos, sin = jnp.cos(ang), jnp.sin(ang)
    return jnp.concatenate([cos, cos], axis=-1), jnp.concatenate([-sin, sin], axis=-1)


def _overlap_table(seq, n_half):
    n_slc = seq // SLC_BLOCK
    cmp_start = np.arange(n_half) * CMP_STRIDE
    slc_start = np.arange(n_slc) * SLC_BLOCK
    ovl = ((cmp_start[None, :] < slc_start[:, None] + SLC_BLOCK)
           & (cmp_start[None, :] + CMP_BLOCK > slc_start[:, None]))
    ovl = ovl & (cmp_start[None, :] + CMP_BLOCK <= seq)
    out = np.zeros((LANES, n_half), np.float32)
    out[:n_slc, :] = ovl
    return jnp.asarray(out, BF16)


def _expand_table(seq, tk):
    key_blk = np.arange(seq) // SLC_BLOCK
    e = (key_blk[:, None] == np.arange(LANES)[None, :]).astype(np.float32)
    return jnp.asarray(e.reshape(seq // tk, tk, LANES), BF16)


def kernel(x, mem, norm_g, w_in, conv_w, conv_b, cmp_pos, cmp_w1, cmp_w2, mem_norm_g, w_mem_kv,
           w_up_a, w_up_b, w_up_x, w_out, final_g):
    bsz, seq, d_model = x.shape
    mem_len = mem.shape[1]
    depth = w_in.shape[0]
    sc_w = conv_w.shape[-1]
    nsa_w = NSA_HEADS * HEAD_DIM
    kv_w = NSA_BRANCHES * 2 * NSA_KV_GROUPS * HEAD_DIM
    ng_w = NSA_BRANCHES * NSA_HEADS
    x_w = X_HEADS * HEAD_DIM
    proj_tn = 512

    q_off = 4 * sc_w
    kv_off = q_off + nsa_w
    lo_w = kv_off + kv_w
    nz_off = 0
    xq_off = nz_off + nsa_w
    xz_off = xq_off + x_w
    mg_off = xz_off + x_w
    hi_w = mg_off + N_BRANCHES * d_model
    assert seq % NSA_TQ == 0 and seq // SLC_BLOCK <= LANES and sc_w % 256 == 0
    assert lo_w % proj_tn == 0 and hi_w % proj_tn == 0 and 0 < ng_w < LANES
    assert (lo_w + ng_w) % 16 == 0 and lo_w + ng_w + hi_w == w_in.shape[-1]

    w_in_t = jnp.swapaxes(w_in, 1, 2)
    w_ng = jnp.pad(w_in_t[:, lo_w:lo_w + ng_w, :], ((0, 0), (0, LANES - ng_w), (0, 0)))
    w_ng = jnp.swapaxes(w_ng, 1, 2).astype(BF16)
    w_memkv_b = w_mem_kv.astype(BF16)
    cmp_w1_b = cmp_w1.astype(BF16)
    cmp_w2_b = cmp_w2.astype(BF16)
    norm_g3 = norm_g.reshape(depth, 1, d_model)
    mem_norm_g3 = mem_norm_g.reshape(depth, 1, d_model)
    conv_b3 = conv_b.reshape(depth, 1, sc_w)

    cos, sin_signed = _rope_tables(seq)
    n_half = seq // CMP_STRIDE
    ovl = _overlap_table(seq, n_half)
    expand = _expand_table(seq, NSA_TK)

    t = bsz * seq
    xf = x.reshape(t, d_model)
    memf = mem.reshape(bsz * mem_len, d_model)

    for l in range(depth):
        h = _rmsnorm(xf, norm_g3, l, BF16)
        proj_lo = _matmul_nt(h, w_in_t, l, 0, lo_w, F32, 2048, proj_tn, "in_proj_lo")
        proj_hi = _matmul_nt(h, w_in_t, l, lo_w + ng_w, hi_w, F32, 2048, proj_tn, "in_proj_hi")
        ng = _matmul(h, w_ng, l, F32, 1024, LANES, name="gate_proj")

        y_a = _conv_branch(proj_lo, conv_w, conv_b3, l, bsz, seq, sc_w)

        hm = _rmsnorm(memf, mem_norm_g3, l, BF16)
        memkv = _matmul(hm, w_memkv_b, l, F32, 512, 1024, name="mem_kv_proj")
        y_x = _mem_attention(proj_hi, memkv, bsz, seq, mem_len, xq_off // HEAD_DIM, xz_off // HEAD_DIM)

        cmpkv = _compress(proj_lo, cmp_pos, cmp_w1_b, cmp_w2_b, l, bsz, seq, kv_off // HEAD_DIM)
        qb, qr, ks, vst, kw, vwt = _nsa_prep(proj_lo, cos, sin_signed, bsz, seq, q_off, kv_off)
        y_b = _nsa_attention(qb, qr, cmpkv, ks, vst, kw, vwt, ng, proj_hi, ovl, expand,
                             bsz, seq, nz_off // (NSA_REP * HEAD_DIM))

        u = _merge(y_a, y_b, y_x, w_up_a, w_up_b, w_up_x, proj_hi, l, mg_off, d_model)
        xf = _matmul_ws_res(u, w_out, l, xf, F32, 1024, 512, "out_proj")

    out = _rmsnorm(xf, final_g.reshape(1, 1, d_model), 0, F32)
    return out.reshape(bsz, seq, d_model)
```

```python
import functools

import numpy as np
import jax
import jax.numpy as jnp
from jax import lax
from jax.experimental import pallas as pl
from jax.experimental.pallas import tpu as pltpu

F32 = jnp.float32
BF16 = jnp.bfloat16

HEAD_DIM = 128
ROPE_THETA = 10000.0
EPS = 1e-6
NEG = -1e30
SC_KERNEL = 3
NSA_HEADS = 16
NSA_KV_GROUPS = 4
NSA_REP = NSA_HEADS // NSA_KV_GROUPS
NSA_BRANCHES = 3
CMP_BLOCK = 32
CMP_STRIDE = 16
SLC_BLOCK = 64
SLC_TOP_N = 16
WINDOW = 512
X_HEADS = 4
N_BRANCHES = 3
SCALE = HEAD_DIM ** -0.5
LOG2E = 1.4426950408889634

V7X_VMEM_BYTES = 64 * 1024 * 1024
VMEM_LIMIT_BYTES = V7X_VMEM_BYTES - 8 * 1024 * 1024
LANES = 128

NSA_TQ = 512
NSA_TK = 256
VT_PAD = 16


def _cparams(*sem):
    return pltpu.CompilerParams(dimension_semantics=sem, vmem_limit_bytes=VMEM_LIMIT_BYTES)


def _sigmoid(x):
    return 0.5 * jnp.tanh(0.5 * x) + 0.5


def _silu(x):
    return x * _sigmoid(x)


def _rmsnorm_kernel(x_ref, g_ref, o_ref):
    x = x_ref[...]
    ms = jnp.mean(x * x, axis=-1, keepdims=True)
    o_ref[...] = (x * lax.rsqrt(ms + EPS) * g_ref[...]).astype(o_ref.dtype)


def _rmsnorm(x, g3, l, out_dtype, tm=256):
    m, d = x.shape
    return pl.pallas_call(
        _rmsnorm_kernel,
        grid=(m // tm,),
        in_specs=[pl.BlockSpec((tm, d), lambda i: (i, 0)),
                  pl.BlockSpec((None, 1, d), lambda i: (l, 0, 0))],
        out_specs=pl.BlockSpec((tm, d), lambda i: (i, 0)),
        out_shape=jax.ShapeDtypeStruct((m, d), out_dtype),
        compiler_params=_cparams("parallel"),
        name="rmsnorm",
    )(x, g3)


def _mm_kernel(a_ref, b_ref, o_ref):
    o_ref[...] = jnp.dot(a_ref[...], b_ref[...], preferred_element_type=F32).astype(o_ref.dtype)


def _matmul(a, b, l, out_dtype, tm, tn, name):
    m, kdim = a.shape
    n = b.shape[-1]
    tm, tn = min(tm, m), min(tn, n)
    return pl.pallas_call(
        _mm_kernel,
        grid=(m // tm, n // tn),
        in_specs=[pl.BlockSpec((tm, kdim), lambda i, j: (i, 0)),
                  pl.BlockSpec((None, kdim, tn), lambda i, j: (l, 0, j))],
        out_specs=pl.BlockSpec((tm, tn), lambda i, j: (i, j)),
        out_shape=jax.ShapeDtypeStruct((m, n), out_dtype),
        compiler_params=_cparams("parallel", "arbitrary"),
        name=name,
    )(a, b)


def _mm_nt_kernel(a_ref, w_ref, o_ref):
    o_ref[...] = lax.dot_general(a_ref[...], w_ref[0].astype(BF16), (((1,), (1,)), ((), ())),
                                 preferred_element_type=F32).astype(o_ref.dtype)


def _matmul_nt(a, wt, l, row0, n, out_dtype, tm, tn, name):
    m, kdim = a.shape
    w_spec = pl.BlockSpec((pl.Element(1), pl.Element(tn), pl.Element(kdim)),
                          lambda i, j: (l, pl.multiple_of(row0 + j * tn, 16), 0))
    return pl.pallas_call(
        _mm_nt_kernel,
        grid=(m // tm, n // tn),
        in_specs=[pl.BlockSpec((tm, kdim), lambda i, j: (i, 0), pipeline_mode=pl.Buffered(1)),
                  w_spec],
        out_specs=pl.BlockSpec((tm, tn), lambda i, j: (i, j)),
        out_shape=jax.ShapeDtypeStruct((m, n), out_dtype),
        compiler_params=_cparams("parallel", "arbitrary"),
        name=name,
    )(a, wt)


def _mm_res_kernel(a_ref, w_ref, r_ref, o_ref):
    acc = jnp.dot(a_ref[...], w_ref[...].astype(BF16), preferred_element_type=F32)
    o_ref[...] = (r_ref[...] + acc).astype(o_ref.dtype)


def _matmul_res(a, w, l, residual, out_dtype, tm, tn, name):
    m, kdim = a.shape
    n = w.shape[-1]
    return pl.pallas_call(
        _mm_res_kernel,
        grid=(m // tm, n // tn),
        in_specs=[pl.BlockSpec((tm, kdim), lambda i, j: (i, 0), pipeline_mode=pl.Buffered(1)),
                  pl.BlockSpec((None, kdim, tn), lambda i, j: (l, 0, j)),
                  pl.BlockSpec((tm, tn), lambda i, j: (i, j))],
        out_specs=pl.BlockSpec((tm, tn), lambda i, j: (i, j)),
        out_shape=jax.ShapeDtypeStruct((m, n), out_dtype),
        compiler_params=_cparams("parallel", "arbitrary"),
        name=name,
    )(a, w, residual)


def _conv_kernel(h_ref, b_ref, c_ref, z_ref, w_ref, cb_ref, o_ref):
    u = c_ref[...] * h_ref[...]
    row = lax.broadcasted_iota(jnp.int32, u.shape, 0)
    u1 = jnp.where(row >= 1, pltpu.roll(u, 1, 0), 0.0)
    u2 = jnp.where(row >= 2, pltpu.roll(u, 2, 0), 0.0)
    w = w_ref[...]
    y = cb_ref[...] + w[0:1, :] * u2
    y = y + w[1:2, :] * u1
    y = y + w[2:3, :] * u
    o_ref[...] = (b_ref[...] * y * _silu(z_ref[...])).astype(o_ref.dtype)


def _conv_branch(proj, conv_w, conv_b3, l, bsz, seq, width, tc=256):
    nb = width // tc
    blk = lambda off: pl.BlockSpec((seq, tc), lambda b, j: (b, off * nb + j))
    return pl.pallas_call(
        _conv_kernel,
        grid=(bsz, nb),
        in_specs=[blk(0), blk(1), blk(2), blk(3),
                  pl.BlockSpec((None, SC_KERNEL, tc), lambda b, j: (l, 0, j)),
                  pl.BlockSpec((None, 1, tc), lambda b, j: (l, 0, j))],
        out_specs=pl.BlockSpec((seq, tc), lambda b, j: (b, j)),
        out_shape=jax.ShapeDtypeStruct((bsz * seq, width), BF16),
        compiler_params=_cparams("parallel", "parallel"),
        name="conv_branch",
    )(proj, proj, proj, proj, conv_w, conv_b3)


def _memattn_kernel(q_ref, k_ref, v_ref, z_ref, o_ref):
    q = q_ref[...].astype(BF16)
    k = k_ref[...].astype(BF16)
    v = v_ref[...].astype(BF16)
    s = lax.dot_general(q, k, (((1,), (1,)), ((), ())), preferred_element_type=F32) * SCALE
    m = jnp.max(s, axis=-1, keepdims=True)
    e = jnp.exp(s - m)
    p = e / jnp.sum(e, axis=-1, keepdims=True)
    o = jnp.dot(p.astype(BF16), v, preferred_element_type=F32)
    o_ref[...] = (o * _silu(z_ref[...])).astype(o_ref.dtype)


def _mem_attention(proj, memkv, bsz, seq, mem_len, q_blk, z_blk, tq=512):
    nq = seq // tq
    return pl.pallas_call(
        _memattn_kernel,
        grid=(bsz, X_HEADS, nq),
        in_specs=[pl.BlockSpec((tq, HEAD_DIM), lambda b, h, i: (b * nq + i, q_blk + h)),
                  pl.BlockSpec((mem_len, HEAD_DIM), lambda b, h, i: (b, h)),
                  pl.BlockSpec((mem_len, HEAD_DIM), lambda b, h, i: (b, X_HEADS + h)),
                  pl.BlockSpec((tq, HEAD_DIM), lambda b, h, i: (b * nq + i, z_blk + h))],
        out_specs=pl.BlockSpec((tq, HEAD_DIM), lambda b, h, i: (b * nq + i, h)),
        out_shape=jax.ShapeDtypeStruct((bsz * seq, X_HEADS * HEAD_DIM), BF16),
        compiler_params=_cparams("parallel", "parallel", "parallel"),
        name="mem_attention",
    )(proj, memkv, memkv, proj)


def _compress_kernel(kv_ref, pos_ref, w1_ref, w2_ref, o_ref, *, n_half):
    half = CMP_BLOCK // 2
    acc_a = jnp.zeros((n_half, HEAD_DIM), F32)
    acc_b = jnp.zeros((n_half, HEAD_DIM), F32)
    for t in range(half):
        x = kv_ref[pl.ds(t, n_half, stride=half), :]
        xa = (x + pos_ref[t:t + 1, :]).astype(BF16)
        xb = (x + pos_ref[half + t:half + t + 1, :]).astype(BF16)
        acc_a = acc_a + jnp.dot(xa, w1_ref[t * HEAD_DIM:(t + 1) * HEAD_DIM, :],
                                preferred_element_type=F32)
        acc_b = acc_b + jnp.dot(xb, w1_ref[(half + t) * HEAD_DIM:(half + t + 1) * HEAD_DIM, :],
                                preferred_element_type=F32)
    hid = _silu(acc_a + pltpu.roll(acc_b, n_half - 1, 0))
    o_ref[...] = jnp.dot(hid.astype(BF16), w2_ref[...], preferred_element_type=F32)


def _compress(proj, cmp_pos, cmp_w1, cmp_w2, l, bsz, seq, kv_blk):
    g = NSA_KV_GROUPS
    n_half = seq // (CMP_BLOCK // 2)
    return pl.pallas_call(
        functools.partial(_compress_kernel, n_half=n_half),
        grid=(bsz, 2, g),
        in_specs=[pl.BlockSpec((seq, HEAD_DIM), lambda b, c, gi: (b, kv_blk + c * g + gi)),
                  pl.BlockSpec((None, None, CMP_BLOCK, HEAD_DIM), lambda b, c, gi: (l, c, 0, 0)),
                  pl.BlockSpec((None, None, CMP_BLOCK * HEAD_DIM, HEAD_DIM),
                               lambda b, c, gi: (l, c, 0, 0)),
                  pl.BlockSpec((None, None, HEAD_DIM, HEAD_DIM), lambda b, c, gi: (l, c, 0, 0))],
        out_specs=pl.BlockSpec((None, None, None, n_half, HEAD_DIM),
                               lambda b, c, gi: (b, c, gi, 0, 0)),
        out_shape=jax.ShapeDtypeStruct((bsz, 2, g, n_half, HEAD_DIM), F32),
        compiler_params=_cparams("parallel", "parallel", "parallel"),
        name="nsa_compress",
    )(proj, cmp_pos, cmp_w1, cmp_w2)


def _rope_tile(x, cos, sin_signed):
    return x * cos + pltpu.roll(x, HEAD_DIM // 2, 1) * sin_signed


def _nsa_prep_kernel(q_ref, ks_ref, vs_ref, kw_ref, vw_ref, cos_ref, sin_ref,
                     qb_ref, qr_ref, kso_ref, vst_ref, kwo_ref, vwt_ref):
    cos = cos_ref[...]
    sin = sin_ref[...]
    for h in range(NSA_HEADS):
        sl = slice(h * HEAD_DIM, (h + 1) * HEAD_DIM)
        x = q_ref[:, sl]
        qb_ref[:, sl] = x.astype(BF16)
        qr_ref[:, sl] = (_rope_tile(x, cos, sin) * (SCALE * LOG2E)).astype(BF16)
    n_tiles, _, width = vst_ref.shape
    for g in range(NSA_KV_GROUPS):
        sl = slice(g * HEAD_DIM, (g + 1) * HEAD_DIM)
        kso_ref[:, sl] = _rope_tile(ks_ref[:, sl], cos, sin).astype(BF16)
        kwo_ref[:, sl] = _rope_tile(kw_ref[:, sl], cos, sin).astype(BF16)
        vs_t = jnp.transpose(vs_ref[:, sl]).astype(BF16)
        vw_t = jnp.transpose(vw_ref[:, sl]).astype(BF16)
        for c in range(n_tiles):
            vst_ref[c, sl, :] = vs_t[:, c * width:(c + 1) * width]
            vwt_ref[c, sl, :] = vw_t[:, c * width:(c + 1) * width]


def _nsa_prep(proj, cos, sin_signed, bsz, seq, q_off, kv_off):
    tr = NSA_TQ
    nsa_w = NSA_HEADS * HEAD_DIM
    gw = NSA_KV_GROUPS * HEAD_DIM
    nr = seq // tr
    t = bsz * seq
    qblk = q_off // nsa_w
    kvb = kv_off // gw
    n_tiles = tr // NSA_TK
    rowq = lambda w, cb: pl.BlockSpec((tr, w), lambda b, i: (b * nr + i, cb))
    tab = pl.BlockSpec((tr, HEAD_DIM), lambda b, i: (i, 0))
    vt_spec = pl.BlockSpec((None, n_tiles, gw, NSA_TK), lambda b, i: (b, i, 0, 0))
    vt_shape = jax.ShapeDtypeStruct((bsz, seq // NSA_TK, gw, NSA_TK), BF16)
    return pl.pallas_call(
        _nsa_prep_kernel,
        grid=(bsz, nr),
        in_specs=[rowq(nsa_w, qblk), rowq(gw, kvb + 2), rowq(gw, kvb + 3),
                  rowq(gw, kvb + 4), rowq(gw, kvb + 5), tab, tab],
        out_specs=[rowq(nsa_w, 0), rowq(nsa_w, 0), rowq(gw, 0), vt_spec, rowq(gw, 0), vt_spec],
        out_shape=[jax.ShapeDtypeStruct((t, nsa_w), BF16), jax.ShapeDtypeStruct((t, nsa_w), BF16),
                   jax.ShapeDtypeStruct((t, gw), BF16), vt_shape,
                   jax.ShapeDtypeStruct((t, gw), BF16), vt_shape],
        compiler_params=_cparams("parallel", "parallel"),
        name="nsa_prep",
    )(proj, proj, proj, proj, proj, cos, sin_signed)


def _col_reduce(x, op, ways=4):
    nblk = x.shape[0] // 8
    accs = [x[8 * k:8 * (k + 1)] for k in range(min(ways, nblk))]
    for k in range(len(accs), nblk):
        accs[k % ways] = op(accs[k % ways], x[8 * k:8 * (k + 1)])
    while len(accs) > 1:
        accs = [op(accs[a], accs[a + 1]) for a in range(0, len(accs), 2)]
    return accs[0]


def _col_max(x):
    return jnp.max(_col_reduce(x, jnp.maximum), axis=0, keepdims=True)


def _col_sum(x):
    return jnp.sum(_col_reduce(x, jnp.add), axis=0, keepdims=True)


def _nsa_kernel(qb_ref, qr_ref, kc_ref, vc_ref, ks_ref, vst_ref, kw_ref, vwt_ref,
                ng_ref, nz_ref, ovl_ref, exp_ref, o_ref,
                qa_ref, ts0_ref, ts1_ref, tw0_ref, tw1_ref, xs0_ref, xs1_ref, xw0_ref, xw1_ref,
                ms_ref, mw_ref, accs_ref, accw_ref, ocmp_ref, ngt_ref,
                *, tq, tk, n_slc):
    assert tq == 2 * tk and WINDOW == tq
    gi = pl.program_id(1)
    i = pl.program_id(2)
    q0 = i * tq
    nt = (((1,), (1,)), ((), ()))
    pos_row = q0 + lax.broadcasted_iota(jnp.int32, (1, tq), 1)
    pos_all = jnp.concatenate([pos_row] * NSA_REP, axis=1)
    heads = [slice(r * HEAD_DIM, (r + 1) * HEAD_DIM) for r in range(NSA_REP)]
    cols = [slice(r * tq, (r + 1) * tq) for r in range(NSA_REP)]

    qb = jnp.concatenate([qb_ref[:, h] for h in heads], axis=0)
    kc = kc_ref[...].astype(BF16)
    vc_t = jnp.transpose(vc_ref[...]).astype(BF16)
    n_cmp = kc.shape[0]
    cmp_end = CMP_STRIDE * lax.broadcasted_iota(jnp.int32, (n_cmp, 1), 0) + (CMP_BLOCK - 1)
    cmask = cmp_end <= pos_all
    s = lax.dot_general(kc, qb, nt, preferred_element_type=F32) * SCALE
    s = jnp.where(cmask, s, NEG)
    e = jnp.exp(s - _col_max(s))
    p = e * (1.0 / _col_sum(e))
    p = jnp.where(cmask, p, 0.0).astype(BF16)
    ocmp_ref[...] = jnp.dot(vc_t, p, preferred_element_type=F32)
    imp_all = jnp.dot(ovl_ref[...], p, preferred_element_type=F32)
    imp = imp_all[:, cols[0]]
    for c in cols[1:]:
        imp = imp + imp_all[:, c]

    val = imp[0:n_slc, :]
    blk = lax.broadcasted_iota(jnp.int32, (n_slc, 1), 0)
    cur = pos_row // SLC_BLOCK
    future = blk * SLC_BLOCK > pos_row
    forced = (blk == 0) | (blk == cur) | (blk == cur - 1)
    val = jnp.where(future, -jnp.inf, jnp.where(forced, jnp.inf, val))
    rank = jnp.zeros((n_slc, tq), jnp.int32)
    for j in range(n_slc):
        vj = val[j:j + 1, :]
        beats = (vj > val) | ((vj == val) & (blk > j))
        rank = rank + beats.astype(jnp.int32)
    sel_bias = jnp.where(rank < min(SLC_TOP_N, n_slc), 0.0, NEG)
    sel_bias = jnp.concatenate([sel_bias, jnp.zeros((LANES - n_slc, tq), F32)], axis=0)
    sel_bias = jnp.transpose(sel_bias).astype(BF16)
    qa_ref[:, 0:HEAD_DIM] = jnp.concatenate([qr_ref[:, h] for h in heads], axis=0)
    qa_ref[:, HEAD_DIM:] = jnp.concatenate([sel_bias] * NSA_REP, axis=0)

    ones_rows = (lax.broadcasted_iota(jnp.int32, (VT_PAD, tk), 0) == 0).astype(BF16)

    def scores(t_ref, x_ref, keys, queries, visible):
        t = lax.dot_general(keys, queries, nt, preferred_element_type=F32)
        if visible is not None:
            t = jnp.where(visible, t, NEG)
        t_ref[...] = t
        x_ref[...] = _col_max(t)

    def absorb(t_ref, x_ref, vt, m_ref, acc_ref):
        m_old = m_ref[...]
        m_new = jnp.maximum(m_old, x_ref[...])
        alpha = jnp.exp2(m_old - m_new)
        e = jnp.exp2(t_ref[...] - m_new).astype(BF16)
        vt_ones = jnp.concatenate([vt, ones_rows], axis=0)
        acc_ref[...] = alpha * acc_ref[...] + jnp.dot(vt_ones, e, preferred_element_type=F32)
        m_ref[...] = m_new

    def key_pos(c):
        return c * tk + lax.broadcasted_iota(jnp.int32, (tk, 1), 0)

    def slc_scores(t_ref, x_ref, c, causal):
        start = pl.multiple_of(c * tk, tk)
        keys = jnp.concatenate([ks_ref[pl.ds(start, tk), :], exp_ref[c]], axis=1)
        scores(t_ref, x_ref, keys, qa_ref[...], (key_pos(c) <= pos_all) if causal else None)

    def slc_absorb(t_ref, x_ref, c):
        absorb(t_ref, x_ref, vst_ref[c], ms_ref, accs_ref)

    slc_scores(ts0_ref, xs0_ref, 2 * i, True)
    slc_scores(ts1_ref, xs1_ref, 2 * i + 1, True)
    ms_ref[...] = jnp.maximum(xs0_ref[...], xs1_ref[...])
    accs_ref[...] = jnp.zeros(accs_ref.shape, F32)
    slc_absorb(ts0_ref, xs0_ref, 2 * i)

    def slc_pair(j, carry):
        slc_scores(ts0_ref, xs0_ref, 2 * j, False)
        slc_absorb(ts1_ref, xs1_ref, jnp.where(j == 0, 2 * i + 1, 2 * j - 1))
        slc_scores(ts1_ref, xs1_ref, 2 * j + 1, False)
        slc_absorb(ts0_ref, xs0_ref, 2 * j)
        return carry

    lax.fori_loop(0, i, slc_pair, 0)
    slc_absorb(ts1_ref, xs1_ref, jnp.where(i == 0, 1, 2 * i - 1))

    low = jnp.maximum(pos_all - WINDOW, -1)

    def win_scores(t_ref, x_ref, c, causal):
        kpos = key_pos(c)
        start = pl.multiple_of(jnp.maximum(c, 0) * tk, tk)
        scores(t_ref, x_ref, kw_ref[pl.ds(start, tk), :], qa_ref[:, 0:HEAD_DIM],
               (kpos <= pos_all) if causal else (kpos > low))

    def win_absorb(t_ref, x_ref, c):
        absorb(t_ref, x_ref, vwt_ref[jnp.maximum(c, 0)], mw_ref, accw_ref)

    mw_ref[...] = jnp.full(mw_ref.shape, NEG, F32)
    accw_ref[...] = jnp.zeros(accw_ref.shape, F32)
    win_scores(tw0_ref, xw0_ref, 2 * i, True)
    win_scores(tw1_ref, xw1_ref, 2 * i + 1, True)
    win_absorb(tw0_ref, xw0_ref, 2 * i)
    win_scores(tw0_ref, xw0_ref, 2 * i - 1, False)
    win_absorb(tw1_ref, xw1_ref, 2 * i + 1)
    win_scores(tw1_ref, xw1_ref, 2 * i - 2, False)
    win_absorb(tw0_ref, xw0_ref, 2 * i - 1)
    win_absorb(tw1_ref, xw1_ref, 2 * i - 2)

    ngt_ref[...] = jnp.transpose(ng_ref[...])
    inv_ls = 1.0 / accs_ref[HEAD_DIM:HEAD_DIM + 1, :]
    inv_lw = 1.0 / accw_ref[HEAD_DIM:HEAD_DIM + 1, :]
    for r in range(NSA_REP):
        row = (gi * NSA_REP + r) * NSA_BRANCHES
        g_cmp = _sigmoid(ngt_ref[pl.ds(row, 1), :])
        g_slc = _sigmoid(ngt_ref[pl.ds(row + 1, 1), :])
        g_win = _sigmoid(ngt_ref[pl.ds(row + 2, 1), :])
        o_t = (g_cmp * ocmp_ref[:, cols[r]]
               + (g_slc * inv_ls[:, cols[r]]) * accs_ref[0:HEAD_DIM, cols[r]]
               + (g_win * inv_lw[:, cols[r]]) * accw_ref[0:HEAD_DIM, cols[r]])
        o_ref[:, heads[r]] = (jnp.transpose(o_t) * _silu(nz_ref[:, heads[r]])).astype(o_ref.dtype)


def _nsa_attention(qb, qr, cmpkv, ks, vst, kw, vwt, ng, proj, ovl, expand, bsz, seq, nz_blk):
    tq, tk = NSA_TQ, NSA_TK
    g = NSA_KV_GROUPS
    gw = NSA_REP * HEAD_DIM
    nq = seq // tq
    n_slc = seq // SLC_BLOCK
    n_half = cmpkv.shape[-2]
    stacked = NSA_REP * tq
    rowq = lambda cb: pl.BlockSpec((tq, gw), lambda b, gi, i: (b * nq + i, cb + gi))
    kvs = pl.BlockSpec((seq, HEAD_DIM), lambda b, gi, i: (b, gi))
    cmps = lambda c: pl.BlockSpec((None, None, None, n_half, HEAD_DIM),
                                  lambda b, gi, i: (b, c, gi, 0, 0))
    vts = pl.BlockSpec((None, seq // tk, HEAD_DIM, tk), lambda b, gi, i: (b, 0, gi, 0))
    score_buf = pltpu.VMEM((tk, stacked), F32)
    stat_row = pltpu.VMEM((1, stacked), F32)
    flash_acc = pltpu.VMEM((HEAD_DIM + VT_PAD, stacked), F32)
    return pl.pallas_call(
        functools.partial(_nsa_kernel, tq=tq, tk=tk, n_slc=n_slc),
        grid=(bsz, g, nq),
        in_specs=[rowq(0), rowq(0), cmps(0), cmps(1), kvs, vts, kvs, vts,
                  pl.BlockSpec((tq, LANES), lambda b, gi, i: (b * nq + i, 0)),
                  rowq(nz_blk),
                  pl.BlockSpec((LANES, n_half), lambda b, gi, i: (0, 0)),
                  pl.BlockSpec((seq // tk, tk, LANES), lambda b, gi, i: (0, 0, 0))],
        out_specs=rowq(0),
        out_shape=jax.ShapeDtypeStruct((bsz * seq, NSA_HEADS * HEAD_DIM), BF16),
        scratch_shapes=[pltpu.VMEM((stacked, 2 * HEAD_DIM), BF16),
                        score_buf, score_buf, score_buf, score_buf,
                        stat_row, stat_row, stat_row, stat_row,
                        stat_row, stat_row,
                        flash_acc, flash_acc, pltpu.VMEM((HEAD_DIM, stacked), F32),
                        pltpu.VMEM((LANES, tq), F32)],
        compiler_params=_cparams("parallel", "parallel", "arbitrary"),
        name="nsa_attention",
    )(qb, qr, cmpkv, cmpkv, ks, vst, kw, vwt, ng, proj, ovl, expand)


def _merge_kernel(ya_ref, yb_ref, yx_ref, wa_ref, wb_ref, wx_ref, g0_ref, g1_ref, g2_ref, o_ref):
    def branch(y_ref, w_ref, g_ref):
        up = jnp.dot(y_ref[...], w_ref[...].astype(BF16), preferred_element_type=F32)
        return _sigmoid(g_ref[...]) * up

    u = branch(ya_ref, wa_ref, g0_ref) + branch(yb_ref, wb_ref, g1_ref) + branch(yx_ref, wx_ref, g2_ref)
    o_ref[...] = u.astype(o_ref.dtype)


def _merge(ya, yb, yx, wa, wb, wx, proj, l, mg_off, d_model, tm=2048, tn=256):
    t = ya.shape[0]
    nj = d_model // tn
    mgb = mg_off // tn
    act = lambda w: pl.BlockSpec((tm, w), lambda i, j: (i, 0), pipeline_mode=pl.Buffered(1))
    wsp = lambda w: pl.BlockSpec((None, w, tn), lambda i, j: (l, 0, j))
    gsp = lambda br: pl.BlockSpec((tm, tn), lambda i, j: (i, mgb + br * nj + j))
    return pl.pallas_call(
        _merge_kernel,
        grid=(t // tm, nj),
        in_specs=[act(ya.shape[1]), act(yb.shape[1]), act(yx.shape[1]),
                  wsp(wa.shape[1]), wsp(wb.shape[1]), wsp(wx.shape[1]),
                  gsp(0), gsp(1), gsp(2)],
        out_specs=pl.BlockSpec((tm, tn), lambda i, j: (i, j)),
        out_shape=jax.ShapeDtypeStruct((t, d_model), BF16),
        compiler_params=_cparams("parallel", "arbitrary"),
        name="gated_merge",
    )(ya, yb, yx, wa, wb, wx, proj, proj, proj)


def _rope_tables(seq):
    half = HEAD_DIM // 2
    inv_freq = ROPE_THETA ** (-jnp.arange(half, dtype=F32) / half)
    ang = jnp.arange(seq, dtype=jnp.int32).astype(F32)[:, None] * inv_freq[None, :]
    cos, sin = jnp.cos(ang), jnp.sin(ang)
    return jnp.concatenate([cos, cos], axis=-1), jnp.concatenate([-sin, sin], axis=-1)


def _overlap_table(seq, n_half):
    n_slc = seq // SLC_BLOCK
    cmp_start = np.arange(n_half) * CMP_STRIDE
    slc_start = np.arange(n_slc) * SLC_BLOCK
    ovl = ((cmp_start[None, :] < slc_start[:, None] + SLC_BLOCK)
           & (cmp_start[None, :] + CMP_BLOCK > slc_start[:, None]))
    ovl = ovl & (cmp_start[None, :] + CMP_BLOCK <= seq)
    out = np.zeros((LANES, n_half), np.float32)
    out[:n_slc, :] = ovl
    return jnp.asarray(out, BF16)


def _expand_table(seq, tk):
    key_blk = np.arange(seq) // SLC_BLOCK
    e = (key_blk[:, None] == np.arange(LANES)[None, :]).astype(np.float32)
    return jnp.asarray(e.reshape(seq // tk, tk, LANES), BF16)


def kernel(x, mem, norm_g, w_in, conv_w, conv_b, cmp_pos, cmp_w1, cmp_w2, mem_norm_g, w_mem_kv,
           w_up_a, w_up_b, w_up_x, w_out, final_g):
    bsz, seq, d_model = x.shape
    mem_len = mem.shape[1]
    depth = w_in.shape[0]
    sc_w = conv_w.shape[-1]
    nsa_w = NSA_HEADS * HEAD_DIM
    kv_w = NSA_BRANCHES * 2 * NSA_KV_GROUPS * HEAD_DIM
    ng_w = NSA_BRANCHES * NSA_HEADS
    x_w = X_HEADS * HEAD_DIM
    proj_tn = 512

    q_off = 4 * sc_w
    kv_off = q_off + nsa_w
    lo_w = kv_off + kv_w
    nz_off = 0
    xq_off = nz_off + nsa_w
    xz_off = xq_off + x_w
    mg_off = xz_off + x_w
    hi_w = mg_off + N_BRANCHES * d_model
    assert seq % NSA_TQ == 0 and seq // SLC_BLOCK <= LANES and sc_w % 256 == 0
    assert lo_w % proj_tn == 0 and hi_w % proj_tn == 0 and 0 < ng_w < LANES
    assert (lo_w + ng_w) % 16 == 0 and lo_w + ng_w + hi_w == w_in.shape[-1]

    w_in_t = jnp.swapaxes(w_in, 1, 2)
    w_ng = jnp.pad(w_in_t[:, lo_w:lo_w + ng_w, :], ((0, 0), (0, LANES - ng_w), (0, 0)))
    w_ng = jnp.swapaxes(w_ng, 1, 2).astype(BF16)
    w_memkv_b = w_mem_kv.astype(BF16)
    cmp_w1_b = cmp_w1.astype(BF16)
    cmp_w2_b = cmp_w2.astype(BF16)
    norm_g3 = norm_g.reshape(depth, 1, d_model)
    mem_norm_g3 = mem_norm_g.reshape(depth, 1, d_model)
    conv_b3 = conv_b.reshape(depth, 1, sc_w)

    cos, sin_signed = _rope_tables(seq)
    n_half = seq // CMP_STRIDE
    ovl = _overlap_table(seq, n_half)
    expand = _expand_table(seq, NSA_TK)

    t = bsz * seq
    xf = x.reshape(t, d_model)
    memf = mem.reshape(bsz * mem_len, d_model)

    for l in range(depth):
        h = _rmsnorm(xf, norm_g3, l, BF16)
        proj_lo = _matmul_nt(h, w_in_t, l, 0, lo_w, F32, 2048, proj_tn, "in_proj_lo")
        proj_hi = _matmul_nt(h, w_in_t, l, lo_w + ng_w, hi_w, F32, 2048, proj_tn, "in_proj_hi")
        ng = _matmul(h, w_ng, l, F32, 1024, LANES, name="gate_proj")

        y_a = _conv_branch(proj_lo, conv_w, conv_b3, l, bsz, seq, sc_w)

        hm = _rmsnorm(memf, mem_norm_g3, l, BF16)
        memkv = _matmul(hm, w_memkv_b, l, F32, 512, 1024, name="mem_kv_proj")
        y_x = _mem_attention(proj_hi, memkv, bsz, seq, mem_len, xq_off // HEAD_DIM, xz_off // HEAD_DIM)

        cmpkv = _compress(proj_lo, cmp_pos, cmp_w1_b, cmp_w2_b, l, bsz, seq, kv_off // HEAD_DIM)
        qb, qr, ks, vst, kw, vwt = _nsa_prep(proj_lo, cos, sin_signed, bsz, seq, q_off, kv_off)
        y_b = _nsa_attention(qb, qr, cmpkv, ks, vst, kw, vwt, ng, proj_hi, ovl, expand,
                             bsz, seq, nz_off // (NSA_REP * HEAD_DIM))

        u = _merge(y_a, y_b, y_x, w_up_a, w_up_b, w_up_x, proj_hi, l, mg_off, d_model)
        xf = _matmul_res(u, w_out, l, xf, F32, 2048, 512, "out_proj")

    out = _rmsnorm(xf, final_g.reshape(1, 1, d_model), 0, F32)
    return out.reshape(bsz, seq, d_model)
```

```python
import functools

import numpy as np
import jax
import jax.numpy as jnp
from jax import lax
from jax.experimental import pallas as pl
from jax.experimental.pallas import tpu as pltpu

F32 = jnp.float32
BF16 = jnp.bfloat16

HEAD_DIM = 128
ROPE_THETA = 10000.0
EPS = 1e-6
NEG = -1e30
SC_KERNEL = 3
NSA_HEADS = 16
NSA_KV_GROUPS = 4
NSA_REP = NSA_HEADS // NSA_KV_GROUPS
NSA_BRANCHES = 3
CMP_BLOCK = 32
CMP_STRIDE = 16
SLC_BLOCK = 64
SLC_TOP_N = 16
WINDOW = 512
X_HEADS = 4
N_BRANCHES = 3
SCALE = HEAD_DIM ** -0.5
LOG2E = 1.4426950408889634

V7X_VMEM_BYTES = 64 * 1024 * 1024
VMEM_LIMIT_BYTES = V7X_VMEM_BYTES - 8 * 1024 * 1024
LANES = 128
MXU_WIDTH = 256

NSA_TQ = 512
NSA_TK = 256
VT_PAD = 16


def _cparams(*sem):
    return pltpu.CompilerParams(dimension_semantics=sem, vmem_limit_bytes=VMEM_LIMIT_BYTES)


def _resident_rows_spec(tm, kdim):
    big = tm * kdim * 2 >= V7X_VMEM_BYTES // 4
    return pl.BlockSpec((tm, kdim), lambda i, j: (i, 0),
                        pipeline_mode=pl.Buffered(1) if big else None)


def _sigmoid(x):
    return 0.5 * jnp.tanh(0.5 * x) + 0.5


def _silu(x):
    return x * _sigmoid(x)


def _rmsnorm_kernel(x_ref, g_ref, o_ref):
    x = x_ref[...]
    ms = jnp.mean(x * x, axis=-1, keepdims=True)
    o_ref[...] = (x * lax.rsqrt(ms + EPS) * g_ref[...]).astype(o_ref.dtype)


def _rmsnorm(x, g3, l, out_dtype, tm=256):
    m, d = x.shape
    return pl.pallas_call(
        _rmsnorm_kernel,
        grid=(m // tm,),
        in_specs=[pl.BlockSpec((tm, d), lambda i: (i, 0)),
                  pl.BlockSpec((None, 1, d), lambda i: (l, 0, 0))],
        out_specs=pl.BlockSpec((tm, d), lambda i: (i, 0)),
        out_shape=jax.ShapeDtypeStruct((m, d), out_dtype),
        compiler_params=_cparams("parallel"),
        name="rmsnorm",
    )(x, g3)


def _prenorm_kernel(x_ref, g_ref, a_ref, r_ref):
    x = x_ref[...]
    r_ref[...] = lax.rsqrt(jnp.mean(x * x, axis=-1, keepdims=True) + EPS)
    a_ref[...] = (x * g_ref[...]).astype(a_ref.dtype)


def _prenorm(x, g3, l, tm=256):
    m, d = x.shape
    return pl.pallas_call(
        _prenorm_kernel,
        grid=(m // tm,),
        in_specs=[pl.BlockSpec((tm, d), lambda i: (i, 0)),
                  pl.BlockSpec((None, 1, d), lambda i: (l, 0, 0))],
        out_specs=[pl.BlockSpec((tm, d), lambda i: (i, 0)), pl.BlockSpec((tm, 1), lambda i: (i, 0))],
        out_shape=[jax.ShapeDtypeStruct((m, d), BF16), jax.ShapeDtypeStruct((m, 1), F32)],
        compiler_params=_cparams("parallel"),
        name="prenorm",
    )(x, g3)


def _mm_kernel(a_ref, b_ref, o_ref):
    o_ref[...] = jnp.dot(a_ref[...], b_ref[...], preferred_element_type=F32).astype(o_ref.dtype)


def _mm_rowscale_kernel(a_ref, b_ref, r_ref, o_ref):
    acc = jnp.dot(a_ref[...], b_ref[...], preferred_element_type=F32)
    o_ref[...] = (acc * r_ref[...]).astype(o_ref.dtype)


def _matmul(a, b, l, out_dtype, tm, tn, name, row_scale=None):
    m, kdim = a.shape
    n = b.shape[-1]
    tm, tn = min(tm, m), min(tn, n)
    in_specs = [pl.BlockSpec((tm, kdim), lambda i, j: (i, 0)),
                pl.BlockSpec((None, kdim, tn), lambda i, j: (l, 0, j))]
    args = [a, b]
    if row_scale is not None:
        in_specs.append(pl.BlockSpec((tm, 1), lambda i, j: (i, 0)))
        args.append(row_scale)
    return pl.pallas_call(
        _mm_kernel if row_scale is None else _mm_rowscale_kernel,
        grid=(m // tm, n // tn),
        in_specs=in_specs,
        out_specs=pl.BlockSpec((tm, tn), lambda i, j: (i, j)),
        out_shape=jax.ShapeDtypeStruct((m, n), out_dtype),
        compiler_params=_cparams("parallel", "arbitrary"),
        name=name,
    )(*args)


def _in_proj_kernel(a_ref, w_ref, r_ref, o_ref):
    acc = lax.dot_general(a_ref[...], w_ref[0].astype(BF16), (((1,), (1,)), ((), ())),
                          preferred_element_type=F32)
    o_ref[...] = (acc * r_ref[...]).astype(o_ref.dtype)


def _in_proj(a, row_scale, wt, l, lo, gap, n, tm, tn):
    m, kdim = a.shape
    lo_tiles = lo // tn

    def w_index(i, j):
        return l, pl.multiple_of(j * tn + jnp.where(j >= lo_tiles, gap, 0), 16), 0

    return pl.pallas_call(
        _in_proj_kernel,
        grid=(m // tm, n // tn),
        in_specs=[_resident_rows_spec(tm, kdim),
                  pl.BlockSpec((pl.Element(1), pl.Element(tn), pl.Element(kdim)), w_index),
                  pl.BlockSpec((tm, 1), lambda i, j: (i, 0))],
        out_specs=pl.BlockSpec((tm, tn), lambda i, j: (i, j)),
        out_shape=jax.ShapeDtypeStruct((m, n), F32),
        compiler_params=_cparams("parallel", "arbitrary"),
        name="in_proj",
    )(a, wt, row_scale)


def _out_proj_kernel(a_ref, w_ref, res_ref, o_ref):
    acc = jnp.dot(a_ref[...], w_ref[...].astype(BF16), preferred_element_type=F32)
    o_ref[...] = res_ref[...] + acc


def _out_proj_next_kernel(a_ref, w_ref, res_ref, g_ref, o_ref, an_ref, rn_ref, *, d_model):
    j = pl.program_id(1)
    ss = 0.0
    for c in range(0, o_ref.shape[1], MXU_WIDTH):
        sl = slice(c, c + MXU_WIDTH)
        acc = jnp.dot(a_ref[...], w_ref[:, sl].astype(BF16), preferred_element_type=F32)
        xn = res_ref[:, sl] + acc
        o_ref[:, sl] = xn
        an_ref[:, sl] = (xn * g_ref[:, sl]).astype(an_ref.dtype)
        ss = ss + jnp.sum(xn * xn, axis=-1, keepdims=True)

    @pl.when(j == 0)
    def _():
        rn_ref[...] = ss

    @pl.when(j > 0)
    def _():
        rn_ref[...] = rn_ref[...] + ss

    @pl.when(j == pl.num_programs(1) - 1)
    def _():
        rn_ref[...] = lax.rsqrt(rn_ref[...] * (1.0 / d_model) + EPS)


def _out_proj(a, w, l, residual, tm, tn, next_gain=None):
    m, kdim = a.shape
    n = w.shape[-1]
    tile = pl.BlockSpec((tm, tn), lambda i, j: (i, j))
    in_specs = [_resident_rows_spec(tm, kdim),
                pl.BlockSpec((None, kdim, tn), lambda i, j: (l, 0, j)), tile]
    if next_gain is None:
        return pl.pallas_call(
            _out_proj_kernel,
            grid=(m // tm, n // tn),
            in_specs=in_specs,
            out_specs=tile,
            out_shape=jax.ShapeDtypeStruct((m, n), F32),
            compiler_params=_cparams("parallel", "arbitrary"),
            name="out_proj",
        )(a, w, residual)
    return pl.pallas_call(
        functools.partial(_out_proj_next_kernel, d_model=n),
        grid=(m // tm, n // tn),
        in_specs=in_specs + [pl.BlockSpec((None, 1, tn), lambda i, j: (l + 1, 0, j))],
        out_specs=[tile, tile, pl.BlockSpec((tm, 1), lambda i, j: (i, 0))],
        out_shape=[jax.ShapeDtypeStruct((m, n), F32), jax.ShapeDtypeStruct((m, n), BF16),
                   jax.ShapeDtypeStruct((m, 1), F32)],
        compiler_params=_cparams("parallel", "arbitrary"),
        name="out_proj_next",
    )(a, w, residual, next_gain)


def _conv_kernel(h_ref, b_ref, c_ref, z_ref, w_ref, cb_ref, o_ref):
    u = c_ref[...] * h_ref[...]
    row = lax.broadcasted_iota(jnp.int32, u.shape, 0)
    u1 = jnp.where(row >= 1, pltpu.roll(u, 1, 0), 0.0)
    u2 = jnp.where(row >= 2, pltpu.roll(u, 2, 0), 0.0)
    w = w_ref[...]
    y = cb_ref[...] + w[0:1, :] * u2
    y = y + w[1:2, :] * u1
    y = y + w[2:3, :] * u
    o_ref[...] = (b_ref[...] * y * _silu(z_ref[...])).astype(o_ref.dtype)


def _conv_branch(proj, conv_w, conv_b3, l, bsz, seq, width, tc=256):
    nb = width // tc
    blk = lambda off: pl.BlockSpec((seq, tc), lambda b, j: (b, off * nb + j))
    return pl.pallas_call(
        _conv_kernel,
        grid=(bsz, nb),
        in_specs=[blk(0), blk(1), blk(2), blk(3),
                  pl.BlockSpec((None, SC_KERNEL, tc), lambda b, j: (l, 0, j)),
                  pl.BlockSpec((None, 1, tc), lambda b, j: (l, 0, j))],
        out_specs=pl.BlockSpec((seq, tc), lambda b, j: (b, j)),
        out_shape=jax.ShapeDtypeStruct((bsz * seq, width), BF16),
        compiler_params=_cparams("parallel", "parallel"),
        name="conv_branch",
    )(proj, proj, proj, proj, conv_w, conv_b3)


def _memattn_kernel(q_ref, k_ref, v_ref, z_ref, o_ref):
    q = q_ref[...].astype(BF16)
    k = k_ref[...].astype(BF16)
    v = v_ref[...].astype(BF16)
    s = lax.dot_general(q, k, (((1,), (1,)), ((), ())), preferred_element_type=F32) * SCALE
    m = jnp.max(s, axis=-1, keepdims=True)
    e = jnp.exp(s - m)
    p = e / jnp.sum(e, axis=-1, keepdims=True)
    o = jnp.dot(p.astype(BF16), v, preferred_element_type=F32)
    o_ref[...] = (o * _silu(z_ref[...])).astype(o_ref.dtype)


def _mem_attention(proj, memkv, bsz, seq, mem_len, q_blk, z_blk, tq=512):
    nq = seq // tq
    return pl.pallas_call(
        _memattn_kernel,
        grid=(bsz, X_HEADS, nq),
        in_specs=[pl.BlockSpec((tq, HEAD_DIM), lambda b, h, i: (b * nq + i, q_blk + h)),
                  pl.BlockSpec((mem_len, HEAD_DIM), lambda b, h, i: (b, h)),
                  pl.BlockSpec((mem_len, HEAD_DIM), lambda b, h, i: (b, X_HEADS + h)),
                  pl.BlockSpec((tq, HEAD_DIM), lambda b, h, i: (b * nq + i, z_blk + h))],
        out_specs=pl.BlockSpec((tq, HEAD_DIM), lambda b, h, i: (b * nq + i, h)),
        out_shape=jax.ShapeDtypeStruct((bsz * seq, X_HEADS * HEAD_DIM), BF16),
        compiler_params=_cparams("parallel", "parallel", "parallel"),
        name="mem_attention",
    )(proj, memkv, memkv, proj)


def _compress_kernel(kv_ref, pos_ref, w1_ref, w2_ref, o_ref, *, n_half):
    half = CMP_BLOCK // 2
    acc_a = jnp.zeros((n_half, HEAD_DIM), F32)
    acc_b = jnp.zeros((n_half, HEAD_DIM), F32)
    for t in range(half):
        x = kv_ref[pl.ds(t, n_half, stride=half), :]
        xa = (x + pos_ref[t:t + 1, :]).astype(BF16)
        xb = (x + pos_ref[half + t:half + t + 1, :]).astype(BF16)
        acc_a = acc_a + jnp.dot(xa, w1_ref[t * HEAD_DIM:(t + 1) * HEAD_DIM, :],
                                preferred_element_type=F32)
        acc_b = acc_b + jnp.dot(xb, w1_ref[(half + t) * HEAD_DIM:(half + t + 1) * HEAD_DIM, :],
                                preferred_element_type=F32)
    hid = _silu(acc_a + pltpu.roll(acc_b, n_half - 1, 0))
    o_ref[...] = jnp.dot(hid.astype(BF16), w2_ref[...], preferred_element_type=F32)


def _compress(proj, cmp_pos, cmp_w1, cmp_w2, l, bsz, seq, kv_blk):
    g = NSA_KV_GROUPS
    n_half = seq // (CMP_BLOCK // 2)
    return pl.pallas_call(
        functools.partial(_compress_kernel, n_half=n_half),
        grid=(bsz, 2, g),
        in_specs=[pl.BlockSpec((seq, HEAD_DIM), lambda b, c, gi: (b, kv_blk + c * g + gi)),
                  pl.BlockSpec((None, None, CMP_BLOCK, HEAD_DIM), lambda b, c, gi: (l, c, 0, 0)),
                  pl.BlockSpec((None, None, CMP_BLOCK * HEAD_DIM, HEAD_DIM),
                               lambda b, c, gi: (l, c, 0, 0)),
                  pl.BlockSpec((None, None, HEAD_DIM, HEAD_DIM), lambda b, c, gi: (l, c, 0, 0))],
        out_specs=pl.BlockSpec((None, None, None, n_half, HEAD_DIM),
                               lambda b, c, gi: (b, c, gi, 0, 0)),
        out_shape=jax.ShapeDtypeStruct((bsz, 2, g, n_half, HEAD_DIM), F32),
        compiler_params=_cparams("parallel", "parallel", "parallel"),
        name="nsa_compress",
    )(proj, cmp_pos, cmp_w1, cmp_w2)


def _rope_tile(x, cos, sin_signed):
    return x * cos + pltpu.roll(x, HEAD_DIM // 2, 1) * sin_signed


def _nsa_prep_kernel(q_ref, ks_ref, vs_ref, kw_ref, vw_ref, cos_ref, sin_ref,
                     qb_ref, qr_ref, kso_ref, vst_ref, kwo_ref, vwt_ref):
    cos = cos_ref[...]
    sin = sin_ref[...]
    for h in range(NSA_HEADS):
        sl = slice(h * HEAD_DIM, (h + 1) * HEAD_DIM)
        x = q_ref[:, sl]
        qb_ref[:, sl] = x.astype(BF16)
        qr_ref[:, sl] = (_rope_tile(x, cos, sin) * (SCALE * LOG2E)).astype(BF16)
    n_tiles, _, width = vst_ref.shape
    for g in range(NSA_KV_GROUPS):
        sl = slice(g * HEAD_DIM, (g + 1) * HEAD_DIM)
        kso_ref[:, sl] = _rope_tile(ks_ref[:, sl], cos, sin).astype(BF16)
        kwo_ref[:, sl] = _rope_tile(kw_ref[:, sl], cos, sin).astype(BF16)
        vs_t = jnp.transpose(vs_ref[:, sl]).astype(BF16)
        vw_t = jnp.transpose(vw_ref[:, sl]).astype(BF16)
        for c in range(n_tiles):
            vst_ref[c, sl, :] = vs_t[:, c * width:(c + 1) * width]
            vwt_ref[c, sl, :] = vw_t[:, c * width:(c + 1) * width]


def _nsa_prep(proj, cos, sin_signed, bsz, seq, q_off, kv_off):
    tr = NSA_TQ
    nsa_w = NSA_HEADS * HEAD_DIM
    gw = NSA_KV_GROUPS * HEAD_DIM
    nr = seq // tr
    t = bsz * seq
    qblk = q_off // nsa_w
    kvb = kv_off // gw
    n_tiles = tr // NSA_TK
    rowq = lambda w, cb: pl.BlockSpec((tr, w), lambda b, i: (b * nr + i, cb))
    tab = pl.BlockSpec((tr, HEAD_DIM), lambda b, i: (i, 0))
    vt_spec = pl.BlockSpec((None, n_tiles, gw, NSA_TK), lambda b, i: (b, i, 0, 0))
    vt_shape = jax.ShapeDtypeStruct((bsz, seq // NSA_TK, gw, NSA_TK), BF16)
    return pl.pallas_call(
        _nsa_prep_kernel,
        grid=(bsz, nr),
        in_specs=[rowq(nsa_w, qblk), rowq(gw, kvb + 2), rowq(gw, kvb + 3),
                  rowq(gw, kvb + 4), rowq(gw, kvb + 5), tab, tab],
        out_specs=[rowq(nsa_w, 0), rowq(nsa_w, 0), rowq(gw, 0), vt_spec, rowq(gw, 0), vt_spec],
        out_shape=[jax.ShapeDtypeStruct((t, nsa_w), BF16), jax.ShapeDtypeStruct((t, nsa_w), BF16),
                   jax.ShapeDtypeStruct((t, gw), BF16), vt_shape,
                   jax.ShapeDtypeStruct((t, gw), BF16), vt_shape],
        compiler_params=_cparams("parallel", "parallel"),
        name="nsa_prep",
    )(proj, proj, proj, proj, proj, cos, sin_signed)


def _col_reduce(x, op, ways=4):
    nblk = x.shape[0] // 8
    accs = [x[8 * k:8 * (k + 1)] for k in range(min(ways, nblk))]
    for k in range(len(accs), nblk):
        accs[k % ways] = op(accs[k % ways], x[8 * k:8 * (k + 1)])
    while len(accs) > 1:
        accs = [op(accs[a], accs[a + 1]) for a in range(0, len(accs), 2)]
    return accs[0]


def _col_max(x):
    return jnp.max(_col_reduce(x, jnp.maximum), axis=0, keepdims=True)


def _col_sum(x):
    return jnp.sum(_col_reduce(x, jnp.add), axis=0, keepdims=True)


def _nsa_kernel(qb_ref, qr_ref, kc_ref, vc_ref, ks_ref, vst_ref, kw_ref, vwt_ref,
                ng_ref, nz_ref, ovl_ref, exp_ref, o_ref,
                qa_ref, ts0_ref, ts1_ref, tw0_ref, tw1_ref, xs0_ref, xs1_ref, xw0_ref, xw1_ref,
                ms_ref, mw_ref, accs_ref, accw_ref, ocmp_ref, ngt_ref,
                *, tq, tk, n_slc):
    assert tq == 2 * tk and WINDOW == tq
    gi = pl.program_id(1)
    i = pl.program_id(2)
    q0 = i * tq
    nt = (((1,), (1,)), ((), ()))
    pos_row = q0 + lax.broadcasted_iota(jnp.int32, (1, tq), 1)
    pos_all = jnp.concatenate([pos_row] * NSA_REP, axis=1)
    heads = [slice(r * HEAD_DIM, (r + 1) * HEAD_DIM) for r in range(NSA_REP)]
    cols = [slice(r * tq, (r + 1) * tq) for r in range(NSA_REP)]

    qb = jnp.concatenate([qb_ref[:, h] for h in heads], axis=0)
    kc = kc_ref[...].astype(BF16)
    vc_t = jnp.transpose(vc_ref[...]).astype(BF16)
    n_cmp = kc.shape[0]
    cmp_end = CMP_STRIDE * lax.broadcasted_iota(jnp.int32, (n_cmp, 1), 0) + (CMP_BLOCK - 1)
    cmask = cmp_end <= pos_all
    s = lax.dot_general(kc, qb, nt, preferred_element_type=F32) * SCALE
    s = jnp.where(cmask, s, NEG)
    e = jnp.exp(s - _col_max(s))
    p = e * (1.0 / _col_sum(e))
    p = jnp.where(cmask, p, 0.0).astype(BF16)
    ocmp_ref[...] = jnp.dot(vc_t, p, preferred_element_type=F32)
    imp_all = jnp.dot(ovl_ref[...], p, preferred_element_type=F32)
    imp = imp_all[:, cols[0]]
    for c in cols[1:]:
        imp = imp + imp_all[:, c]

    val = imp[0:n_slc, :]
    blk = lax.broadcasted_iota(jnp.int32, (n_slc, 1), 0)
    cur = pos_row // SLC_BLOCK
    future = blk * SLC_BLOCK > pos_row
    forced = (blk == 0) | (blk == cur) | (blk == cur - 1)
    val = jnp.where(future, -jnp.inf, jnp.where(forced, jnp.inf, val))
    rank = jnp.zeros((n_slc, tq), jnp.int32)
    for j in range(n_slc):
        vj = val[j:j + 1, :]
        beats = (vj > val) | ((vj == val) & (blk > j))
        rank = rank + beats.astype(jnp.int32)
    sel_bias = jnp.where(rank < min(SLC_TOP_N, n_slc), 0.0, NEG)
    sel_bias = jnp.concatenate([sel_bias, jnp.zeros((LANES - n_slc, tq), F32)], axis=0)
    sel_bias = jnp.transpose(sel_bias).astype(BF16)
    qa_ref[:, 0:HEAD_DIM] = jnp.concatenate([qr_ref[:, h] for h in heads], axis=0)
    qa_ref[:, HEAD_DIM:] = jnp.concatenate([sel_bias] * NSA_REP, axis=0)

    ones_rows = (lax.broadcasted_iota(jnp.int32, (VT_PAD, tk), 0) == 0).astype(BF16)

    def scores(t_ref, x_ref, keys, queries, visible):
        t = lax.dot_general(keys, queries, nt, preferred_element_type=F32)
        if visible is not None:
            t = jnp.where(visible, t, NEG)
        t_ref[...] = t
        x_ref[...] = _col_max(t)

    def absorb(t_ref, x_ref, vt, m_ref, acc_ref):
        m_old = m_ref[...]
        m_new = jnp.maximum(m_old, x_ref[...])
        alpha = jnp.exp2(m_old - m_new)
        e = jnp.exp2(t_ref[...] - m_new).astype(BF16)
        vt_ones = jnp.concatenate([vt, ones_rows], axis=0)
        acc_ref[...] = alpha * acc_ref[...] + jnp.dot(vt_ones, e, preferred_element_type=F32)
        m_ref[...] = m_new

    def key_pos(c):
        return c * tk + lax.broadcasted_iota(jnp.int32, (tk, 1), 0)

    def slc_scores(t_ref, x_ref, c, causal):
        start = pl.multiple_of(c * tk, tk)
        keys = jnp.concatenate([ks_ref[pl.ds(start, tk), :], exp_ref[c]], axis=1)
        scores(t_ref, x_ref, keys, qa_ref[...], (key_pos(c) <= pos_all) if causal else None)

    def slc_absorb(t_ref, x_ref, c):
        absorb(t_ref, x_ref, vst_ref[c], ms_ref, accs_ref)

    slc_scores(ts0_ref, xs0_ref, 2 * i, True)
    slc_scores(ts1_ref, xs1_ref, 2 * i + 1, True)
    ms_ref[...] = jnp.maximum(xs0_ref[...], xs1_ref[...])
    accs_ref[...] = jnp.zeros(accs_ref.shape, F32)
    slc_absorb(ts0_ref, xs0_ref, 2 * i)

    def slc_pair(j, carry):
        slc_scores(ts0_ref, xs0_ref, 2 * j, False)
        slc_absorb(ts1_ref, xs1_ref, jnp.where(j == 0, 2 * i + 1, 2 * j - 1))
        slc_scores(ts1_ref, xs1_ref, 2 * j + 1, False)
        slc_absorb(ts0_ref, xs0_ref, 2 * j)
        return carry

    lax.fori_loop(0, i, slc_pair, 0)
    slc_absorb(ts1_ref, xs1_ref, jnp.where(i == 0, 1, 2 * i - 1))

    low = jnp.maximum(pos_all - WINDOW, -1)

    def win_scores(t_ref, x_ref, c, causal):
        kpos = key_pos(c)
        start = pl.multiple_of(jnp.maximum(c, 0) * tk, tk)
        scores(t_ref, x_ref, kw_ref[pl.ds(start, tk), :], qa_ref[:, 0:HEAD_DIM],
               (kpos <= pos_all) if causal else (kpos > low))

    def win_absorb(t_ref, x_ref, c):
        absorb(t_ref, x_ref, vwt_ref[jnp.maximum(c, 0)], mw_ref, accw_ref)

    mw_ref[...] = jnp.full(mw_ref.shape, NEG, F32)
    accw_ref[...] = jnp.zeros(accw_ref.shape, F32)
    win_scores(tw0_ref, xw0_ref, 2 * i, True)
    win_scores(tw1_ref, xw1_ref, 2 * i + 1, True)
    win_absorb(tw0_ref, xw0_ref, 2 * i)
    win_scores(tw0_ref, xw0_ref, 2 * i - 1, False)
    win_absorb(tw1_ref, xw1_ref, 2 * i + 1)
    win_scores(tw1_ref, xw1_ref, 2 * i - 2, False)
    win_absorb(tw0_ref, xw0_ref, 2 * i - 1)
    win_absorb(tw1_ref, xw1_ref, 2 * i - 2)

    ngt_ref[...] = jnp.transpose(ng_ref[...])
    inv_ls = 1.0 / accs_ref[HEAD_DIM:HEAD_DIM + 1, :]
    inv_lw = 1.0 / accw_ref[HEAD_DIM:HEAD_DIM + 1, :]
    for r in range(NSA_REP):
        row = (gi * NSA_REP + r) * NSA_BRANCHES
        g_cmp = _sigmoid(ngt_ref[pl.ds(row, 1), :])
        g_slc = _sigmoid(ngt_ref[pl.ds(row + 1, 1), :])
        g_win = _sigmoid(ngt_ref[pl.ds(row + 2, 1), :])
        o_t = (g_cmp * ocmp_ref[:, cols[r]]
               + (g_slc * inv_ls[:, cols[r]]) * accs_ref[0:HEAD_DIM, cols[r]]
               + (g_win * inv_lw[:, cols[r]]) * accw_ref[0:HEAD_DIM, cols[r]])
        o_ref[:, heads[r]] = (jnp.transpose(o_t) * _silu(nz_ref[:, heads[r]])).astype(o_ref.dtype)


def _nsa_attention(qb, qr, cmpkv, ks, vst, kw, vwt, ng, proj, ovl, expand, bsz, seq, nz_blk):
    tq, tk = NSA_TQ, NSA_TK
    g = NSA_KV_GROUPS
    gw = NSA_REP * HEAD_DIM
    nq = seq // tq
    n_slc = seq // SLC_BLOCK
    n_half = cmpkv.shape[-2]
    stacked = NSA_REP * tq
    rowq = lambda cb: pl.BlockSpec((tq, gw), lambda b, gi, i: (b * nq + i, cb + gi))
    kvs = pl.BlockSpec((seq, HEAD_DIM), lambda b, gi, i: (b, gi))
    cmps = lambda c: pl.BlockSpec((None, None, None, n_half, HEAD_DIM),
                                  lambda b, gi, i: (b, c, gi, 0, 0))
    vts = pl.BlockSpec((None, seq // tk, HEAD_DIM, tk), lambda b, gi, i: (b, 0, gi, 0))
    score_buf = pltpu.VMEM((tk, stacked), F32)
    stat_row = pltpu.VMEM((1, stacked), F32)
    flash_acc = pltpu.VMEM((HEAD_DIM + VT_PAD, stacked), F32)
    return pl.pallas_call(
        functools.partial(_nsa_kernel, tq=tq, tk=tk, n_slc=n_slc),
        grid=(bsz, g, nq),
        in_specs=[rowq(0), rowq(0), cmps(0), cmps(1), kvs, vts, kvs, vts,
                  pl.BlockSpec((tq, LANES), lambda b, gi, i: (b * nq + i, 0)),
                  rowq(nz_blk),
                  pl.BlockSpec((LANES, n_half), lambda b, gi, i: (0, 0)),
                  pl.BlockSpec((seq // tk, tk, LANES), lambda b, gi, i: (0, 0, 0))],
        out_specs=rowq(0),
        out_shape=jax.ShapeDtypeStruct((bsz * seq, NSA_HEADS * HEAD_DIM), BF16),
        scratch_shapes=[pltpu.VMEM((stacked, 2 * HEAD_DIM), BF16),
                        score_buf, score_buf, score_buf, score_buf,
                        stat_row, stat_row, stat_row, stat_row,
                        stat_row, stat_row,
                        flash_acc, flash_acc, pltpu.VMEM((HEAD_DIM, stacked), F32),
                        pltpu.VMEM((LANES, tq), F32)],
        compiler_params=_cparams("parallel", "parallel", "arbitrary"),
        name="nsa_attention",
    )(qb, qr, cmpkv, cmpkv, ks, vst, kw, vwt, ng, proj, ovl, expand)


def _merge_kernel(ya_ref, yb_ref, yx_ref, wa_ref, wb_ref, wx_ref, g0_ref, g1_ref, g2_ref, o_ref):
    def branch(y_ref, w_ref, g_ref):
        up = jnp.dot(y_ref[...], w_ref[...].astype(BF16), preferred_element_type=F32)
        return _sigmoid(g_ref[...]) * up

    u = branch(ya_ref, wa_ref, g0_ref) + branch(yb_ref, wb_ref, g1_ref) + branch(yx_ref, wx_ref, g2_ref)
    o_ref[...] = u.astype(o_ref.dtype)


def _merge(ya, yb, yx, wa, wb, wx, proj, l, mg_off, d_model, tm=2048, tn=256):
    t = ya.shape[0]
    nj = d_model // tn
    mgb = mg_off // tn
    act = lambda w: pl.BlockSpec((tm, w), lambda i, j: (i, 0), pipeline_mode=pl.Buffered(1))
    wsp = lambda w: pl.BlockSpec((None, w, tn), lambda i, j: (l, 0, j))
    gsp = lambda br: pl.BlockSpec((tm, tn), lambda i, j: (i, mgb + br * nj + j))
    return pl.pallas_call(
        _merge_kernel,
        grid=(t // tm, nj),
        in_specs=[act(ya.shape[1]), act(yb.shape[1]), act(yx.shape[1]),
                  wsp(wa.shape[1]), wsp(wb.shape[1]), wsp(wx.shape[1]),
                  gsp(0), gsp(1), gsp(2)],
        out_specs=pl.BlockSpec((tm, tn), lambda i, j: (i, j)),
        out_shape=jax.ShapeDtypeStruct((t, d_model), BF16),
        compiler_params=_cparams("parallel", "arbitrary"),
        name="gated_merge",
    )(ya, yb, yx, wa, wb, wx, proj, proj, proj)


def _rope_tables(seq):
    half = HEAD_DIM // 2
    inv_freq = ROPE_THETA ** (-jnp.arange(half, dtype=F32) / half)
    ang = jnp.arange(seq, dtype=jnp.int32).astype(F32)[:, None] * inv_freq[None, :]
    cos, sin = jnp.cos(ang), jnp.sin(ang)
    return jnp.concatenate([cos, cos], axis=-1), jnp.concatenate([-sin, sin], axis=-1)


def _overlap_table(seq, n_half):
    n_slc = seq // SLC_BLOCK
    cmp_start = np.arange(n_half) * CMP_STRIDE
    slc_start = np.arange(n_slc) * SLC_BLOCK
    ovl = ((cmp_start[None, :] < slc_start[:, None] + SLC_BLOCK)
           & (cmp_start[None, :] + CMP_BLOCK > slc_start[:, None]))
    ovl = ovl & (cmp_start[None, :] + CMP_BLOCK <= seq)
    out = np.zeros((LANES, n_half), np.float32)
    out[:n_slc, :] = ovl
    return jnp.asarray(out, BF16)


def _expand_table(seq, tk):
    key_blk = np.arange(seq) // SLC_BLOCK
    e = (key_blk[:, None] == np.arange(LANES)[None, :]).astype(np.float32)
    return jnp.asarray(e.reshape(seq // tk, tk, LANES), BF16)


def kernel(x, mem, norm_g, w_in, conv_w, conv_b, cmp_pos, cmp_w1, cmp_w2, mem_norm_g, w_mem_kv,
           w_up_a, w_up_b, w_up_x, w_out, final_g):
    bsz, seq, d_model = x.shape
    mem_len = mem.shape[1]
    depth = w_in.shape[0]
    sc_w = conv_w.shape[-1]
    nsa_w = NSA_HEADS * HEAD_DIM
    kv_w = NSA_BRANCHES * 2 * NSA_KV_GROUPS * HEAD_DIM
    ng_w = NSA_BRANCHES * NSA_HEADS
    x_w = X_HEADS * HEAD_DIM
    proj_tn = 512

    q_off = 4 * sc_w
    kv_off = q_off + nsa_w
    lo_w = kv_off + kv_w
    nz_off = lo_w
    xq_off = nz_off + nsa_w
    xz_off = xq_off + x_w
    mg_off = xz_off + x_w
    proj_w = mg_off + N_BRANCHES * d_model
    assert seq % NSA_TQ == 0 and seq // SLC_BLOCK <= LANES and sc_w % 256 == 0
    assert lo_w % proj_tn == 0 and proj_w % proj_tn == 0 and 0 < ng_w < LANES
    assert ng_w % 16 == 0 and proj_w + ng_w == w_in.shape[-1]

    w_in_t = jnp.swapaxes(w_in, 1, 2)
    w_ng = jnp.pad(w_in_t[:, lo_w:lo_w + ng_w, :], ((0, 0), (0, LANES - ng_w), (0, 0)))
    w_ng = jnp.swapaxes(w_ng, 1, 2).astype(BF16)
    w_memkv_b = w_mem_kv.astype(BF16)
    cmp_w1_b = cmp_w1.astype(BF16)
    cmp_w2_b = cmp_w2.astype(BF16)
    norm_g3 = norm_g.reshape(depth, 1, d_model)
    mem_norm_g3 = mem_norm_g.reshape(depth, 1, d_model)
    conv_b3 = conv_b.reshape(depth, 1, sc_w)

    cos, sin_signed = _rope_tables(seq)
    n_half = seq // CMP_STRIDE
    ovl = _overlap_table(seq, n_half)
    expand = _expand_table(seq, NSA_TK)

    t = bsz * seq
    xf = x.reshape(t, d_model)
    memf = mem.reshape(bsz * mem_len, d_model)

    hg, hr = _prenorm(xf, norm_g3, 0)
    for l in range(depth):
        proj = _in_proj(hg, hr, w_in_t, l, lo_w, ng_w, proj_w, 2048, proj_tn)
        ng = _matmul(hg, w_ng, l, F32, 1024, LANES, name="gate_proj", row_scale=hr)

        y_a = _conv_branch(proj, conv_w, conv_b3, l, bsz, seq, sc_w)

        hm = _rmsnorm(memf, mem_norm_g3, l, BF16)
        memkv = _matmul(hm, w_memkv_b, l, F32, 512, 1024, name="mem_kv_proj")
        y_x = _mem_attention(proj, memkv, bsz, seq, mem_len, xq_off // HEAD_DIM, xz_off // HEAD_DIM)

        cmpkv = _compress(proj, cmp_pos, cmp_w1_b, cmp_w2_b, l, bsz, seq, kv_off // HEAD_DIM)
        qb, qr, ks, vst, kw, vwt = _nsa_prep(proj, cos, sin_signed, bsz, seq, q_off, kv_off)
        y_b = _nsa_attention(qb, qr, cmpkv, ks, vst, kw, vwt, ng, proj, ovl, expand,
                             bsz, seq, nz_off // (NSA_REP * HEAD_DIM))

        u = _merge(y_a, y_b, y_x, w_up_a, w_up_b, w_up_x, proj, l, mg_off, d_model)
        if l + 1 < depth:
            xf, hg, hr = _out_proj(u, w_out, l, xf, 1024, 512, next_gain=norm_g3)
        else:
            xf = _out_proj(u, w_out, l, xf, 2048, 512)

    out = _rmsnorm(xf, final_g.reshape(1, 1, d_model), 0, F32)
    return out.reshape(bsz, seq, d_model)
```

```python
import functools

import numpy as np
import jax
import jax.numpy as jnp
from jax import lax
from jax.experimental import pallas as pl
from jax.experimental.pallas import tpu as pltpu

F32 = jnp.float32
BF16 = jnp.bfloat16

HEAD_DIM = 128
ROPE_THETA = 10000.0
EPS = 1e-6
NEG = -1e30
SC_KERNEL = 3
NSA_HEADS = 16
NSA_KV_GROUPS = 4
NSA_REP = NSA_HEADS // NSA_KV_GROUPS
NSA_BRANCHES = 3
CMP_BLOCK = 32
CMP_STRIDE = 16
SLC_BLOCK = 64
SLC_TOP_N = 16
WINDOW = 512
X_HEADS = 4
N_BRANCHES = 3
SCALE = HEAD_DIM ** -0.5
LOG2E = 1.4426950408889634

V7X_VMEM_BYTES = 64 * 1024 * 1024
VMEM_LIMIT_BYTES = V7X_VMEM_BYTES - 8 * 1024 * 1024
LANES = 128
MXU_WIDTH = 256

NSA_TQ = 512
NSA_TK = 256
VT_PAD = 16


def _cparams(*sem):
    return pltpu.CompilerParams(dimension_semantics=sem, vmem_limit_bytes=VMEM_LIMIT_BYTES)


def _resident_rows_spec(tm, kdim):
    big = tm * kdim * 2 >= V7X_VMEM_BYTES // 4
    return pl.BlockSpec((tm, kdim), lambda i, j: (i, 0),
                        pipeline_mode=pl.Buffered(1) if big else None)


def _sigmoid(x):
    return 0.5 * jnp.tanh(0.5 * x) + 0.5


def _silu(x):
    return x * _sigmoid(x)


def _rmsnorm_kernel(x_ref, g_ref, o_ref):
    x = x_ref[...]
    ms = jnp.mean(x * x, axis=-1, keepdims=True)
    o_ref[...] = (x * lax.rsqrt(ms + EPS) * g_ref[...]).astype(o_ref.dtype)


def _rmsnorm(x, g3, l, out_dtype, tm=256):
    m, d = x.shape
    return pl.pallas_call(
        _rmsnorm_kernel,
        grid=(m // tm,),
        in_specs=[pl.BlockSpec((tm, d), lambda i: (i, 0)),
                  pl.BlockSpec((None, 1, d), lambda i: (l, 0, 0))],
        out_specs=pl.BlockSpec((tm, d), lambda i: (i, 0)),
        out_shape=jax.ShapeDtypeStruct((m, d), out_dtype),
        compiler_params=_cparams("parallel"),
        name="rmsnorm",
    )(x, g3)


def _prenorm_kernel(x_ref, g_ref, a_ref, r_ref):
    x = x_ref[...]
    r_ref[...] = lax.rsqrt(jnp.mean(x * x, axis=-1, keepdims=True) + EPS)
    a_ref[...] = (x * g_ref[...]).astype(a_ref.dtype)


def _prenorm(x, g3, l, tm=256):
    m, d = x.shape
    return pl.pallas_call(
        _prenorm_kernel,
        grid=(m // tm,),
        in_specs=[pl.BlockSpec((tm, d), lambda i: (i, 0)),
                  pl.BlockSpec((None, 1, d), lambda i: (l, 0, 0))],
        out_specs=[pl.BlockSpec((tm, d), lambda i: (i, 0)), pl.BlockSpec((tm, 1), lambda i: (i, 0))],
        out_shape=[jax.ShapeDtypeStruct((m, d), BF16), jax.ShapeDtypeStruct((m, 1), F32)],
        compiler_params=_cparams("parallel"),
        name="prenorm",
    )(x, g3)


def _mm_kernel(a_ref, b_ref, o_ref):
    o_ref[...] = jnp.dot(a_ref[...], b_ref[...], preferred_element_type=F32).astype(o_ref.dtype)


def _matmul(a, b, l, out_dtype, tm, tn, name):
    m, kdim = a.shape
    n = b.shape[-1]
    tm, tn = min(tm, m), min(tn, n)
    return pl.pallas_call(
        _mm_kernel,
        grid=(m // tm, n // tn),
        in_specs=[pl.BlockSpec((tm, kdim), lambda i, j: (i, 0)),
                  pl.BlockSpec((None, kdim, tn), lambda i, j: (l, 0, j))],
        out_specs=pl.BlockSpec((tm, tn), lambda i, j: (i, j)),
        out_shape=jax.ShapeDtypeStruct((m, n), out_dtype),
        compiler_params=_cparams("parallel", "arbitrary"),
        name=name,
    )(a, b)


def _in_proj_kernel(a_ref, w_ref, r_ref, o_ref):
    acc = lax.dot_general(a_ref[...], w_ref[0].astype(BF16), (((1,), (1,)), ((), ())),
                          preferred_element_type=F32)
    o_ref[...] = (acc * r_ref[...]).astype(o_ref.dtype)


def _in_proj(a, row_scale, wt, l, lo, gap, tm, tn):
    m, kdim = a.shape
    n_rows = wt.shape[1]
    lo_tiles = lo // tn
    n = n_rows - gap + tn
    assert lo % tn == 0 and n % tn == 0 and gap % 16 == 0 and gap < tn

    def w_index(i, j):
        return l, pl.multiple_of(jnp.where(j <= lo_tiles, j * tn, (j - 1) * tn + gap), 16), 0

    return pl.pallas_call(
        _in_proj_kernel,
        grid=(m // tm, n // tn),
        in_specs=[_resident_rows_spec(tm, kdim),
                  pl.BlockSpec((pl.Element(1), pl.Element(tn), pl.Element(kdim)), w_index),
                  pl.BlockSpec((tm, 1), lambda i, j: (i, 0))],
        out_specs=pl.BlockSpec((tm, tn), lambda i, j: (i, j)),
        out_shape=jax.ShapeDtypeStruct((m, n), F32),
        compiler_params=_cparams("parallel", "arbitrary"),
        name="in_proj",
    )(a, wt, row_scale)


def _out_proj_kernel(a_ref, w_ref, res_ref, o_ref):
    acc = jnp.dot(a_ref[...], w_ref[...].astype(BF16), preferred_element_type=F32)
    o_ref[...] = res_ref[...] + acc


def _out_proj_next_kernel(a_ref, w_ref, res_ref, g_ref, o_ref, an_ref, rn_ref, *, d_model):
    j = pl.program_id(1)
    ss = 0.0
    for c in range(0, o_ref.shape[1], MXU_WIDTH):
        sl = slice(c, c + MXU_WIDTH)
        acc = jnp.dot(a_ref[...], w_ref[:, sl].astype(BF16), preferred_element_type=F32)
        xn = res_ref[:, sl] + acc
        o_ref[:, sl] = xn
        an_ref[:, sl] = (xn * g_ref[:, sl]).astype(an_ref.dtype)
        ss = ss + jnp.sum(xn * xn, axis=-1, keepdims=True)

    @pl.when(j == 0)
    def _():
        rn_ref[...] = ss

    @pl.when(j > 0)
    def _():
        rn_ref[...] = rn_ref[...] + ss

    @pl.when(j == pl.num_programs(1) - 1)
    def _():
        rn_ref[...] = lax.rsqrt(rn_ref[...] * (1.0 / d_model) + EPS)


def _out_proj(a, w, l, residual, tm, tn, next_gain=None):
    m, kdim = a.shape
    n = w.shape[-1]
    tile = pl.BlockSpec((tm, tn), lambda i, j: (i, j))
    in_specs = [_resident_rows_spec(tm, kdim),
                pl.BlockSpec((None, kdim, tn), lambda i, j: (l, 0, j)), tile]
    if next_gain is None:
        return pl.pallas_call(
            _out_proj_kernel,
            grid=(m // tm, n // tn),
            in_specs=in_specs,
            out_specs=tile,
            out_shape=jax.ShapeDtypeStruct((m, n), F32),
            compiler_params=_cparams("parallel", "arbitrary"),
            name="out_proj",
        )(a, w, residual)
    return pl.pallas_call(
        functools.partial(_out_proj_next_kernel, d_model=n),
        grid=(m // tm, n // tn),
        in_specs=in_specs + [pl.BlockSpec((None, 1, tn), lambda i, j: (l + 1, 0, j))],
        out_specs=[tile, tile, pl.BlockSpec((tm, 1), lambda i, j: (i, 0))],
        out_shape=[jax.ShapeDtypeStruct((m, n), F32), jax.ShapeDtypeStruct((m, n), BF16),
                   jax.ShapeDtypeStruct((m, 1), F32)],
        compiler_params=_cparams("parallel", "arbitrary"),
        name="out_proj_next",
    )(a, w, residual, next_gain)


def _conv_kernel(h_ref, b_ref, c_ref, z_ref, w_ref, cb_ref, o_ref):
    u = c_ref[...] * h_ref[...]
    row = lax.broadcasted_iota(jnp.int32, u.shape, 0)
    u1 = jnp.where(row >= 1, pltpu.roll(u, 1, 0), 0.0)
    u2 = jnp.where(row >= 2, pltpu.roll(u, 2, 0), 0.0)
    w = w_ref[...]
    y = cb_ref[...] + w[0:1, :] * u2
    y = y + w[1:2, :] * u1
    y = y + w[2:3, :] * u
    o_ref[...] = (b_ref[...] * y * _silu(z_ref[...])).astype(o_ref.dtype)


def _conv_branch(proj, conv_w, conv_b3, l, bsz, seq, width, tc=256):
    nb = width // tc
    blk = lambda off: pl.BlockSpec((seq, tc), lambda b, j: (b, off * nb + j))
    return pl.pallas_call(
        _conv_kernel,
        grid=(bsz, nb),
        in_specs=[blk(0), blk(1), blk(2), blk(3),
                  pl.BlockSpec((None, SC_KERNEL, tc), lambda b, j: (l, 0, j)),
                  pl.BlockSpec((None, 1, tc), lambda b, j: (l, 0, j))],
        out_specs=pl.BlockSpec((seq, tc), lambda b, j: (b, j)),
        out_shape=jax.ShapeDtypeStruct((bsz * seq, width), BF16),
        compiler_params=_cparams("parallel", "parallel"),
        name="conv_branch",
    )(proj, proj, proj, proj, conv_w, conv_b3)


def _memattn_kernel(q_ref, k_ref, v_ref, z_ref, o_ref):
    q = q_ref[...].astype(BF16)
    k = k_ref[...].astype(BF16)
    v = v_ref[...].astype(BF16)
    s = lax.dot_general(q, k, (((1,), (1,)), ((), ())), preferred_element_type=F32) * SCALE
    m = jnp.max(s, axis=-1, keepdims=True)
    e = jnp.exp(s - m)
    p = e / jnp.sum(e, axis=-1, keepdims=True)
    o = jnp.dot(p.astype(BF16), v, preferred_element_type=F32)
    o_ref[...] = (o * _silu(z_ref[...])).astype(o_ref.dtype)


def _mem_attention(proj, memkv, bsz, seq, mem_len, q_blk, z_blk, tq=512):
    nq = seq // tq
    return pl.pallas_call(
        _memattn_kernel,
        grid=(bsz, X_HEADS, nq),
        in_specs=[pl.BlockSpec((tq, HEAD_DIM), lambda b, h, i: (b * nq + i, q_blk + h)),
                  pl.BlockSpec((mem_len, HEAD_DIM), lambda b, h, i: (b, h)),
                  pl.BlockSpec((mem_len, HEAD_DIM), lambda b, h, i: (b, X_HEADS + h)),
                  pl.BlockSpec((tq, HEAD_DIM), lambda b, h, i: (b * nq + i, z_blk + h))],
        out_specs=pl.BlockSpec((tq, HEAD_DIM), lambda b, h, i: (b * nq + i, h)),
        out_shape=jax.ShapeDtypeStruct((bsz * seq, X_HEADS * HEAD_DIM), BF16),
        compiler_params=_cparams("parallel", "parallel", "parallel"),
        name="mem_attention",
    )(proj, memkv, memkv, proj)


def _compress_kernel(kv_ref, pos_ref, w1_ref, w2_ref, o_ref, *, n_half):
    half = CMP_BLOCK // 2
    acc_a = jnp.zeros((n_half, HEAD_DIM), F32)
    acc_b = jnp.zeros((n_half, HEAD_DIM), F32)
    for t in range(half):
        x = kv_ref[pl.ds(t, n_half, stride=half), :]
        xa = (x + pos_ref[t:t + 1, :]).astype(BF16)
        xb = (x + pos_ref[half + t:half + t + 1, :]).astype(BF16)
        acc_a = acc_a + jnp.dot(xa, w1_ref[t * HEAD_DIM:(t + 1) * HEAD_DIM, :],
                                preferred_element_type=F32)
        acc_b = acc_b + jnp.dot(xb, w1_ref[(half + t) * HEAD_DIM:(half + t + 1) * HEAD_DIM, :],
                                preferred_element_type=F32)
    hid = _silu(acc_a + pltpu.roll(acc_b, n_half - 1, 0))
    o_ref[...] = jnp.dot(hid.astype(BF16), w2_ref[...], preferred_element_type=F32)


def _compress(proj, cmp_pos, cmp_w1, cmp_w2, l, bsz, seq, kv_blk):
    g = NSA_KV_GROUPS
    n_half = seq // (CMP_BLOCK // 2)
    return pl.pallas_call(
        functools.partial(_compress_kernel, n_half=n_half),
        grid=(bsz, 2, g),
        in_specs=[pl.BlockSpec((seq, HEAD_DIM), lambda b, c, gi: (b, kv_blk + c * g + gi)),
                  pl.BlockSpec((None, None, CMP_BLOCK, HEAD_DIM), lambda b, c, gi: (l, c, 0, 0)),
                  pl.BlockSpec((None, None, CMP_BLOCK * HEAD_DIM, HEAD_DIM),
                               lambda b, c, gi: (l, c, 0, 0)),
                  pl.BlockSpec((None, None, HEAD_DIM, HEAD_DIM), lambda b, c, gi: (l, c, 0, 0))],
        out_specs=pl.BlockSpec((None, None, None, n_half, HEAD_DIM),
                               lambda b, c, gi: (b, c, gi, 0, 0)),
        out_shape=jax.ShapeDtypeStruct((bsz, 2, g, n_half, HEAD_DIM), F32),
        compiler_params=_cparams("parallel", "parallel", "parallel"),
        name="nsa_compress",
    )(proj, cmp_pos, cmp_w1, cmp_w2)


def _rope_tile(x, cos, sin_signed):
    return x * cos + pltpu.roll(x, HEAD_DIM // 2, 1) * sin_signed


def _nsa_prep_kernel(q_ref, ks_ref, vs_ref, kw_ref, vw_ref, cos_ref, sin_ref,
                     qb_ref, qr_ref, kso_ref, vst_ref, kwo_ref, vwt_ref):
    cos = cos_ref[...]
    sin = sin_ref[...]
    for h in range(NSA_HEADS):
        sl = slice(h * HEAD_DIM, (h + 1) * HEAD_DIM)
        x = q_ref[:, sl]
        qb_ref[:, sl] = x.astype(BF16)
        qr_ref[:, sl] = (_rope_tile(x, cos, sin) * (SCALE * LOG2E)).astype(BF16)
    n_tiles, _, width = vst_ref.shape
    for g in range(NSA_KV_GROUPS):
        sl = slice(g * HEAD_DIM, (g + 1) * HEAD_DIM)
        kso_ref[:, sl] = _rope_tile(ks_ref[:, sl], cos, sin).astype(BF16)
        kwo_ref[:, sl] = _rope_tile(kw_ref[:, sl], cos, sin).astype(BF16)
        vs_t = jnp.transpose(vs_ref[:, sl]).astype(BF16)
        vw_t = jnp.transpose(vw_ref[:, sl]).astype(BF16)
        for c in range(n_tiles):
            vst_ref[c, sl, :] = vs_t[:, c * width:(c + 1) * width]
            vwt_ref[c, sl, :] = vw_t[:, c * width:(c + 1) * width]


def _nsa_prep(proj, cos, sin_signed, bsz, seq, q_off, kv_off):
    tr = NSA_TQ
    nsa_w = NSA_HEADS * HEAD_DIM
    gw = NSA_KV_GROUPS * HEAD_DIM
    nr = seq // tr
    t = bsz * seq
    qblk = q_off // nsa_w
    kvb = kv_off // gw
    n_tiles = tr // NSA_TK
    rowq = lambda w, cb: pl.BlockSpec((tr, w), lambda b, i: (b * nr + i, cb))
    tab = pl.BlockSpec((tr, HEAD_DIM), lambda b, i: (i, 0))
    vt_spec = pl.BlockSpec((None, n_tiles, gw, NSA_TK), lambda b, i: (b, i, 0, 0))
    vt_shape = jax.ShapeDtypeStruct((bsz, seq // NSA_TK, gw, NSA_TK), BF16)
    return pl.pallas_call(
        _nsa_prep_kernel,
        grid=(bsz, nr),
        in_specs=[rowq(nsa_w, qblk), rowq(gw, kvb + 2), rowq(gw, kvb + 3),
                  rowq(gw, kvb + 4), rowq(gw, kvb + 5), tab, tab],
        out_specs=[rowq(nsa_w, 0), rowq(nsa_w, 0), rowq(gw, 0), vt_spec, rowq(gw, 0), vt_spec],
        out_shape=[jax.ShapeDtypeStruct((t, nsa_w), BF16), jax.ShapeDtypeStruct((t, nsa_w), BF16),
                   jax.ShapeDtypeStruct((t, gw), BF16), vt_shape,
                   jax.ShapeDtypeStruct((t, gw), BF16), vt_shape],
        compiler_params=_cparams("parallel", "parallel"),
        name="nsa_prep",
    )(proj, proj, proj, proj, proj, cos, sin_signed)


def _col_reduce(x, op, ways=4):
    nblk = x.shape[0] // 8
    accs = [x[8 * k:8 * (k + 1)] for k in range(min(ways, nblk))]
    for k in range(len(accs), nblk):
        accs[k % ways] = op(accs[k % ways], x[8 * k:8 * (k + 1)])
    while len(accs) > 1:
        accs = [op(accs[a], accs[a + 1]) for a in range(0, len(accs), 2)]
    return accs[0]


def _col_max(x):
    return jnp.max(_col_reduce(x, jnp.maximum), axis=0, keepdims=True)


def _col_sum(x):
    return jnp.sum(_col_reduce(x, jnp.add), axis=0, keepdims=True)


def _nsa_kernel(qb_ref, qr_ref, kc_ref, vc_ref, ks_ref, vst_ref, kw_ref, vwt_ref,
                ng_ref, nz_ref, ovl_ref, exp_ref, o_ref,
                qa_ref, ts0_ref, ts1_ref, tw0_ref, tw1_ref, xs0_ref, xs1_ref, xw0_ref, xw1_ref,
                ms_ref, mw_ref, accs_ref, accw_ref, ocmp_ref, ngt_ref,
                *, tq, tk, n_slc):
    assert tq == 2 * tk and WINDOW == tq
    gi = pl.program_id(1)
    i = pl.program_id(2)
    q0 = i * tq
    nt = (((1,), (1,)), ((), ()))
    pos_row = q0 + lax.broadcasted_iota(jnp.int32, (1, tq), 1)
    pos_all = jnp.concatenate([pos_row] * NSA_REP, axis=1)
    heads = [slice(r * HEAD_DIM, (r + 1) * HEAD_DIM) for r in range(NSA_REP)]
    cols = [slice(r * tq, (r + 1) * tq) for r in range(NSA_REP)]

    qb = jnp.concatenate([qb_ref[:, h] for h in heads], axis=0)
    kc = kc_ref[...].astype(BF16)
    vc_t = jnp.transpose(vc_ref[...]).astype(BF16)
    n_cmp = kc.shape[0]
    cmp_end = CMP_STRIDE * lax.broadcasted_iota(jnp.int32, (n_cmp, 1), 0) + (CMP_BLOCK - 1)
    cmask = cmp_end <= pos_all
    s = lax.dot_general(kc, qb, nt, preferred_element_type=F32) * SCALE
    s = jnp.where(cmask, s, NEG)
    e = jnp.exp(s - _col_max(s))
    p = e * (1.0 / _col_sum(e))
    p = jnp.where(cmask, p, 0.0).astype(BF16)
    ocmp_ref[...] = jnp.dot(vc_t, p, preferred_element_type=F32)
    imp_all = jnp.dot(ovl_ref[...], p, preferred_element_type=F32)
    imp = imp_all[:, cols[0]]
    for c in cols[1:]:
        imp = imp + imp_all[:, c]

    val = imp[0:n_slc, :]
    blk = lax.broadcasted_iota(jnp.int32, (n_slc, 1), 0)
    cur = pos_row // SLC_BLOCK
    future = blk * SLC_BLOCK > pos_row
    forced = (blk == 0) | (blk == cur) | (blk == cur - 1)
    val = jnp.where(future, -jnp.inf, jnp.where(forced, jnp.inf, val))
    rank = jnp.zeros((n_slc, tq), jnp.int32)
    for j in range(n_slc):
        vj = val[j:j + 1, :]
        beats = (vj > val) | ((vj == val) & (blk > j))
        rank = rank + beats.astype(jnp.int32)
    sel_bias = jnp.where(rank < min(SLC_TOP_N, n_slc), 0.0, NEG)
    sel_bias = jnp.concatenate([sel_bias, jnp.zeros((LANES - n_slc, tq), F32)], axis=0)
    sel_bias = jnp.transpose(sel_bias).astype(BF16)
    qa_ref[:, 0:HEAD_DIM] = jnp.concatenate([qr_ref[:, h] for h in heads], axis=0)
    qa_ref[:, HEAD_DIM:] = jnp.concatenate([sel_bias] * NSA_REP, axis=0)

    ones_rows = (lax.broadcasted_iota(jnp.int32, (VT_PAD, tk), 0) == 0).astype(BF16)

    def scores(t_ref, x_ref, keys, queries, visible):
        t = lax.dot_general(keys, queries, nt, preferred_element_type=F32)
        if visible is not None:
            t = jnp.where(visible, t, NEG)
        t_ref[...] = t
        x_ref[...] = _col_max(t)

    def absorb(t_ref, x_ref, vt, m_ref, acc_ref):
        m_old = m_ref[...]
        m_new = jnp.maximum(m_old, x_ref[...])
        alpha = jnp.exp2(m_old - m_new)
        e = jnp.exp2(t_ref[...] - m_new).astype(BF16)
        vt_ones = jnp.concatenate([vt, ones_rows], axis=0)
        acc_ref[...] = alpha * acc_ref[...] + jnp.dot(vt_ones, e, preferred_element_type=F32)
        m_ref[...] = m_new

    def key_pos(c):
        return c * tk + lax.broadcasted_iota(jnp.int32, (tk, 1), 0)

    def slc_scores(t_ref, x_ref, c, causal):
        start = pl.multiple_of(c * tk, tk)
        keys = jnp.concatenate([ks_ref[pl.ds(start, tk), :], exp_ref[c]], axis=1)
        scores(t_ref, x_ref, keys, qa_ref[...], (key_pos(c) <= pos_all) if causal else None)

    def slc_absorb(t_ref, x_ref, c):
        absorb(t_ref, x_ref, vst_ref[c], ms_ref, accs_ref)

    slc_scores(ts0_ref, xs0_ref, 2 * i, True)
    slc_scores(ts1_ref, xs1_ref, 2 * i + 1, True)
    ms_ref[...] = jnp.maximum(xs0_ref[...], xs1_ref[...])
    accs_ref[...] = jnp.zeros(accs_ref.shape, F32)
    slc_absorb(ts0_ref, xs0_ref, 2 * i)

    def slc_pair(j, carry):
        slc_scores(ts0_ref, xs0_ref, 2 * j, False)
        slc_absorb(ts1_ref, xs1_ref, jnp.where(j == 0, 2 * i + 1, 2 * j - 1))
        slc_scores(ts1_ref, xs1_ref, 2 * j + 1, False)
        slc_absorb(ts0_ref, xs0_ref, 2 * j)
        return carry

    lax.fori_loop(0, i, slc_pair, 0)
    slc_absorb(ts1_ref, xs1_ref, jnp.where(i == 0, 1, 2 * i - 1))

    low = jnp.maximum(pos_all - WINDOW, -1)

    def win_scores(t_ref, x_ref, c, causal):
        kpos = key_pos(c)
        start = pl.multiple_of(jnp.maximum(c, 0) * tk, tk)
        scores(t_ref, x_ref, kw_ref[pl.ds(start, tk), :], qa_ref[:, 0:HEAD_DIM],
               (kpos <= pos_all) if causal else (kpos > low))

    def win_absorb(t_ref, x_ref, c):
        absorb(t_ref, x_ref, vwt_ref[jnp.maximum(c, 0)], mw_ref, accw_ref)

    mw_ref[...] = jnp.full(mw_ref.shape, NEG, F32)
    accw_ref[...] = jnp.zeros(accw_ref.shape, F32)
    win_scores(tw0_ref, xw0_ref, 2 * i, True)
    win_scores(tw1_ref, xw1_ref, 2 * i + 1, True)
    win_absorb(tw0_ref, xw0_ref, 2 * i)
    win_scores(tw0_ref, xw0_ref, 2 * i - 1, False)
    win_absorb(tw1_ref, xw1_ref, 2 * i + 1)
    win_scores(tw1_ref, xw1_ref, 2 * i - 2, False)
    win_absorb(tw0_ref, xw0_ref, 2 * i - 1)
    win_absorb(tw1_ref, xw1_ref, 2 * i - 2)

    ngt_ref[...] = jnp.transpose(ng_ref[...])
    inv_ls = 1.0 / accs_ref[HEAD_DIM:HEAD_DIM + 1, :]
    inv_lw = 1.0 / accw_ref[HEAD_DIM:HEAD_DIM + 1, :]
    for r in range(NSA_REP):
        row = (gi * NSA_REP + r) * NSA_BRANCHES
        g_cmp = _sigmoid(ngt_ref[pl.ds(row, 1), :])
        g_slc = _sigmoid(ngt_ref[pl.ds(row + 1, 1), :])
        g_win = _sigmoid(ngt_ref[pl.ds(row + 2, 1), :])
        o_t = (g_cmp * ocmp_ref[:, cols[r]]
               + (g_slc * inv_ls[:, cols[r]]) * accs_ref[0:HEAD_DIM, cols[r]]
               + (g_win * inv_lw[:, cols[r]]) * accw_ref[0:HEAD_DIM, cols[r]])
        o_ref[:, heads[r]] = (jnp.transpose(o_t) * _silu(nz_ref[:, heads[r]])).astype(o_ref.dtype)


def _nsa_attention(qb, qr, cmpkv, ks, vst, kw, vwt, proj, ovl, expand, bsz, seq, ng_blk, nz_blk):
    tq, tk = NSA_TQ, NSA_TK
    g = NSA_KV_GROUPS
    gw = NSA_REP * HEAD_DIM
    nq = seq // tq
    n_slc = seq // SLC_BLOCK
    n_half = cmpkv.shape[-2]
    stacked = NSA_REP * tq
    rowq = lambda cb: pl.BlockSpec((tq, gw), lambda b, gi, i: (b * nq + i, cb + gi))
    kvs = pl.BlockSpec((seq, HEAD_DIM), lambda b, gi, i: (b, gi))
    cmps = lambda c: pl.BlockSpec((None, None, None, n_half, HEAD_DIM),
                                  lambda b, gi, i: (b, c, gi, 0, 0))
    vts = pl.BlockSpec((None, seq // tk, HEAD_DIM, tk), lambda b, gi, i: (b, 0, gi, 0))
    score_buf = pltpu.VMEM((tk, stacked), F32)
    stat_row = pltpu.VMEM((1, stacked), F32)
    flash_acc = pltpu.VMEM((HEAD_DIM + VT_PAD, stacked), F32)
    return pl.pallas_call(
        functools.partial(_nsa_kernel, tq=tq, tk=tk, n_slc=n_slc),
        grid=(bsz, g, nq),
        in_specs=[rowq(0), rowq(0), cmps(0), cmps(1), kvs, vts, kvs, vts,
                  pl.BlockSpec((tq, LANES), lambda b, gi, i: (b * nq + i, ng_blk)),
                  rowq(nz_blk),
                  pl.BlockSpec((LANES, n_half), lambda b, gi, i: (0, 0)),
                  pl.BlockSpec((seq // tk, tk, LANES), lambda b, gi, i: (0, 0, 0))],
        out_specs=rowq(0),
        out_shape=jax.ShapeDtypeStruct((bsz * seq, NSA_HEADS * HEAD_DIM), BF16),
        scratch_shapes=[pltpu.VMEM((stacked, 2 * HEAD_DIM), BF16),
                        score_buf, score_buf, score_buf, score_buf,
                        stat_row, stat_row, stat_row, stat_row,
                        stat_row, stat_row,
                        flash_acc, flash_acc, pltpu.VMEM((HEAD_DIM, stacked), F32),
                        pltpu.VMEM((LANES, tq), F32)],
        compiler_params=_cparams("parallel", "parallel", "arbitrary"),
        name="nsa_attention",
    )(qb, qr, cmpkv, cmpkv, ks, vst, kw, vwt, proj, proj, ovl, expand)


def _merge_kernel(ya_ref, yb_ref, yx_ref, wa_ref, wb_ref, wx_ref, g0_ref, g1_ref, g2_ref, o_ref):
    def branch(y_ref, w_ref, g_ref):
        up = jnp.dot(y_ref[...], w_ref[...].astype(BF16), preferred_element_type=F32)
        return _sigmoid(g_ref[...]) * up

    u = branch(ya_ref, wa_ref, g0_ref) + branch(yb_ref, wb_ref, g1_ref) + branch(yx_ref, wx_ref, g2_ref)
    o_ref[...] = u.astype(o_ref.dtype)


def _merge(ya, yb, yx, wa, wb, wx, proj, l, mg_off, d_model, tm=2048, tn=256):
    t = ya.shape[0]
    nj = d_model // tn
    mgb = mg_off // tn
    act = lambda w: pl.BlockSpec((tm, w), lambda i, j: (i, 0), pipeline_mode=pl.Buffered(1))
    wsp = lambda w: pl.BlockSpec((None, w, tn), lambda i, j: (l, 0, j))
    gsp = lambda br: pl.BlockSpec((tm, tn), lambda i, j: (i, mgb + br * nj + j))
    return pl.pallas_call(
        _merge_kernel,
        grid=(t // tm, nj),
        in_specs=[act(ya.shape[1]), act(yb.shape[1]), act(yx.shape[1]),
                  wsp(wa.shape[1]), wsp(wb.shape[1]), wsp(wx.shape[1]),
                  gsp(0), gsp(1), gsp(2)],
        out_specs=pl.BlockSpec((tm, tn), lambda i, j: (i, j)),
        out_shape=jax.ShapeDtypeStruct((t, d_model), BF16),
        compiler_params=_cparams("parallel", "arbitrary"),
        name="gated_merge",
    )(ya, yb, yx, wa, wb, wx, proj, proj, proj)


def _rope_tables(seq):
    half = HEAD_DIM // 2
    inv_freq = ROPE_THETA ** (-jnp.arange(half, dtype=F32) / half)
    ang = jnp.arange(seq, dtype=jnp.int32).astype(F32)[:, None] * inv_freq[None, :]
    cos, sin = jnp.cos(ang), jnp.sin(ang)
    return jnp.concatenate([cos, cos], axis=-1), jnp.concatenate([-sin, sin], axis=-1)


def _overlap_table(seq, n_half):
    n_slc = seq // SLC_BLOCK
    cmp_start = np.arange(n_half) * CMP_STRIDE
    slc_start = np.arange(n_slc) * SLC_BLOCK
    ovl = ((cmp_start[None, :] < slc_start[:, None] + SLC_BLOCK)
           & (cmp_start[None, :] + CMP_BLOCK > slc_start[:, None]))
    ovl = ovl & (cmp_start[None, :] + CMP_BLOCK <= seq)
    out = np.zeros((LANES, n_half), np.float32)
    out[:n_slc, :] = ovl
    return jnp.asarray(out, BF16)


def _expand_table(seq, tk):
    key_blk = np.arange(seq) // SLC_BLOCK
    e = (key_blk[:, None] == np.arange(LANES)[None, :]).astype(np.float32)
    return jnp.asarray(e.reshape(seq // tk, tk, LANES), BF16)


def kernel(x, mem, norm_g, w_in, conv_w, conv_b, cmp_pos, cmp_w1, cmp_w2, mem_norm_g, w_mem_kv,
           w_up_a, w_up_b, w_up_x, w_out, final_g):
    bsz, seq, d_model = x.shape
    mem_len = mem.shape[1]
    depth = w_in.shape[0]
    sc_w = conv_w.shape[-1]
    nsa_w = NSA_HEADS * HEAD_DIM
    kv_w = NSA_BRANCHES * 2 * NSA_KV_GROUPS * HEAD_DIM
    ng_w = NSA_BRANCHES * NSA_HEADS
    x_w = X_HEADS * HEAD_DIM
    proj_tn = 512

    q_off = 4 * sc_w
    kv_off = q_off + nsa_w
    lo_w = kv_off + kv_w
    ng_off = lo_w
    nz_off = ng_off + proj_tn
    xq_off = nz_off + nsa_w
    xz_off = xq_off + x_w
    mg_off = xz_off + x_w
    assert seq % NSA_TQ == 0 and seq // SLC_BLOCK <= LANES and sc_w % 256 == 0
    assert 0 < ng_w < LANES and mg_off + N_BRANCHES * d_model - proj_tn + ng_w == w_in.shape[-1]

    w_in_t = jnp.swapaxes(w_in, 1, 2)
    w_memkv_b = w_mem_kv.astype(BF16)
    cmp_w1_b = cmp_w1.astype(BF16)
    cmp_w2_b = cmp_w2.astype(BF16)
    norm_g3 = norm_g.reshape(depth, 1, d_model)
    mem_norm_g3 = mem_norm_g.reshape(depth, 1, d_model)
    conv_b3 = conv_b.reshape(depth, 1, sc_w)

    cos, sin_signed = _rope_tables(seq)
    n_half = seq // CMP_STRIDE
    ovl = _overlap_table(seq, n_half)
    expand = _expand_table(seq, NSA_TK)

    t = bsz * seq
    xf = x.reshape(t, d_model)
    memf = mem.reshape(bsz * mem_len, d_model)

    hg, hr = _prenorm(xf, norm_g3, 0)
    for l in range(depth):
        proj = _in_proj(hg, hr, w_in_t, l, lo_w, ng_w, 2048, proj_tn)

        y_a = _conv_branch(proj, conv_w, conv_b3, l, bsz, seq, sc_w)

        hm = _rmsnorm(memf, mem_norm_g3, l, BF16)
        memkv = _matmul(hm, w_memkv_b, l, F32, 512, 1024, name="mem_kv_proj")
        y_x = _mem_attention(proj, memkv, bsz, seq, mem_len, xq_off // HEAD_DIM, xz_off // HEAD_DIM)

        cmpkv = _compress(proj, cmp_pos, cmp_w1_b, cmp_w2_b, l, bsz, seq, kv_off // HEAD_DIM)
        qb, qr, ks, vst, kw, vwt = _nsa_prep(proj, cos, sin_signed, bsz, seq, q_off, kv_off)
        y_b = _nsa_attention(qb, qr, cmpkv, ks, vst, kw, vwt, proj, ovl, expand, bsz, seq,
                             ng_off // LANES, nz_off // (NSA_REP * HEAD_DIM))

        u = _merge(y_a, y_b, y_x, w_up_a, w_up_b, w_up_x, proj, l, mg_off, d_model)
        if l + 1 < depth:
            xf, hg, hr = _out_proj(u, w_out, l, xf, 1024, 512, next_gain=norm_g3)
        else:
            xf = _out_proj(u, w_out, l, xf, 2048, 512)

    out = _rmsnorm(xf, final_g.reshape(1, 1, d_model), 0, F32)
    return out.reshape(bsz, seq, d_model)
```

```python
import functools

import numpy as np
import jax
import jax.numpy as jnp
from jax import lax
from jax.experimental import pallas as pl
from jax.experimental.pallas import tpu as pltpu

F32 = jnp.float32
BF16 = jnp.bfloat16

HEAD_DIM = 128
ROPE_THETA = 10000.0
EPS = 1e-6
NEG = -1e30
SC_KERNEL = 3
NSA_HEADS = 16
NSA_KV_GROUPS = 4
NSA_REP = NSA_HEADS // NSA_KV_GROUPS
NSA_BRANCHES = 3
CMP_BLOCK = 32
CMP_STRIDE = 16
SLC_BLOCK = 64
SLC_TOP_N = 16
WINDOW = 512
X_HEADS = 4
N_BRANCHES = 3
SCALE = HEAD_DIM ** -0.5
LOG2E = 1.4426950408889634

V7X_VMEM_BYTES = 64 * 1024 * 1024
VMEM_LIMIT_BYTES = V7X_VMEM_BYTES - 8 * 1024 * 1024
LANES = 128
MXU_WIDTH = 256

NSA_TQ = 512
NSA_TK = 256
VT_PAD = 16
MERGE_ROW_CHUNKS = 2
CONV_ROW_CHUNKS = 4


def _cparams(*sem):
    return pltpu.CompilerParams(dimension_semantics=sem, vmem_limit_bytes=VMEM_LIMIT_BYTES)


def _resident_rows_spec(tm, kdim):
    big = tm * kdim * 2 >= V7X_VMEM_BYTES // 4
    return pl.BlockSpec((tm, kdim), lambda i, j: (i, 0),
                        pipeline_mode=pl.Buffered(1) if big else None)


def _sigmoid(x):
    return 0.5 * jnp.tanh(0.5 * x) + 0.5


def _silu(x):
    return x * _sigmoid(x)


def _rmsnorm_kernel(x_ref, g_ref, o_ref):
    x = x_ref[...]
    ms = jnp.mean(x * x, axis=-1, keepdims=True)
    o_ref[...] = (x * lax.rsqrt(ms + EPS) * g_ref[...]).astype(o_ref.dtype)


def _rmsnorm(x, g3, l, out_dtype, tm=256):
    m, d = x.shape
    return pl.pallas_call(
        _rmsnorm_kernel,
        grid=(m // tm,),
        in_specs=[pl.BlockSpec((tm, d), lambda i: (i, 0)),
                  pl.BlockSpec((None, 1, d), lambda i: (l, 0, 0))],
        out_specs=pl.BlockSpec((tm, d), lambda i: (i, 0)),
        out_shape=jax.ShapeDtypeStruct((m, d), out_dtype),
        compiler_params=_cparams("parallel"),
        name="rmsnorm",
    )(x, g3)


def _prenorm_kernel(x_ref, g_ref, a_ref, r_ref):
    x = x_ref[...]
    r_ref[...] = lax.rsqrt(jnp.mean(x * x, axis=-1, keepdims=True) + EPS)
    a_ref[...] = (x * g_ref[...]).astype(a_ref.dtype)


def _prenorm(x, g3, l, tm=256):
    m, d = x.shape
    return pl.pallas_call(
        _prenorm_kernel,
        grid=(m // tm,),
        in_specs=[pl.BlockSpec((tm, d), lambda i: (i, 0)),
                  pl.BlockSpec((None, 1, d), lambda i: (l, 0, 0))],
        out_specs=[pl.BlockSpec((tm, d), lambda i: (i, 0)), pl.BlockSpec((tm, 1), lambda i: (i, 0))],
        out_shape=[jax.ShapeDtypeStruct((m, d), BF16), jax.ShapeDtypeStruct((m, 1), F32)],
        compiler_params=_cparams("parallel"),
        name="prenorm",
    )(x, g3)


def _mm_kernel(a_ref, b_ref, o_ref):
    o_ref[...] = jnp.dot(a_ref[...], b_ref[...], preferred_element_type=F32).astype(o_ref.dtype)


def _matmul(a, b, l, out_dtype, tm, tn, name):
    m, kdim = a.shape
    n = b.shape[-1]
    tm, tn = min(tm, m), min(tn, n)
    return pl.pallas_call(
        _mm_kernel,
        grid=(m // tm, n // tn),
        in_specs=[pl.BlockSpec((tm, kdim), lambda i, j: (i, 0)),
                  pl.BlockSpec((None, kdim, tn), lambda i, j: (l, 0, j))],
        out_specs=pl.BlockSpec((tm, tn), lambda i, j: (i, j)),
        out_shape=jax.ShapeDtypeStruct((m, n), out_dtype),
        compiler_params=_cparams("parallel", "arbitrary"),
        name=name,
    )(a, b)


def _in_proj_kernel(a_ref, w_ref, r_ref, o_ref):
    acc = lax.dot_general(a_ref[...], w_ref[0].astype(BF16), (((1,), (1,)), ((), ())),
                          preferred_element_type=F32)
    o_ref[...] = (acc * r_ref[...]).astype(o_ref.dtype)


def _in_proj(a, row_scale, wt, l, first, lo, gap, tm, tn):
    m, kdim = a.shape
    lo_tiles = lo // tn
    n = wt.shape[1] - first - gap + tn
    assert lo % tn == 0 and n % tn == 0 and gap % 16 == 0 and gap < tn and first % 16 == 0

    def w_index(i, j):
        row = first + jnp.where(j <= lo_tiles, j * tn, (j - 1) * tn + gap)
        return l, pl.multiple_of(row, 16), 0

    return pl.pallas_call(
        _in_proj_kernel,
        grid=(m // tm, n // tn),
        in_specs=[_resident_rows_spec(tm, kdim),
                  pl.BlockSpec((pl.Element(1), pl.Element(tn), pl.Element(kdim)), w_index),
                  pl.BlockSpec((tm, 1), lambda i, j: (i, 0))],
        out_specs=pl.BlockSpec((tm, tn), lambda i, j: (i, j)),
        out_shape=jax.ShapeDtypeStruct((m, n), F32),
        compiler_params=_cparams("parallel", "arbitrary"),
        name="in_proj",
    )(a, wt, row_scale)


def _out_proj_kernel(a_ref, w_ref, res_ref, o_ref):
    acc = jnp.dot(a_ref[...], w_ref[...].astype(BF16), preferred_element_type=F32)
    o_ref[...] = res_ref[...] + acc


def _out_proj_next_kernel(a_ref, w_ref, res_ref, g_ref, o_ref, an_ref, rn_ref, *, d_model):
    j = pl.program_id(1)
    ss = 0.0
    for c in range(0, o_ref.shape[1], MXU_WIDTH):
        sl = slice(c, c + MXU_WIDTH)
        acc = jnp.dot(a_ref[...], w_ref[:, sl].astype(BF16), preferred_element_type=F32)
        xn = res_ref[:, sl] + acc
        o_ref[:, sl] = xn
        an_ref[:, sl] = (xn * g_ref[:, sl]).astype(an_ref.dtype)
        ss = ss + jnp.sum(xn * xn, axis=-1, keepdims=True)

    @pl.when(j == 0)
    def _():
        rn_ref[...] = ss

    @pl.when(j > 0)
    def _():
        rn_ref[...] = rn_ref[...] + ss

    @pl.when(j == pl.num_programs(1) - 1)
    def _():
        rn_ref[...] = lax.rsqrt(rn_ref[...] * (1.0 / d_model) + EPS)


def _out_proj(a, w, l, residual, tm, tn, next_gain=None):
    m, kdim = a.shape
    n = w.shape[-1]
    tile = pl.BlockSpec((tm, tn), lambda i, j: (i, j))
    in_specs = [_resident_rows_spec(tm, kdim),
                pl.BlockSpec((None, kdim, tn), lambda i, j: (l, 0, j)), tile]
    if next_gain is None:
        return pl.pallas_call(
            _out_proj_kernel,
            grid=(m // tm, n // tn),
            in_specs=in_specs,
            out_specs=tile,
            out_shape=jax.ShapeDtypeStruct((m, n), F32),
            compiler_params=_cparams("parallel", "arbitrary"),
            name="out_proj",
        )(a, w, residual)
    return pl.pallas_call(
        functools.partial(_out_proj_next_kernel, d_model=n),
        grid=(m // tm, n // tn),
        in_specs=in_specs + [pl.BlockSpec((None, 1, tn), lambda i, j: (l + 1, 0, j))],
        out_specs=[tile, tile, pl.BlockSpec((tm, 1), lambda i, j: (i, 0))],
        out_shape=[jax.ShapeDtypeStruct((m, n), F32), jax.ShapeDtypeStruct((m, n), BF16),
                   jax.ShapeDtypeStruct((m, 1), F32)],
        compiler_params=_cparams("parallel", "arbitrary"),
        name="out_proj_next",
    )(a, w, residual, next_gain)


def _conv_proj_kernel(a_ref, wh_ref, wb_ref, wc_ref, wz_ref, r_ref, cw_ref, cb_ref, o_ref):
    w = jnp.concatenate([wh_ref[...], wb_ref[...], wc_ref[...], wz_ref[...]], axis=0).astype(BF16)
    cw = cw_ref[...]
    cb = cb_ref[...]
    seq, tc = o_ref.shape
    step = seq // CONV_ROW_CHUNKS
    row = lax.broadcasted_iota(jnp.int32, (step, tc), 0)
    u_prev = jnp.zeros((step, tc), F32)
    for r0 in range(0, seq, step):
        p = lax.dot_general(a_ref[r0:r0 + step, :], w, (((1,), (1,)), ((), ())),
                            preferred_element_type=F32) * r_ref[r0:r0 + step, :]
        h, b, c, z = (p[:, g * tc:(g + 1) * tc] for g in range(4))
        u = c * h
        u1 = jnp.where(row >= 1, pltpu.roll(u, 1, 0), pltpu.roll(u_prev, 1, 0))
        u2 = jnp.where(row >= 2, pltpu.roll(u, 2, 0), pltpu.roll(u_prev, 2, 0))
        y = cb + cw[0:1, :] * u2
        y = y + cw[1:2, :] * u1
        y = y + cw[2:3, :] * u
        o_ref[r0:r0 + step, :] = (b * y * _silu(z)).astype(o_ref.dtype)
        u_prev = u


def _conv_branch(a, row_scale, wt, conv_w, conv_b3, l, seq, width, tc=LANES):
    m, kdim = a.shape
    nb = width // tc
    wsp = lambda g: pl.BlockSpec((None, tc, kdim), lambda b, j: (l, g * nb + j, 0))
    return pl.pallas_call(
        _conv_proj_kernel,
        grid=(m // seq, nb),
        in_specs=[_resident_rows_spec(seq, kdim), wsp(0), wsp(1), wsp(2), wsp(3),
                  pl.BlockSpec((seq, 1), lambda b, j: (b, 0)),
                  pl.BlockSpec((None, SC_KERNEL, tc), lambda b, j: (l, 0, j)),
                  pl.BlockSpec((None, 1, tc), lambda b, j: (l, 0, j))],
        out_specs=pl.BlockSpec((seq, tc), lambda b, j: (b, j)),
        out_shape=jax.ShapeDtypeStruct((m, width), BF16),
        compiler_params=_cparams("parallel", "arbitrary"),
        name="conv_branch",
    )(a, wt, wt, wt, wt, row_scale, conv_w, conv_b3)


def _memattn_kernel(q_ref, k_ref, v_ref, z_ref, o_ref):
    q = q_ref[...].astype(BF16)
    k = k_ref[...].astype(BF16)
    v = v_ref[...].astype(BF16)
    s = lax.dot_general(q, k, (((1,), (1,)), ((), ())), preferred_element_type=F32) * SCALE
    m = jnp.max(s, axis=-1, keepdims=True)
    e = jnp.exp(s - m)
    p = e / jnp.sum(e, axis=-1, keepdims=True)
    o = jnp.dot(p.astype(BF16), v, preferred_element_type=F32)
    o_ref[...] = (o * _silu(z_ref[...])).astype(o_ref.dtype)


def _mem_attention(proj, memkv, bsz, seq, mem_len, q_blk, z_blk, tq=512):
    nq = seq // tq
    return pl.pallas_call(
        _memattn_kernel,
        grid=(bsz, X_HEADS, nq),
        in_specs=[pl.BlockSpec((tq, HEAD_DIM), lambda b, h, i: (b * nq + i, q_blk + h)),
                  pl.BlockSpec((mem_len, HEAD_DIM), lambda b, h, i: (b, h)),
                  pl.BlockSpec((mem_len, HEAD_DIM), lambda b, h, i: (b, X_HEADS + h)),
                  pl.BlockSpec((tq, HEAD_DIM), lambda b, h, i: (b * nq + i, z_blk + h))],
        out_specs=pl.BlockSpec((tq, HEAD_DIM), lambda b, h, i: (b * nq + i, h)),
        out_shape=jax.ShapeDtypeStruct((bsz * seq, X_HEADS * HEAD_DIM), BF16),
        compiler_params=_cparams("parallel", "parallel", "parallel"),
        name="mem_attention",
    )(proj, memkv, memkv, proj)


def _compress_kernel(kv_ref, pos_ref, w1_ref, w2_ref, o_ref, *, n_half):
    half = CMP_BLOCK // 2
    acc_a = jnp.zeros((n_half, HEAD_DIM), F32)
    acc_b = jnp.zeros((n_half, HEAD_DIM), F32)
    for t in range(half):
        x = kv_ref[pl.ds(t, n_half, stride=half), :]
        xa = (x + pos_ref[t:t + 1, :]).astype(BF16)
        xb = (x + pos_ref[half + t:half + t + 1, :]).astype(BF16)
        acc_a = acc_a + jnp.dot(xa, w1_ref[t * HEAD_DIM:(t + 1) * HEAD_DIM, :],
                                preferred_element_type=F32)
        acc_b = acc_b + jnp.dot(xb, w1_ref[(half + t) * HEAD_DIM:(half + t + 1) * HEAD_DIM, :],
                                preferred_element_type=F32)
    hid = _silu(acc_a + pltpu.roll(acc_b, n_half - 1, 0))
    o_ref[...] = jnp.dot(hid.astype(BF16), w2_ref[...], preferred_element_type=F32)


def _compress(proj, cmp_pos, cmp_w1, cmp_w2, l, bsz, seq, kv_blk):
    g = NSA_KV_GROUPS
    n_half = seq // (CMP_BLOCK // 2)
    return pl.pallas_call(
        functools.partial(_compress_kernel, n_half=n_half),
        grid=(bsz, 2, g),
        in_specs=[pl.BlockSpec((seq, HEAD_DIM), lambda b, c, gi: (b, kv_blk + c * g + gi)),
                  pl.BlockSpec((None, None, CMP_BLOCK, HEAD_DIM), lambda b, c, gi: (l, c, 0, 0)),
                  pl.BlockSpec((None, None, CMP_BLOCK * HEAD_DIM, HEAD_DIM),
                               lambda b, c, gi: (l, c, 0, 0)),
                  pl.BlockSpec((None, None, HEAD_DIM, HEAD_DIM), lambda b, c, gi: (l, c, 0, 0))],
        out_specs=pl.BlockSpec((None, None, None, n_half, HEAD_DIM),
                               lambda b, c, gi: (b, c, gi, 0, 0)),
        out_shape=jax.ShapeDtypeStruct((bsz, 2, g, n_half, HEAD_DIM), F32),
        compiler_params=_cparams("parallel", "parallel", "parallel"),
        name="nsa_compress",
    )(proj, cmp_pos, cmp_w1, cmp_w2)


def _rope_tile(x, cos, sin_signed):
    return x * cos + pltpu.roll(x, HEAD_DIM // 2, 1) * sin_signed


def _nsa_prep_kernel(q_ref, ks_ref, vs_ref, kw_ref, vw_ref, cos_ref, sin_ref,
                     qb_ref, qr_ref, kso_ref, vst_ref, kwo_ref, vwt_ref):
    cos = cos_ref[...]
    sin = sin_ref[...]
    for h in range(NSA_HEADS):
        sl = slice(h * HEAD_DIM, (h + 1) * HEAD_DIM)
        x = q_ref[:, sl]
        qb_ref[:, sl] = x.astype(BF16)
        qr_ref[:, sl] = (_rope_tile(x, cos, sin) * (SCALE * LOG2E)).astype(BF16)
    n_tiles, _, width = vst_ref.shape
    for g in range(NSA_KV_GROUPS):
        sl = slice(g * HEAD_DIM, (g + 1) * HEAD_DIM)
        kso_ref[:, sl] = _rope_tile(ks_ref[:, sl], cos, sin).astype(BF16)
        kwo_ref[:, sl] = _rope_tile(kw_ref[:, sl], cos, sin).astype(BF16)
        vs_t = jnp.transpose(vs_ref[:, sl]).astype(BF16)
        vw_t = jnp.transpose(vw_ref[:, sl]).astype(BF16)
        for c in range(n_tiles):
            vst_ref[c, sl, :] = vs_t[:, c * width:(c + 1) * width]
            vwt_ref[c, sl, :] = vw_t[:, c * width:(c + 1) * width]


def _nsa_prep(proj, cos, sin_signed, bsz, seq, q_off, kv_off):
    tr = NSA_TQ
    nsa_w = NSA_HEADS * HEAD_DIM
    gw = NSA_KV_GROUPS * HEAD_DIM
    nr = seq // tr
    t = bsz * seq
    qblk = q_off // nsa_w
    kvb = kv_off // gw
    n_tiles = tr // NSA_TK
    rowq = lambda w, cb: pl.BlockSpec((tr, w), lambda b, i: (b * nr + i, cb))
    tab = pl.BlockSpec((tr, HEAD_DIM), lambda b, i: (i, 0))
    vt_spec = pl.BlockSpec((None, n_tiles, gw, NSA_TK), lambda b, i: (b, i, 0, 0))
    vt_shape = jax.ShapeDtypeStruct((bsz, seq // NSA_TK, gw, NSA_TK), BF16)
    return pl.pallas_call(
        _nsa_prep_kernel,
        grid=(bsz, nr),
        in_specs=[rowq(nsa_w, qblk), rowq(gw, kvb + 2), rowq(gw, kvb + 3),
                  rowq(gw, kvb + 4), rowq(gw, kvb + 5), tab, tab],
        out_specs=[rowq(nsa_w, 0), rowq(nsa_w, 0), rowq(gw, 0), vt_spec, rowq(gw, 0), vt_spec],
        out_shape=[jax.ShapeDtypeStruct((t, nsa_w), BF16), jax.ShapeDtypeStruct((t, nsa_w), BF16),
                   jax.ShapeDtypeStruct((t, gw), BF16), vt_shape,
                   jax.ShapeDtypeStruct((t, gw), BF16), vt_shape],
        compiler_params=_cparams("parallel", "parallel"),
        name="nsa_prep",
    )(proj, proj, proj, proj, proj, cos, sin_signed)


def _col_reduce(x, op, ways=4):
    nblk = x.shape[0] // 8
    accs = [x[8 * k:8 * (k + 1)] for k in range(min(ways, nblk))]
    for k in range(len(accs), nblk):
        accs[k % ways] = op(accs[k % ways], x[8 * k:8 * (k + 1)])
    while len(accs) > 1:
        accs = [op(accs[a], accs[a + 1]) for a in range(0, len(accs), 2)]
    return accs[0]


def _col_max(x):
    return jnp.max(_col_reduce(x, jnp.maximum), axis=0, keepdims=True)


def _col_sum(x):
    return jnp.sum(_col_reduce(x, jnp.add), axis=0, keepdims=True)


def _nsa_kernel(qb_ref, qr_ref, kc_ref, vc_ref, ks_ref, vst_ref, kw_ref, vwt_ref,
                ng_ref, nz_ref, ovl_ref, exp_ref, o_ref,
                qa_ref, ts0_ref, ts1_ref, tw0_ref, tw1_ref, xs0_ref, xs1_ref, xw0_ref, xw1_ref,
                ms_ref, mw_ref, accs_ref, accw_ref, ocmp_ref, ngt_ref,
                *, tq, tk, n_slc):
    assert tq == 2 * tk and WINDOW == tq
    gi = pl.program_id(1)
    i = pl.program_id(2)
    q0 = i * tq
    nt = (((1,), (1,)), ((), ()))
    pos_row = q0 + lax.broadcasted_iota(jnp.int32, (1, tq), 1)
    pos_all = jnp.concatenate([pos_row] * NSA_REP, axis=1)
    heads = [slice(r * HEAD_DIM, (r + 1) * HEAD_DIM) for r in range(NSA_REP)]
    cols = [slice(r * tq, (r + 1) * tq) for r in range(NSA_REP)]

    qb = jnp.concatenate([qb_ref[:, h] for h in heads], axis=0)
    kc = kc_ref[...].astype(BF16)
    vc_t = jnp.transpose(vc_ref[...]).astype(BF16)
    n_cmp = kc.shape[0]
    cmp_end = CMP_STRIDE * lax.broadcasted_iota(jnp.int32, (n_cmp, 1), 0) + (CMP_BLOCK - 1)
    cmask = cmp_end <= pos_all
    s = lax.dot_general(kc, qb, nt, preferred_element_type=F32) * SCALE
    s = jnp.where(cmask, s, NEG)
    e = jnp.exp(s - _col_max(s))
    p = e * (1.0 / _col_sum(e))
    p = jnp.where(cmask, p, 0.0).astype(BF16)
    ocmp_ref[...] = jnp.dot(vc_t, p, preferred_element_type=F32)

    qa_ref[:, 0:HEAD_DIM] = jnp.concatenate([qr_ref[:, h] for h in heads], axis=0)
    n_top = min(SLC_TOP_N, n_slc)

    @pl.when((i + 1) * tq <= n_top * SLC_BLOCK)
    def _():
        qa_ref[:, HEAD_DIM:] = jnp.zeros((NSA_REP * tq, LANES), BF16)

    @pl.when((i + 1) * tq > n_top * SLC_BLOCK)
    def _():
        imp_all = jnp.dot(ovl_ref[...], p, preferred_element_type=F32)
        val = imp_all[:, cols[0]]
        for c in cols[1:]:
            val = val + imp_all[:, c]
        blk = lax.broadcasted_iota(jnp.int32, (n_slc, 1), 0)
        cur = pos_row // SLC_BLOCK
        future = blk * SLC_BLOCK > pos_row
        forced = (blk == 0) | (blk == cur) | (blk == cur - 1)
        val = jnp.where(future, -jnp.inf, jnp.where(forced, jnp.inf, val))
        rank = jnp.zeros((n_slc, tq), jnp.int32)
        for j in range(n_slc):
            vj = val[j:j + 1, :]
            beats = (vj > val) | ((vj == val) & (blk > j))
            rank = rank + beats.astype(jnp.int32)
        sel_bias = jnp.where(rank < n_top, 0.0, NEG)
        sel_bias = jnp.concatenate([sel_bias, jnp.zeros((LANES - n_slc, tq), F32)], axis=0)
        sel_bias = jnp.transpose(sel_bias).astype(BF16)
        qa_ref[:, HEAD_DIM:] = jnp.concatenate([sel_bias] * NSA_REP, axis=0)

    ones_rows = (lax.broadcasted_iota(jnp.int32, (VT_PAD, tk), 0) == 0).astype(BF16)

    def scores(t_ref, x_ref, keys, queries, visible):
        t = lax.dot_general(keys, queries, nt, preferred_element_type=F32)
        if visible is not None:
            t = jnp.where(visible, t, NEG)
        t_ref[...] = t
        x_ref[...] = _col_max(t)

    def absorb(t_ref, x_ref, vt, m_ref, acc_ref):
        m_old = m_ref[...]
        m_new = jnp.maximum(m_old, x_ref[...])
        alpha = jnp.exp2(m_old - m_new)
        e = jnp.exp2(t_ref[...] - m_new).astype(BF16)
        vt_ones = jnp.concatenate([vt, ones_rows], axis=0)
        acc_ref[...] = alpha * acc_ref[...] + jnp.dot(vt_ones, e, preferred_element_type=F32)
        m_ref[...] = m_new

    def key_pos(c):
        return c * tk + lax.broadcasted_iota(jnp.int32, (tk, 1), 0)

    def slc_scores(t_ref, x_ref, c, causal):
        start = pl.multiple_of(c * tk, tk)
        keys = jnp.concatenate([ks_ref[pl.ds(start, tk), :], exp_ref[c]], axis=1)
        scores(t_ref, x_ref, keys, qa_ref[...], (key_pos(c) <= pos_all) if causal else None)

    def slc_absorb(t_ref, x_ref, c):
        absorb(t_ref, x_ref, vst_ref[c], ms_ref, accs_ref)

    slc_scores(ts0_ref, xs0_ref, 2 * i, True)
    slc_scores(ts1_ref, xs1_ref, 2 * i + 1, True)
    ms_ref[...] = jnp.maximum(xs0_ref[...], xs1_ref[...])
    accs_ref[...] = jnp.zeros(accs_ref.shape, F32)
    slc_absorb(ts0_ref, xs0_ref, 2 * i)

    def slc_pair(j, carry):
        slc_scores(ts0_ref, xs0_ref, 2 * j, False)
        slc_absorb(ts1_ref, xs1_ref, jnp.where(j == 0, 2 * i + 1, 2 * j - 1))
        slc_scores(ts1_ref, xs1_ref, 2 * j + 1, False)
        slc_absorb(ts0_ref, xs0_ref, 2 * j)
        return carry

    lax.fori_loop(0, i, slc_pair, 0)
    slc_absorb(ts1_ref, xs1_ref, jnp.where(i == 0, 1, 2 * i - 1))

    low = jnp.maximum(pos_all - WINDOW, -1)

    def win_scores(t_ref, x_ref, c, causal):
        kpos = key_pos(c)
        start = pl.multiple_of(jnp.maximum(c, 0) * tk, tk)
        scores(t_ref, x_ref, kw_ref[pl.ds(start, tk), :], qa_ref[:, 0:HEAD_DIM],
               (kpos <= pos_all) if causal else (kpos > low))

    def win_absorb(t_ref, x_ref, c):
        absorb(t_ref, x_ref, vwt_ref[jnp.maximum(c, 0)], mw_ref, accw_ref)

    mw_ref[...] = jnp.full(mw_ref.shape, NEG, F32)
    accw_ref[...] = jnp.zeros(accw_ref.shape, F32)
    win_scores(tw0_ref, xw0_ref, 2 * i, True)
    win_scores(tw1_ref, xw1_ref, 2 * i + 1, True)
    win_absorb(tw0_ref, xw0_ref, 2 * i)
    win_scores(tw0_ref, xw0_ref, 2 * i - 1, False)
    win_absorb(tw1_ref, xw1_ref, 2 * i + 1)
    win_scores(tw1_ref, xw1_ref, 2 * i - 2, False)
    win_absorb(tw0_ref, xw0_ref, 2 * i - 1)
    win_absorb(tw1_ref, xw1_ref, 2 * i - 2)

    ngt_ref[...] = jnp.transpose(ng_ref[...])
    inv_ls = 1.0 / accs_ref[HEAD_DIM:HEAD_DIM + 1, :]
    inv_lw = 1.0 / accw_ref[HEAD_DIM:HEAD_DIM + 1, :]
    for r in range(NSA_REP):
        row = (gi * NSA_REP + r) * NSA_BRANCHES
        g_cmp = _sigmoid(ngt_ref[pl.ds(row, 1), :])
        g_slc = _sigmoid(ngt_ref[pl.ds(row + 1, 1), :])
        g_win = _sigmoid(ngt_ref[pl.ds(row + 2, 1), :])
        o_t = (g_cmp * ocmp_ref[:, cols[r]]
               + (g_slc * inv_ls[:, cols[r]]) * accs_ref[0:HEAD_DIM, cols[r]]
               + (g_win * inv_lw[:, cols[r]]) * accw_ref[0:HEAD_DIM, cols[r]])
        o_ref[:, heads[r]] = (jnp.transpose(o_t) * _silu(nz_ref[:, heads[r]])).astype(o_ref.dtype)


def _nsa_attention(qb, qr, cmpkv, ks, vst, kw, vwt, proj, ovl, expand, bsz, seq, ng_blk, nz_blk):
    tq, tk = NSA_TQ, NSA_TK
    g = NSA_KV_GROUPS
    gw = NSA_REP * HEAD_DIM
    nq = seq // tq
    n_slc = seq // SLC_BLOCK
    n_half = cmpkv.shape[-2]
    stacked = NSA_REP * tq
    rowq = lambda cb: pl.BlockSpec((tq, gw), lambda b, gi, i: (b * nq + i, cb + gi))
    kvs = pl.BlockSpec((seq, HEAD_DIM), lambda b, gi, i: (b, gi))
    cmps = lambda c: pl.BlockSpec((None, None, None, n_half, HEAD_DIM),
                                  lambda b, gi, i: (b, c, gi, 0, 0))
    vts = pl.BlockSpec((None, seq // tk, HEAD_DIM, tk), lambda b, gi, i: (b, 0, gi, 0))
    score_buf = pltpu.VMEM((tk, stacked), F32)
    stat_row = pltpu.VMEM((1, stacked), F32)
    flash_acc = pltpu.VMEM((HEAD_DIM + VT_PAD, stacked), F32)
    return pl.pallas_call(
        functools.partial(_nsa_kernel, tq=tq, tk=tk, n_slc=n_slc),
        grid=(bsz, g, nq),
        in_specs=[rowq(0), rowq(0), cmps(0), cmps(1), kvs, vts, kvs, vts,
                  pl.BlockSpec((tq, LANES), lambda b, gi, i: (b * nq + i, ng_blk)),
                  rowq(nz_blk),
                  pl.BlockSpec((n_slc, n_half), lambda b, gi, i: (0, 0)),
                  pl.BlockSpec((seq // tk, tk, LANES), lambda b, gi, i: (0, 0, 0))],
        out_specs=rowq(0),
        out_shape=jax.ShapeDtypeStruct((bsz * seq, NSA_HEADS * HEAD_DIM), BF16),
        scratch_shapes=[pltpu.VMEM((stacked, 2 * HEAD_DIM), BF16),
                        score_buf, score_buf, score_buf, score_buf,
                        stat_row, stat_row, stat_row, stat_row,
                        stat_row, stat_row,
                        flash_acc, flash_acc, pltpu.VMEM((HEAD_DIM, stacked), F32),
                        pltpu.VMEM((LANES, tq), F32)],
        compiler_params=_cparams("parallel", "parallel", "arbitrary"),
        name="nsa_attention",
    )(qb, qr, cmpkv, cmpkv, ks, vst, kw, vwt, proj, proj, ovl, expand)


def _merge_kernel(ya_ref, yb_ref, yx_ref, wa_ref, wb_ref, wx_ref, g0_ref, g1_ref, g2_ref, o_ref):
    wa = wa_ref[...].astype(BF16)
    wb = wb_ref[...].astype(BF16)
    wx = wx_ref[...].astype(BF16)
    tm = o_ref.shape[0]
    step = tm // MERGE_ROW_CHUNKS
    for r0 in range(0, tm, step):
        rows = slice(r0, r0 + step)

        def branch(y_ref, w, g_ref):
            up = jnp.dot(y_ref[rows, :], w, preferred_element_type=F32)
            return _sigmoid(g_ref[rows, :]) * up

        u = branch(ya_ref, wa, g0_ref) + branch(yb_ref, wb, g1_ref) + branch(yx_ref, wx, g2_ref)
        o_ref[rows, :] = u.astype(o_ref.dtype)


def _merge(ya, yb, yx, wa, wb, wx, proj, l, mg_off, d_model, tm=2048, tn=256):
    t = ya.shape[0]
    nj = d_model // tn
    mgb = mg_off // tn
    act = lambda w: pl.BlockSpec((tm, w), lambda i, j: (i, 0), pipeline_mode=pl.Buffered(1))
    wsp = lambda w: pl.BlockSpec((None, w, tn), lambda i, j: (l, 0, j))
    gsp = lambda br: pl.BlockSpec((tm, tn), lambda i, j: (i, mgb + br * nj + j))
    return pl.pallas_call(
        _merge_kernel,
        grid=(t // tm, nj),
        in_specs=[act(ya.shape[1]), act(yb.shape[1]), act(yx.shape[1]),
                  wsp(wa.shape[1]), wsp(wb.shape[1]), wsp(wx.shape[1]),
                  gsp(0), gsp(1), gsp(2)],
        out_specs=pl.BlockSpec((tm, tn), lambda i, j: (i, j)),
        out_shape=jax.ShapeDtypeStruct((t, d_model), BF16),
        compiler_params=_cparams("parallel", "arbitrary"),
        name="gated_merge",
    )(ya, yb, yx, wa, wb, wx, proj, proj, proj)


def _rope_tables(seq):
    half = HEAD_DIM // 2
    inv_freq = ROPE_THETA ** (-jnp.arange(half, dtype=F32) / half)
    ang = jnp.arange(seq, dtype=jnp.int32).astype(F32)[:, None] * inv_freq[None, :]
    cos, sin = jnp.cos(ang), jnp.sin(ang)
    return jnp.concatenate([cos, cos], axis=-1), jnp.concatenate([-sin, sin], axis=-1)


def _overlap_table(seq, n_half):
    n_slc = seq // SLC_BLOCK
    cmp_start = np.arange(n_half) * CMP_STRIDE
    slc_start = np.arange(n_slc) * SLC_BLOCK
    ovl = ((cmp_start[None, :] < slc_start[:, None] + SLC_BLOCK)
           & (cmp_start[None, :] + CMP_BLOCK > slc_start[:, None]))
    ovl = ovl & (cmp_start[None, :] + CMP_BLOCK <= seq)
    return jnp.asarray(ovl.astype(np.float32), BF16)


def _expand_table(seq, tk):
    key_blk = np.arange(seq) // SLC_BLOCK
    e = (key_blk[:, None] == np.arange(LANES)[None, :]).astype(np.float32)
    return jnp.asarray(e.reshape(seq // tk, tk, LANES), BF16)


def kernel(x, mem, norm_g, w_in, conv_w, conv_b, cmp_pos, cmp_w1, cmp_w2, mem_norm_g, w_mem_kv,
           w_up_a, w_up_b, w_up_x, w_out, final_g):
    bsz, seq, d_model = x.shape
    mem_len = mem.shape[1]
    depth = w_in.shape[0]
    sc_w = conv_w.shape[-1]
    nsa_w = NSA_HEADS * HEAD_DIM
    kv_w = NSA_BRANCHES * 2 * NSA_KV_GROUPS * HEAD_DIM
    ng_w = NSA_BRANCHES * NSA_HEADS
    x_w = X_HEADS * HEAD_DIM
    proj_tn = 512

    conv_cols = 4 * sc_w
    q_off = 0
    kv_off = q_off + nsa_w
    ng_off = kv_off + kv_w
    nz_off = ng_off + proj_tn
    xq_off = nz_off + nsa_w
    xz_off = xq_off + x_w
    mg_off = xz_off + x_w
    assert seq % NSA_TQ == 0 and seq // SLC_BLOCK <= LANES and sc_w % 256 == 0 and 0 < ng_w < LANES
    assert conv_cols + mg_off + N_BRANCHES * d_model - proj_tn + ng_w == w_in.shape[-1]

    w_in_t = jnp.swapaxes(w_in, 1, 2)
    w_memkv_b = w_mem_kv.astype(BF16)
    cmp_w1_b = cmp_w1.astype(BF16)
    cmp_w2_b = cmp_w2.astype(BF16)
    norm_g3 = norm_g.reshape(depth, 1, d_model)
    mem_norm_g3 = mem_norm_g.reshape(depth, 1, d_model)
    conv_b3 = conv_b.reshape(depth, 1, sc_w)

    cos, sin_signed = _rope_tables(seq)
    n_half = seq // CMP_STRIDE
    ovl = _overlap_table(seq, n_half)
    expand = _expand_table(seq, NSA_TK)

    t = bsz * seq
    xf = x.reshape(t, d_model)
    memf = mem.reshape(bsz * mem_len, d_model)

    hg, hr = _prenorm(xf, norm_g3, 0)
    for l in range(depth):
        y_a = _conv_branch(hg, hr, w_in_t, conv_w, conv_b3, l, seq, sc_w)
        proj = _in_proj(hg, hr, w_in_t, l, conv_cols, ng_off, ng_w, 2048, proj_tn)

        hm = _rmsnorm(memf, mem_norm_g3, l, BF16)
        memkv = _matmul(hm, w_memkv_b, l, F32, 512, 1024, name="mem_kv_proj")
        y_x = _mem_attention(proj, memkv, bsz, seq, mem_len, xq_off // HEAD_DIM, xz_off // HEAD_DIM)

        cmpkv = _compress(proj, cmp_pos, cmp_w1_b, cmp_w2_b, l, bsz, seq, kv_off // HEAD_DIM)
        qb, qr, ks, vst, kw, vwt = _nsa_prep(proj, cos, sin_signed, bsz, seq, q_off, kv_off)
        y_b = _nsa_attention(qb, qr, cmpkv, ks, vst, kw, vwt, proj, ovl, expand, bsz, seq,
                             ng_off // LANES, nz_off // (NSA_REP * HEAD_DIM))

        u = _merge(y_a, y_b, y_x, w_up_a, w_up_b, w_up_x, proj, l, mg_off, d_model)
        if l + 1 < depth:
            xf, hg, hr = _out_proj(u, w_out, l, xf, 1024, 512, next_gain=norm_g3)
        else:
            xf = _out_proj(u, w_out, l, xf, 2048, 512)

    out = _rmsnorm(xf, final_g.reshape(1, 1, d_model), 0, F32)
    return out.reshape(bsz, seq, d_model)
```

```python
import functools

import numpy as np
import jax
import jax.numpy as jnp
from jax import lax
from jax.experimental import pallas as pl
from jax.experimental.pallas import tpu as pltpu

F32 = jnp.float32
BF16 = jnp.bfloat16

HEAD_DIM = 128
ROPE_THETA = 10000.0
EPS = 1e-6
NEG = -1e30
SC_KERNEL = 3
NSA_HEADS = 16
NSA_KV_GROUPS = 4
NSA_REP = NSA_HEADS // NSA_KV_GROUPS
NSA_BRANCHES = 3
CMP_BLOCK = 32
CMP_STRIDE = 16
SLC_BLOCK = 64
SLC_TOP_N = 16
WINDOW = 512
X_HEADS = 4
N_BRANCHES = 3
SCALE = HEAD_DIM ** -0.5
LOG2E = 1.4426950408889634

V7X_VMEM_BYTES = 64 * 1024 * 1024
VMEM_LIMIT_BYTES = V7X_VMEM_BYTES - 8 * 1024 * 1024
LANES = 128

NSA_TQ = 512
NSA_TK = 256
VT_PAD = 16
MERGE_ROW_CHUNKS = 2
OUT_ROW_CHUNKS = 2
CONV_ROW_CHUNKS = 4


def _cparams(*sem):
    return pltpu.CompilerParams(dimension_semantics=sem, vmem_limit_bytes=VMEM_LIMIT_BYTES)


def _resident_rows_spec(tm, kdim):
    big = tm * kdim * 2 >= V7X_VMEM_BYTES // 4
    return pl.BlockSpec((tm, kdim), lambda i, j: (i, 0),
                        pipeline_mode=pl.Buffered(1) if big else None)


def _sigmoid(x):
    return 0.5 * jnp.tanh(0.5 * x) + 0.5


def _silu(x):
    return x * _sigmoid(x)


def _rmsnorm_kernel(x_ref, g_ref, o_ref):
    x = x_ref[...]
    ms = jnp.mean(x * x, axis=-1, keepdims=True)
    o_ref[...] = (x * lax.rsqrt(ms + EPS) * g_ref[...]).astype(o_ref.dtype)


def _rmsnorm(x, g3, l, out_dtype, tm=256):
    m, d = x.shape
    return pl.pallas_call(
        _rmsnorm_kernel,
        grid=(m // tm,),
        in_specs=[pl.BlockSpec((tm, d), lambda i: (i, 0)),
                  pl.BlockSpec((None, 1, d), lambda i: (l, 0, 0))],
        out_specs=pl.BlockSpec((tm, d), lambda i: (i, 0)),
        out_shape=jax.ShapeDtypeStruct((m, d), out_dtype),
        compiler_params=_cparams("parallel"),
        name="rmsnorm",
    )(x, g3)


def _prenorm_kernel(x_ref, g_ref, a_ref, r_ref):
    x = x_ref[...]
    r_ref[...] = lax.rsqrt(jnp.mean(x * x, axis=-1, keepdims=True) + EPS)
    a_ref[...] = (x * g_ref[...]).astype(a_ref.dtype)


def _prenorm(x, g3, l, tm=256):
    m, d = x.shape
    return pl.pallas_call(
        _prenorm_kernel,
        grid=(m // tm,),
        in_specs=[pl.BlockSpec((tm, d), lambda i: (i, 0)),
                  pl.BlockSpec((None, 1, d), lambda i: (l, 0, 0))],
        out_specs=[pl.BlockSpec((tm, d), lambda i: (i, 0)), pl.BlockSpec((tm, 1), lambda i: (i, 0))],
        out_shape=[jax.ShapeDtypeStruct((m, d), BF16), jax.ShapeDtypeStruct((m, 1), F32)],
        compiler_params=_cparams("parallel"),
        name="prenorm",
    )(x, g3)


def _mm_kernel(a_ref, b_ref, o_ref):
    o_ref[...] = jnp.dot(a_ref[...], b_ref[...], preferred_element_type=F32).astype(o_ref.dtype)


def _matmul(a, b, l, out_dtype, tm, tn, name):
    m, kdim = a.shape
    n = b.shape[-1]
    tm, tn = min(tm, m), min(tn, n)
    return pl.pallas_call(
        _mm_kernel,
        grid=(m // tm, n // tn),
        in_specs=[pl.BlockSpec((tm, kdim), lambda i, j: (i, 0)),
                  pl.BlockSpec((None, kdim, tn), lambda i, j: (l, 0, j))],
        out_specs=pl.BlockSpec((tm, tn), lambda i, j: (i, j)),
        out_shape=jax.ShapeDtypeStruct((m, n), out_dtype),
        compiler_params=_cparams("parallel", "arbitrary"),
        name=name,
    )(a, b)


def _in_proj_kernel(a_ref, w_ref, r_ref, o_ref):
    acc = lax.dot_general(a_ref[...], w_ref[0].astype(BF16), (((1,), (1,)), ((), ())),
                          preferred_element_type=F32)
    o_ref[...] = (acc * r_ref[...]).astype(o_ref.dtype)


def _in_proj(a, row_scale, wt, l, first, lo, gap, tm, tn):
    m, kdim = a.shape
    lo_tiles = lo // tn
    n = wt.shape[1] - first - gap + tn
    assert lo % tn == 0 and n % tn == 0 and gap % 16 == 0 and gap < tn and first % 16 == 0

    def w_index(i, j):
        row = first + jnp.where(j <= lo_tiles, j * tn, (j - 1) * tn + gap)
        return l, pl.multiple_of(row, 16), 0

    return pl.pallas_call(
        _in_proj_kernel,
        grid=(m // tm, n // tn),
        in_specs=[_resident_rows_spec(tm, kdim),
                  pl.BlockSpec((pl.Element(1), pl.Element(tn), pl.Element(kdim)), w_index),
                  pl.BlockSpec((tm, 1), lambda i, j: (i, 0))],
        out_specs=pl.BlockSpec((tm, tn), lambda i, j: (i, j)),
        out_shape=jax.ShapeDtypeStruct((m, n), F32),
        compiler_params=_cparams("parallel", "arbitrary"),
        name="in_proj",
    )(a, wt, row_scale)


def _out_proj_kernel(a_ref, w_ref, res_ref, o_ref):
    w = w_ref[...].astype(BF16)
    tm = o_ref.shape[0]
    step = tm // OUT_ROW_CHUNKS
    for r0 in range(0, tm, step):
        rows = slice(r0, r0 + step)
        acc = jnp.dot(a_ref[rows, :], w, preferred_element_type=F32)
        o_ref[rows, :] = res_ref[rows, :] + acc


def _out_proj_next_kernel(a_ref, w_ref, res_ref, g_ref, o_ref, an_ref, rn_ref, *, d_model):
    j = pl.program_id(1)
    last = pl.num_programs(1) - 1

    @pl.when(j == 0)
    def _():
        rn_ref[...] = jnp.zeros(rn_ref.shape, F32)

    w = w_ref[...].astype(BF16)
    g = g_ref[...]
    tm = o_ref.shape[0]
    step = tm // OUT_ROW_CHUNKS
    for r0 in range(0, tm, step):
        rows = slice(r0, r0 + step)
        xn = res_ref[rows, :] + jnp.dot(a_ref[rows, :], w, preferred_element_type=F32)
        o_ref[rows, :] = xn
        an_ref[rows, :] = (xn * g).astype(an_ref.dtype)
        ss = rn_ref[rows, :] + jnp.sum(xn * xn, axis=-1, keepdims=True)
        rn_ref[rows, :] = jnp.where(j == last, lax.rsqrt(ss * (1.0 / d_model) + EPS), ss)


def _out_proj(a, w, l, residual, tm, tn, next_gain=None):
    m, kdim = a.shape
    n = w.shape[-1]
    tile = pl.BlockSpec((tm, tn), lambda i, j: (i, j))
    in_specs = [_resident_rows_spec(tm, kdim),
                pl.BlockSpec((None, kdim, tn), lambda i, j: (l, 0, j)), tile]
    if next_gain is None:
        return pl.pallas_call(
            _out_proj_kernel,
            grid=(m // tm, n // tn),
            in_specs=in_specs,
            out_specs=tile,
            out_shape=jax.ShapeDtypeStruct((m, n), F32),
            compiler_params=_cparams("parallel", "arbitrary"),
            name="out_proj",
        )(a, w, residual)
    return pl.pallas_call(
        functools.partial(_out_proj_next_kernel, d_model=n),
        grid=(m // tm, n // tn),
        in_specs=in_specs + [pl.BlockSpec((None, 1, tn), lambda i, j: (l + 1, 0, j))],
        out_specs=[tile, tile, pl.BlockSpec((tm, 1), lambda i, j: (i, 0))],
        out_shape=[jax.ShapeDtypeStruct((m, n), F32), jax.ShapeDtypeStruct((m, n), BF16),
                   jax.ShapeDtypeStruct((m, 1), F32)],
        compiler_params=_cparams("parallel", "arbitrary"),
        name="out_proj_next",
    )(a, w, residual, next_gain)


def _conv_proj_kernel(a_ref, wh_ref, wb_ref, wc_ref, wz_ref, r_ref, cw_ref, cb_ref, o_ref):
    w = jnp.concatenate([wh_ref[...], wb_ref[...], wc_ref[...], wz_ref[...]], axis=0).astype(BF16)
    cw = cw_ref[...]
    cb = cb_ref[...]
    seq, tc = o_ref.shape
    step = seq // CONV_ROW_CHUNKS
    row = lax.broadcasted_iota(jnp.int32, (step, tc), 0)
    u_prev = jnp.zeros((step, tc), F32)
    for r0 in range(0, seq, step):
        p = lax.dot_general(a_ref[r0:r0 + step, :], w, (((1,), (1,)), ((), ())),
                            preferred_element_type=F32) * r_ref[r0:r0 + step, :]
        h, b, c, z = (p[:, g * tc:(g + 1) * tc] for g in range(4))
        u = c * h
        u1 = jnp.where(row >= 1, pltpu.roll(u, 1, 0), pltpu.roll(u_prev, 1, 0))
        u2 = jnp.where(row >= 2, pltpu.roll(u, 2, 0), pltpu.roll(u_prev, 2, 0))
        y = cb + cw[0:1, :] * u2
        y = y + cw[1:2, :] * u1
        y = y + cw[2:3, :] * u
        o_ref[r0:r0 + step, :] = (b * y * _silu(z)).astype(o_ref.dtype)
        u_prev = u


def _conv_branch(a, row_scale, wt, conv_w, conv_b3, l, seq, width, tc=LANES):
    m, kdim = a.shape
    nb = width // tc
    wsp = lambda g: pl.BlockSpec((None, tc, kdim), lambda b, j: (l, g * nb + j, 0))
    return pl.pallas_call(
        _conv_proj_kernel,
        grid=(m // seq, nb),
        in_specs=[_resident_rows_spec(seq, kdim), wsp(0), wsp(1), wsp(2), wsp(3),
                  pl.BlockSpec((seq, 1), lambda b, j: (b, 0)),
                  pl.BlockSpec((None, SC_KERNEL, tc), lambda b, j: (l, 0, j)),
                  pl.BlockSpec((None, 1, tc), lambda b, j: (l, 0, j))],
        out_specs=pl.BlockSpec((seq, tc), lambda b, j: (b, j)),
        out_shape=jax.ShapeDtypeStruct((m, width), BF16),
        compiler_params=_cparams("parallel", "arbitrary"),
        name="conv_branch",
    )(a, wt, wt, wt, wt, row_scale, conv_w, conv_b3)


def _memattn_kernel(q_ref, k_ref, v_ref, z_ref, o_ref):
    for h in range(X_HEADS):
        sl = slice(h * HEAD_DIM, (h + 1) * HEAD_DIM)
        q = q_ref[:, sl].astype(BF16)
        k = k_ref[:, sl].astype(BF16)
        v = v_ref[:, sl].astype(BF16)
        s = lax.dot_general(q, k, (((1,), (1,)), ((), ())), preferred_element_type=F32) * SCALE
        e = jnp.exp(s - jnp.max(s, axis=-1, keepdims=True))
        p = e / jnp.sum(e, axis=-1, keepdims=True)
        o = jnp.dot(p.astype(BF16), v, preferred_element_type=F32)
        o_ref[:, sl] = (o * _silu(z_ref[:, sl])).astype(o_ref.dtype)


def _mem_attention(proj, memkv, bsz, seq, mem_len, q_blk, z_blk, tq=512):
    nq = seq // tq
    xw = X_HEADS * HEAD_DIM
    row = lambda cb: pl.BlockSpec((tq, xw), lambda b, i: (b * nq + i, cb))
    return pl.pallas_call(
        _memattn_kernel,
        grid=(bsz, nq),
        in_specs=[row(q_blk),
                  pl.BlockSpec((mem_len, xw), lambda b, i: (b, 0)),
                  pl.BlockSpec((mem_len, xw), lambda b, i: (b, 1)),
                  row(z_blk)],
        out_specs=row(0),
        out_shape=jax.ShapeDtypeStruct((bsz * seq, xw), BF16),
        compiler_params=_cparams("parallel", "parallel"),
        name="mem_attention",
    )(proj, memkv, memkv, proj)


def _compress_kernel(kv_ref, pos_ref, w1_ref, w2_ref, o_ref, *, n_half):
    half = CMP_BLOCK // 2
    acc_a = jnp.zeros((n_half, HEAD_DIM), F32)
    acc_b = jnp.zeros((n_half, HEAD_DIM), F32)
    for t in range(half):
        x = kv_ref[pl.ds(t, n_half, stride=half), :]
        xa = (x + pos_ref[t:t + 1, :]).astype(BF16)
        xb = (x + pos_ref[half + t:half + t + 1, :]).astype(BF16)
        acc_a = acc_a + jnp.dot(xa, w1_ref[t * HEAD_DIM:(t + 1) * HEAD_DIM, :],
                                preferred_element_type=F32)
        acc_b = acc_b + jnp.dot(xb, w1_ref[(half + t) * HEAD_DIM:(half + t + 1) * HEAD_DIM, :],
                                preferred_element_type=F32)
    hid = _silu(acc_a + pltpu.roll(acc_b, n_half - 1, 0))
    o_ref[...] = jnp.dot(hid.astype(BF16), w2_ref[...], preferred_element_type=F32)


def _compress(proj, cmp_pos, cmp_w1, cmp_w2, l, bsz, seq, kv_blk):
    g = NSA_KV_GROUPS
    n_half = seq // (CMP_BLOCK // 2)
    return pl.pallas_call(
        functools.partial(_compress_kernel, n_half=n_half),
        grid=(bsz, 2, g),
        in_specs=[pl.BlockSpec((seq, HEAD_DIM), lambda b, c, gi: (b, kv_blk + c * g + gi)),
                  pl.BlockSpec((None, None, CMP_BLOCK, HEAD_DIM), lambda b, c, gi: (l, c, 0, 0)),
                  pl.BlockSpec((None, None, CMP_BLOCK * HEAD_DIM, HEAD_DIM),
                               lambda b, c, gi: (l, c, 0, 0)),
                  pl.BlockSpec((None, None, HEAD_DIM, HEAD_DIM), lambda b, c, gi: (l, c, 0, 0))],
        out_specs=pl.BlockSpec((None, None, None, n_half, HEAD_DIM),
                               lambda b, c, gi: (b, c, gi, 0, 0)),
        out_shape=jax.ShapeDtypeStruct((bsz, 2, g, n_half, HEAD_DIM), F32),
        compiler_params=_cparams("parallel", "parallel", "parallel"),
        name="nsa_compress",
    )(proj, cmp_pos, cmp_w1, cmp_w2)


def _rope_tile(x, cos, sin_signed):
    return x * cos + pltpu.roll(x, HEAD_DIM // 2, 1) * sin_signed


def _nsa_prep_kernel(q_ref, ks_ref, vs_ref, kw_ref, vw_ref, cos_ref, sin_ref,
                     qb_ref, qr_ref, kso_ref, vst_ref, kwo_ref, vwt_ref):
    cos = cos_ref[...]
    sin = sin_ref[...]
    for h in range(NSA_HEADS):
        sl = slice(h * HEAD_DIM, (h + 1) * HEAD_DIM)
        x = q_ref[:, sl]
        qb_ref[:, sl] = x.astype(BF16)
        qr_ref[:, sl] = (_rope_tile(x, cos, sin) * (SCALE * LOG2E)).astype(BF16)
    n_tiles, _, width = vst_ref.shape
    for g in range(NSA_KV_GROUPS):
        sl = slice(g * HEAD_DIM, (g + 1) * HEAD_DIM)
        kso_ref[:, sl] = _rope_tile(ks_ref[:, sl], cos, sin).astype(BF16)
        kwo_ref[:, sl] = _rope_tile(kw_ref[:, sl], cos, sin).astype(BF16)
        vs_t = jnp.transpose(vs_ref[:, sl]).astype(BF16)
        vw_t = jnp.transpose(vw_ref[:, sl]).astype(BF16)
        for c in range(n_tiles):
            vst_ref[c, sl, :] = vs_t[:, c * width:(c + 1) * width]
            vwt_ref[c, sl, :] = vw_t[:, c * width:(c + 1) * width]


def _nsa_prep(proj, cos, sin_signed, bsz, seq, q_off, kv_off):
    tr = NSA_TQ
    nsa_w = NSA_HEADS * HEAD_DIM
    gw = NSA_KV_GROUPS * HEAD_DIM
    nr = seq // tr
    t = bsz * seq
    qblk = q_off // nsa_w
    kvb = kv_off // gw
    n_tiles = tr // NSA_TK
    rowq = lambda w, cb: pl.BlockSpec((tr, w), lambda b, i: (b * nr + i, cb))
    tab = pl.BlockSpec((tr, HEAD_DIM), lambda b, i: (i, 0))
    vt_spec = pl.BlockSpec((None, n_tiles, gw, NSA_TK), lambda b, i: (b, i, 0, 0))
    vt_shape = jax.ShapeDtypeStruct((bsz, seq // NSA_TK, gw, NSA_TK), BF16)
    return pl.pallas_call(
        _nsa_prep_kernel,
        grid=(bsz, nr),
        in_specs=[rowq(nsa_w, qblk), rowq(gw, kvb + 2), rowq(gw, kvb + 3),
                  rowq(gw, kvb + 4), rowq(gw, kvb + 5), tab, tab],
        out_specs=[rowq(nsa_w, 0), rowq(nsa_w, 0), rowq(gw, 0), vt_spec, rowq(gw, 0), vt_spec],
        out_shape=[jax.ShapeDtypeStruct((t, nsa_w), BF16), jax.ShapeDtypeStruct((t, nsa_w), BF16),
                   jax.ShapeDtypeStruct((t, gw), BF16), vt_shape,
                   jax.ShapeDtypeStruct((t, gw), BF16), vt_shape],
        compiler_params=_cparams("parallel", "parallel"),
        name="nsa_prep",
    )(proj, proj, proj, proj, proj, cos, sin_signed)


def _col_reduce(x, op, ways=4):
    nblk = x.shape[0] // 8
    accs = [x[8 * k:8 * (k + 1)] for k in range(min(ways, nblk))]
    for k in range(len(accs), nblk):
        accs[k % ways] = op(accs[k % ways], x[8 * k:8 * (k + 1)])
    while len(accs) > 1:
        accs = [op(accs[a], accs[a + 1]) for a in range(0, len(accs), 2)]
    return accs[0]


def _col_max(x):
    return jnp.max(_col_reduce(x, jnp.maximum), axis=0, keepdims=True)


def _col_sum(x):
    return jnp.sum(_col_reduce(x, jnp.add), axis=0, keepdims=True)


def _nsa_kernel(qb_ref, qr_ref, kc_ref, vc_ref, ks_ref, vst_ref, kw_ref, vwt_ref,
                ng_ref, nz_ref, ovl_ref, exp_ref, o_ref,
                qa_ref, ts0_ref, ts1_ref, tw0_ref, tw1_ref, xs0_ref, xs1_ref, xw0_ref, xw1_ref,
                ms_ref, mw_ref, accs_ref, accw_ref, ocmp_ref, ngt_ref,
                *, tq, tk, n_slc):
    assert tq == 2 * tk and WINDOW == tq
    gi = pl.program_id(1)
    i = pl.program_id(2)
    q0 = i * tq
    nt = (((1,), (1,)), ((), ()))
    pos_row = q0 + lax.broadcasted_iota(jnp.int32, (1, tq), 1)
    pos_all = jnp.concatenate([pos_row] * NSA_REP, axis=1)
    heads = [slice(r * HEAD_DIM, (r + 1) * HEAD_DIM) for r in range(NSA_REP)]
    cols = [slice(r * tq, (r + 1) * tq) for r in range(NSA_REP)]

    qb = jnp.concatenate([qb_ref[:, h] for h in heads], axis=0)
    kc = kc_ref[...].astype(BF16)
    vc_t = jnp.transpose(vc_ref[...]).astype(BF16)
    n_cmp = kc.shape[0]
    cmp_end = CMP_STRIDE * lax.broadcasted_iota(jnp.int32, (n_cmp, 1), 0) + (CMP_BLOCK - 1)
    cmask = cmp_end <= pos_all
    s = lax.dot_general(kc, qb, nt, preferred_element_type=F32) * SCALE
    s = jnp.where(cmask, s, NEG)
    e = jnp.exp(s - _col_max(s))
    p = e * (1.0 / _col_sum(e))
    p = jnp.where(cmask, p, 0.0).astype(BF16)
    ocmp_ref[...] = jnp.dot(vc_t, p, preferred_element_type=F32)

    qa_ref[:, 0:HEAD_DIM] = jnp.concatenate([qr_ref[:, h] for h in heads], axis=0)
    n_top = min(SLC_TOP_N, n_slc)

    @pl.when((i + 1) * tq <= n_top * SLC_BLOCK)
    def _():
        qa_ref[:, HEAD_DIM:] = jnp.zeros((NSA_REP * tq, LANES), BF16)

    @pl.when((i + 1) * tq > n_top * SLC_BLOCK)
    def _():
        imp_all = jnp.dot(ovl_ref[...], p, preferred_element_type=F32)
        val = imp_all[:, cols[0]]
        for c in cols[1:]:
            val = val + imp_all[:, c]
        blk = lax.broadcasted_iota(jnp.int32, (n_slc, 1), 0)
        cur = pos_row // SLC_BLOCK
        future = blk * SLC_BLOCK > pos_row
        forced = (blk == 0) | (blk == cur) | (blk == cur - 1)
        val = jnp.where(future, -jnp.inf, jnp.where(forced, jnp.inf, val))
        rank = jnp.zeros((n_slc, tq), jnp.int32)
        for j in range(n_slc):
            vj = val[j:j + 1, :]
            beats = (vj > val) | ((vj == val) & (blk > j))
            rank = rank + beats.astype(jnp.int32)
        sel_bias = jnp.where(rank < n_top, 0.0, NEG)
        sel_bias = jnp.concatenate([sel_bias, jnp.zeros((LANES - n_slc, tq), F32)], axis=0)
        sel_bias = jnp.transpose(sel_bias).astype(BF16)
        qa_ref[:, HEAD_DIM:] = jnp.concatenate([sel_bias] * NSA_REP, axis=0)

    ones_rows = (lax.broadcasted_iota(jnp.int32, (VT_PAD, tk), 0) == 0).astype(BF16)

    def scores(t_ref, x_ref, keys, queries, visible):
        t = lax.dot_general(keys, queries, nt, preferred_element_type=F32)
        if visible is not None:
            t = jnp.where(visible, t, NEG)
        t_ref[...] = t
        x_ref[...] = _col_max(t)

    def absorb(t_ref, x_ref, vt, m_ref, acc_ref):
        m_old = m_ref[...]
        m_new = jnp.maximum(m_old, x_ref[...])
        alpha = jnp.exp2(m_old - m_new)
        e = jnp.exp2(t_ref[...] - m_new).astype(BF16)
        vt_ones = jnp.concatenate([vt, ones_rows], axis=0)
        acc_ref[...] = alpha * acc_ref[...] + jnp.dot(vt_ones, e, preferred_element_type=F32)
        m_ref[...] = m_new

    def key_pos(c):
        return c * tk + lax.broadcasted_iota(jnp.int32, (tk, 1), 0)

    def slc_scores(t_ref, x_ref, c, causal):
        start = pl.multiple_of(c * tk, tk)
        keys = jnp.concatenate([ks_ref[pl.ds(start, tk), :], exp_ref[c]], axis=1)
        scores(t_ref, x_ref, keys, qa_ref[...], (key_pos(c) <= pos_all) if causal else None)

    def slc_absorb(t_ref, x_ref, c):
        absorb(t_ref, x_ref, vst_ref[c], ms_ref, accs_ref)

    slc_scores(ts0_ref, xs0_ref, 2 * i, True)
    slc_scores(ts1_ref, xs1_ref, 2 * i + 1, True)
    ms_ref[...] = jnp.maximum(xs0_ref[...], xs1_ref[...])
    accs_ref[...] = jnp.zeros(accs_ref.shape, F32)
    slc_absorb(ts0_ref, xs0_ref, 2 * i)

    def slc_pair(j, carry):
        slc_scores(ts0_ref, xs0_ref, 2 * j, False)
        slc_absorb(ts1_ref, xs1_ref, jnp.where(j == 0, 2 * i + 1, 2 * j - 1))
        slc_scores(ts1_ref, xs1_ref, 2 * j + 1, False)
        slc_absorb(ts0_ref, xs0_ref, 2 * j)
        return carry

    lax.fori_loop(0, i, slc_pair, 0)
    slc_absorb(ts1_ref, xs1_ref, jnp.where(i == 0, 1, 2 * i - 1))

    low = jnp.maximum(pos_all - WINDOW, -1)

    def win_scores(t_ref, x_ref, c, causal):
        kpos = key_pos(c)
        start = pl.multiple_of(jnp.maximum(c, 0) * tk, tk)
        scores(t_ref, x_ref, kw_ref[pl.ds(start, tk), :], qa_ref[:, 0:HEAD_DIM],
               (kpos <= pos_all) if causal else (kpos > low))

    def win_absorb(t_ref, x_ref, c):
        absorb(t_ref, x_ref, vwt_ref[jnp.maximum(c, 0)], mw_ref, accw_ref)

    mw_ref[...] = jnp.full(mw_ref.shape, NEG, F32)
    accw_ref[...] = jnp.zeros(accw_ref.shape, F32)
    win_scores(tw0_ref, xw0_ref, 2 * i, True)
    win_scores(tw1_ref, xw1_ref, 2 * i + 1, True)
    win_absorb(tw0_ref, xw0_ref, 2 * i)
    win_scores(tw0_ref, xw0_ref, 2 * i - 1, False)
    win_absorb(tw1_ref, xw1_ref, 2 * i + 1)
    win_scores(tw1_ref, xw1_ref, 2 * i - 2, False)
    win_absorb(tw0_ref, xw0_ref, 2 * i - 1)
    win_absorb(tw1_ref, xw1_ref, 2 * i - 2)

    ngt_ref[...] = jnp.transpose(ng_ref[...])
    inv_ls = 1.0 / accs_ref[HEAD_DIM:HEAD_DIM + 1, :]
    inv_lw = 1.0 / accw_ref[HEAD_DIM:HEAD_DIM + 1, :]
    for r in range(NSA_REP):
        row = (gi * NSA_REP + r) * NSA_BRANCHES
        g_cmp = _sigmoid(ngt_ref[pl.ds(row, 1), :])
        g_slc = _sigmoid(ngt_ref[pl.ds(row + 1, 1), :])
        g_win = _sigmoid(ngt_ref[pl.ds(row + 2, 1), :])
        o_t = (g_cmp * ocmp_ref[:, cols[r]]
               + (g_slc * inv_ls[:, cols[r]]) * accs_ref[0:HEAD_DIM, cols[r]]
               + (g_win * inv_lw[:, cols[r]]) * accw_ref[0:HEAD_DIM, cols[r]])
        o_ref[:, heads[r]] = (jnp.transpose(o_t) * _silu(nz_ref[:, heads[r]])).astype(o_ref.dtype)


def _nsa_attention(qb, qr, cmpkv, ks, vst, kw, vwt, proj, ovl, expand, bsz, seq, ng_blk, nz_blk):
    tq, tk = NSA_TQ, NSA_TK
    g = NSA_KV_GROUPS
    gw = NSA_REP * HEAD_DIM
    nq = seq // tq
    n_slc = seq // SLC_BLOCK
    n_half = cmpkv.shape[-2]
    stacked = NSA_REP * tq
    rowq = lambda cb: pl.BlockSpec((tq, gw), lambda b, gi, i: (b * nq + i, cb + gi))
    kvs = pl.BlockSpec((seq, HEAD_DIM), lambda b, gi, i: (b, gi))
    cmps = lambda c: pl.BlockSpec((None, None, None, n_half, HEAD_DIM),
                                  lambda b, gi, i: (b, c, gi, 0, 0))
    vts = pl.BlockSpec((None, seq // tk, HEAD_DIM, tk), lambda b, gi, i: (b, 0, gi, 0))
    score_buf = pltpu.VMEM((tk, stacked), F32)
    stat_row = pltpu.VMEM((1, stacked), F32)
    flash_acc = pltpu.VMEM((HEAD_DIM + VT_PAD, stacked), F32)
    return pl.pallas_call(
        functools.partial(_nsa_kernel, tq=tq, tk=tk, n_slc=n_slc),
        grid=(bsz, g, nq),
        in_specs=[rowq(0), rowq(0), cmps(0), cmps(1), kvs, vts, kvs, vts,
                  pl.BlockSpec((tq, LANES), lambda b, gi, i: (b * nq + i, ng_blk)),
                  rowq(nz_blk),
                  pl.BlockSpec((n_slc, n_half), lambda b, gi, i: (0, 0)),
                  pl.BlockSpec((seq // tk, tk, LANES), lambda b, gi, i: (0, 0, 0))],
        out_specs=rowq(0),
        out_shape=jax.ShapeDtypeStruct((bsz * seq, NSA_HEADS * HEAD_DIM), BF16),
        scratch_shapes=[pltpu.VMEM((stacked, 2 * HEAD_DIM), BF16),
                        score_buf, score_buf, score_buf, score_buf,
                        stat_row, stat_row, stat_row, stat_row,
                        stat_row, stat_row,
                        flash_acc, flash_acc, pltpu.VMEM((HEAD_DIM, stacked), F32),
                        pltpu.VMEM((LANES, tq), F32)],
        compiler_params=_cparams("parallel", "parallel", "arbitrary"),
        name="nsa_attention",
    )(qb, qr, cmpkv, cmpkv, ks, vst, kw, vwt, proj, proj, ovl, expand)


def _merge_kernel(ya_ref, yb_ref, yx_ref, wa_ref, wb_ref, wx_ref, g0_ref, g1_ref, g2_ref, o_ref):
    wa = wa_ref[...].astype(BF16)
    wb = wb_ref[...].astype(BF16)
    wx = wx_ref[...].astype(BF16)
    tm = o_ref.shape[0]
    step = tm // MERGE_ROW_CHUNKS
    for r0 in range(0, tm, step):
        rows = slice(r0, r0 + step)

        def branch(y_ref, w, g_ref):
            up = jnp.dot(y_ref[rows, :], w, preferred_element_type=F32)
            return _sigmoid(g_ref[rows, :]) * up

        u = branch(ya_ref, wa, g0_ref) + branch(yb_ref, wb, g1_ref) + branch(yx_ref, wx, g2_ref)
        o_ref[rows, :] = u.astype(o_ref.dtype)


def _merge(ya, yb, yx, wa, wb, wx, proj, l, mg_off, d_model, tm=2048, tn=256):
    t = ya.shape[0]
    nj = d_model // tn
    mgb = mg_off // tn
    act = lambda w: pl.BlockSpec((tm, w), lambda i, j: (i, 0), pipeline_mode=pl.Buffered(1))
    wsp = lambda w: pl.BlockSpec((None, w, tn), lambda i, j: (l, 0, j))
    gsp = lambda br: pl.BlockSpec((tm, tn), lambda i, j: (i, mgb + br * nj + j))
    return pl.pallas_call(
        _merge_kernel,
        grid=(t // tm, nj),
        in_specs=[act(ya.shape[1]), act(yb.shape[1]), act(yx.shape[1]),
                  wsp(wa.shape[1]), wsp(wb.shape[1]), wsp(wx.shape[1]),
                  gsp(0), gsp(1), gsp(2)],
        out_specs=pl.BlockSpec((tm, tn), lambda i, j: (i, j)),
        out_shape=jax.ShapeDtypeStruct((t, d_model), BF16),
        compiler_params=_cparams("parallel", "arbitrary"),
        name="gated_merge",
    )(ya, yb, yx, wa, wb, wx, proj, proj, proj)


def _rope_tables(seq):
    half = HEAD_DIM // 2
    inv_freq = ROPE_THETA ** (-jnp.arange(half, dtype=F32) / half)
    ang = jnp.arange(seq, dtype=jnp.int32).astype(F32)[:, None] * inv_freq[None, :]
    cos, sin = jnp.cos(ang), jnp.sin(ang)
    return jnp.concatenate([cos, cos], axis=-1), jnp.concatenate([-sin, sin], axis=-1)


def _overlap_table(seq, n_half):
    n_slc = seq // SLC_BLOCK
    cmp_start = np.arange(n_half) * CMP_STRIDE
    slc_start = np.arange(n_slc) * SLC_BLOCK
    ovl = ((cmp_start[None, :] < slc_start[:, None] + SLC_BLOCK)
           & (cmp_start[None, :] + CMP_BLOCK > slc_start[:, None]))
    ovl = ovl & (cmp_start[None, :] + CMP_BLOCK <= seq)
    return jnp.asarray(ovl.astype(np.float32), BF16)


def _expand_table(seq, tk):
    key_blk = np.arange(seq) // SLC_BLOCK
    e = (key_blk[:, None] == np.arange(LANES)[None, :]).astype(np.float32)
    return jnp.asarray(e.reshape(seq // tk, tk, LANES), BF16)


def kernel(x, mem, norm_g, w_in, conv_w, conv_b, cmp_pos, cmp_w1, cmp_w2, mem_norm_g, w_mem_kv,
           w_up_a, w_up_b, w_up_x, w_out, final_g):
    bsz, seq, d_model = x.shape
    mem_len = mem.shape[1]
    depth = w_in.shape[0]
    sc_w = conv_w.shape[-1]
    nsa_w = NSA_HEADS * HEAD_DIM
    kv_w = NSA_BRANCHES * 2 * NSA_KV_GROUPS * HEAD_DIM
    ng_w = NSA_BRANCHES * NSA_HEADS
    x_w = X_HEADS * HEAD_DIM
    proj_tn = 512

    conv_cols = 4 * sc_w
    q_off = 0
    kv_off = q_off + nsa_w
    ng_off = kv_off + kv_w
    nz_off = ng_off + proj_tn
    xq_off = nz_off + nsa_w
    xz_off = xq_off + x_w
    mg_off = xz_off + x_w
    assert seq % NSA_TQ == 0 and seq // SLC_BLOCK <= LANES and sc_w % 256 == 0 and 0 < ng_w < LANES
    assert conv_cols + mg_off + N_BRANCHES * d_model - proj_tn + ng_w == w_in.shape[-1]

    w_in_t = jnp.swapaxes(w_in, 1, 2)
    w_memkv_b = w_mem_kv.astype(BF16)
    cmp_w1_b = cmp_w1.astype(BF16)
    cmp_w2_b = cmp_w2.astype(BF16)
    norm_g3 = norm_g.reshape(depth, 1, d_model)
    mem_norm_g3 = mem_norm_g.reshape(depth, 1, d_model)
    conv_b3 = conv_b.reshape(depth, 1, sc_w)

    cos, sin_signed = _rope_tables(seq)
    n_half = seq // CMP_STRIDE
    ovl = _overlap_table(seq, n_half)
    expand = _expand_table(seq, NSA_TK)

    t = bsz * seq
    xf = x.reshape(t, d_model)
    memf = mem.reshape(bsz * mem_len, d_model)

    hg, hr = _prenorm(xf, norm_g3, 0)
    for l in range(depth):
        y_a = _conv_branch(hg, hr, w_in_t, conv_w, conv_b3, l, seq, sc_w)
        proj = _in_proj(hg, hr, w_in_t, l, conv_cols, ng_off, ng_w, 2048, proj_tn)

        hm = _rmsnorm(memf, mem_norm_g3, l, BF16)
        memkv = _matmul(hm, w_memkv_b, l, F32, 512, 1024, name="mem_kv_proj")
        y_x = _mem_attention(proj, memkv, bsz, seq, mem_len, xq_off // x_w, xz_off // x_w)

        cmpkv = _compress(proj, cmp_pos, cmp_w1_b, cmp_w2_b, l, bsz, seq, kv_off // HEAD_DIM)
        qb, qr, ks, vst, kw, vwt = _nsa_prep(proj, cos, sin_signed, bsz, seq, q_off, kv_off)
        y_b = _nsa_attention(qb, qr, cmpkv, ks, vst, kw, vwt, proj, ovl, expand, bsz, seq,
                             ng_off // LANES, nz_off // (NSA_REP * HEAD_DIM))

        u = _merge(y_a, y_b, y_x, w_up_a, w_up_b, w_up_x, proj, l, mg_off, d_model)
        if l + 1 < depth:
            xf, hg, hr = _out_proj(u, w_out, l, xf, 2048, 256, next_gain=norm_g3)
        else:
            xf = _out_proj(u, w_out, l, xf, 2048, 512)

    out = _rmsnorm(xf, final_g.reshape(1, 1, d_model), 0, F32)
    return out.reshape(bsz, seq, d_model)
```

```python
import functools

import numpy as np
import jax
import jax.numpy as jnp
from jax import lax
from jax.experimental import pallas as pl
from jax.experimental.pallas import tpu as pltpu

F32 = jnp.float32
BF16 = jnp.bfloat16

HEAD_DIM = 128
ROPE_THETA = 10000.0
EPS = 1e-6
NEG = -1e30
SC_KERNEL = 3
NSA_HEADS = 16
NSA_KV_GROUPS = 4
NSA_REP = NSA_HEADS // NSA_KV_GROUPS
NSA_BRANCHES = 3
CMP_BLOCK = 32
CMP_STRIDE = 16
SLC_BLOCK = 64
SLC_TOP_N = 16
WINDOW = 512
X_HEADS = 4
N_BRANCHES = 3
SCALE = HEAD_DIM ** -0.5
LOG2E = 1.4426950408889634

V7X_VMEM_BYTES = 64 * 1024 * 1024
VMEM_LIMIT_BYTES = V7X_VMEM_BYTES - 8 * 1024 * 1024
LANES = 128

NSA_TQ = 512
NSA_TK = 256
VT_PAD = 16
MERGE_ROW_CHUNKS = 2
OUT_ROW_CHUNKS = 2
CONV_ROW_CHUNKS = 4


def _cparams(*sem):
    return pltpu.CompilerParams(dimension_semantics=sem, vmem_limit_bytes=VMEM_LIMIT_BYTES)


def _resident_rows_spec(tm, kdim):
    big = tm * kdim * 2 >= V7X_VMEM_BYTES // 4
    return pl.BlockSpec((tm, kdim), lambda i, j: (i, 0),
                        pipeline_mode=pl.Buffered(1) if big else None)


def _sigmoid(x):
    return 0.5 * jnp.tanh(0.5 * x) + 0.5


def _silu(x):
    return x * _sigmoid(x)


def _rmsnorm_kernel(x_ref, g_ref, o_ref):
    x = x_ref[...]
    ms = jnp.mean(x * x, axis=-1, keepdims=True)
    o_ref[...] = (x * lax.rsqrt(ms + EPS) * g_ref[...]).astype(o_ref.dtype)


def _rmsnorm(x, g3, l, out_dtype, tm=256):
    m, d = x.shape
    return pl.pallas_call(
        _rmsnorm_kernel,
        grid=(m // tm,),
        in_specs=[pl.BlockSpec((tm, d), lambda i: (i, 0)),
                  pl.BlockSpec((None, 1, d), lambda i: (l, 0, 0))],
        out_specs=pl.BlockSpec((tm, d), lambda i: (i, 0)),
        out_shape=jax.ShapeDtypeStruct((m, d), out_dtype),
        compiler_params=_cparams("parallel"),
        name="rmsnorm",
    )(x, g3)


def _prenorm_kernel(x_ref, g_ref, a_ref, r_ref):
    x = x_ref[...]
    r_ref[...] = lax.rsqrt(jnp.mean(x * x, axis=-1, keepdims=True) + EPS)
    a_ref[...] = (x * g_ref[...]).astype(a_ref.dtype)


def _prenorm(x, g3, l, tm=256):
    m, d = x.shape
    return pl.pallas_call(
        _prenorm_kernel,
        grid=(m // tm,),
        in_specs=[pl.BlockSpec((tm, d), lambda i: (i, 0)),
                  pl.BlockSpec((None, 1, d), lambda i: (l, 0, 0))],
        out_specs=[pl.BlockSpec((tm, d), lambda i: (i, 0)), pl.BlockSpec((tm, 1), lambda i: (i, 0))],
        out_shape=[jax.ShapeDtypeStruct((m, d), BF16), jax.ShapeDtypeStruct((m, 1), F32)],
        compiler_params=_cparams("parallel"),
        name="prenorm",
    )(x, g3)


def _mm_kernel(a_ref, b_ref, o_ref):
    o_ref[...] = jnp.dot(a_ref[...], b_ref[...], preferred_element_type=F32).astype(o_ref.dtype)


def _matmul(a, b, l, out_dtype, tm, tn, name):
    m, kdim = a.shape
    n = b.shape[-1]
    tm, tn = min(tm, m), min(tn, n)
    return pl.pallas_call(
        _mm_kernel,
        grid=(m // tm, n // tn),
        in_specs=[pl.BlockSpec((tm, kdim), lambda i, j: (i, 0)),
                  pl.BlockSpec((None, kdim, tn), lambda i, j: (l, 0, j))],
        out_specs=pl.BlockSpec((tm, tn), lambda i, j: (i, j)),
        out_shape=jax.ShapeDtypeStruct((m, n), out_dtype),
        compiler_params=_cparams("parallel", "arbitrary"),
        name=name,
    )(a, b)


def _in_proj_kernel(a_ref, w_ref, r_ref, o_ref):
    acc = lax.dot_general(a_ref[...], w_ref[0].astype(BF16), (((1,), (1,)), ((), ())),
                          preferred_element_type=F32)
    o_ref[...] = (acc * r_ref[...]).astype(o_ref.dtype)


def _in_proj(a, row_scale, wt, l, first, lo, gap, tm, tn):
    m, kdim = a.shape
    lo_tiles = lo // tn
    n = wt.shape[1] - first - gap + tn
    assert lo % tn == 0 and n % tn == 0 and gap % 16 == 0 and gap < tn and first % 16 == 0

    def w_index(i, j):
        row = first + jnp.where(j <= lo_tiles, j * tn, (j - 1) * tn + gap)
        return l, pl.multiple_of(row, 16), 0

    return pl.pallas_call(
        _in_proj_kernel,
        grid=(m // tm, n // tn),
        in_specs=[_resident_rows_spec(tm, kdim),
                  pl.BlockSpec((pl.Element(1), pl.Element(tn), pl.Element(kdim)), w_index),
                  pl.BlockSpec((tm, 1), lambda i, j: (i, 0))],
        out_specs=pl.BlockSpec((tm, tn), lambda i, j: (i, j)),
        out_shape=jax.ShapeDtypeStruct((m, n), F32),
        compiler_params=_cparams("parallel", "arbitrary"),
        name="in_proj",
    )(a, wt, row_scale)


def _out_proj_kernel(a_ref, w_ref, res_ref, o_ref):
    w = w_ref[...].astype(BF16)
    tm = o_ref.shape[0]
    step = tm // OUT_ROW_CHUNKS
    for r0 in range(0, tm, step):
        rows = slice(r0, r0 + step)
        acc = jnp.dot(a_ref[rows, :], w, preferred_element_type=F32)
        o_ref[rows, :] = res_ref[rows, :] + acc


def _out_proj_next_kernel(a_ref, w_ref, res_ref, g_ref, o_ref, an_ref, rn_ref, *, d_model):
    j = pl.program_id(1)
    last = pl.num_programs(1) - 1

    @pl.when(j == 0)
    def _():
        rn_ref[...] = jnp.zeros(rn_ref.shape, F32)

    w = w_ref[...].astype(BF16)
    g = g_ref[...]
    tm = o_ref.shape[0]
    step = tm // OUT_ROW_CHUNKS
    for r0 in range(0, tm, step):
        rows = slice(r0, r0 + step)
        xn = res_ref[rows, :] + jnp.dot(a_ref[rows, :], w, preferred_element_type=F32)
        o_ref[rows, :] = xn
        an_ref[rows, :] = (xn * g).astype(an_ref.dtype)
        ss = rn_ref[rows, :] + jnp.sum(xn * xn, axis=-1, keepdims=True)
        rn_ref[rows, :] = jnp.where(j == last, lax.rsqrt(ss * (1.0 / d_model) + EPS), ss)


def _out_proj(a, w, l, residual, tm, tn, next_gain=None):
    m, kdim = a.shape
    n = w.shape[-1]
    tile = pl.BlockSpec((tm, tn), lambda i, j: (i, j))
    in_specs = [_resident_rows_spec(tm, kdim),
                pl.BlockSpec((None, kdim, tn), lambda i, j: (l, 0, j)), tile]
    if next_gain is None:
        return pl.pallas_call(
            _out_proj_kernel,
            grid=(m // tm, n // tn),
            in_specs=in_specs,
            out_specs=tile,
            out_shape=jax.ShapeDtypeStruct((m, n), F32),
            compiler_params=_cparams("parallel", "arbitrary"),
            name="out_proj",
        )(a, w, residual)
    return pl.pallas_call(
        functools.partial(_out_proj_next_kernel, d_model=n),
        grid=(m // tm, n // tn),
        in_specs=in_specs + [pl.BlockSpec((None, 1, tn), lambda i, j: (l + 1, 0, j))],
        out_specs=[tile, tile, pl.BlockSpec((tm, 1), lambda i, j: (i, 0))],
        out_shape=[jax.ShapeDtypeStruct((m, n), F32), jax.ShapeDtypeStruct((m, n), BF16),
                   jax.ShapeDtypeStruct((m, 1), F32)],
        compiler_params=_cparams("parallel", "arbitrary"),
        name="out_proj_next",
    )(a, w, residual, next_gain)


def _conv_proj_kernel(a_ref, wh_ref, wb_ref, wc_ref, wz_ref, r_ref, cw_ref, cb_ref, o_ref):
    w = jnp.concatenate([wh_ref[...], wb_ref[...], wc_ref[...], wz_ref[...]], axis=0).astype(BF16)
    j = pl.program_id(1)
    cw = [cw_ref[k, pl.ds(j, 1), :] for k in range(SC_KERNEL)]
    cb = cb_ref[pl.ds(j, 1), :]
    seq, tc = o_ref.shape
    step = seq // CONV_ROW_CHUNKS
    row = lax.broadcasted_iota(jnp.int32, (step, tc), 0)
    u_prev = jnp.zeros((step, tc), F32)
    for r0 in range(0, seq, step):
        p = lax.dot_general(a_ref[r0:r0 + step, :], w, (((1,), (1,)), ((), ())),
                            preferred_element_type=F32) * r_ref[r0:r0 + step, :]
        h, b, c, z = (p[:, g * tc:(g + 1) * tc] for g in range(4))
        u = c * h
        u1 = jnp.where(row >= 1, pltpu.roll(u, 1, 0), pltpu.roll(u_prev, 1, 0))
        u2 = jnp.where(row >= 2, pltpu.roll(u, 2, 0), pltpu.roll(u_prev, 2, 0))
        y = cb + cw[0] * u2
        y = y + cw[1] * u1
        y = y + cw[2] * u
        o_ref[r0:r0 + step, :] = (b * y * _silu(z)).astype(o_ref.dtype)
        u_prev = u


def _conv_branch(a, row_scale, wt, conv_w, conv_b, l, seq, tc=LANES):
    m, kdim = a.shape
    depth, taps, width = conv_w.shape
    nb = width // tc
    conv_w = conv_w.reshape(depth, taps, nb, tc)
    conv_b = conv_b.reshape(depth, nb, tc)
    wsp = lambda g: pl.BlockSpec((None, tc, kdim), lambda b, j: (l, g * nb + j, 0))
    return pl.pallas_call(
        _conv_proj_kernel,
        grid=(m // seq, nb),
        in_specs=[_resident_rows_spec(seq, kdim), wsp(0), wsp(1), wsp(2), wsp(3),
                  pl.BlockSpec((seq, 1), lambda b, j: (b, 0)),
                  pl.BlockSpec((None, taps, nb, tc), lambda b, j: (l, 0, 0, 0)),
                  pl.BlockSpec((None, nb, tc), lambda b, j: (l, 0, 0))],
        out_specs=pl.BlockSpec((seq, tc), lambda b, j: (b, j)),
        out_shape=jax.ShapeDtypeStruct((m, width), BF16),
        compiler_params=_cparams("parallel", "arbitrary"),
        name="conv_branch",
    )(a, wt, wt, wt, wt, row_scale, conv_w, conv_b)


def _memattn_kernel(q_ref, k_ref, v_ref, z_ref, o_ref):
    for h in range(X_HEADS):
        sl = slice(h * HEAD_DIM, (h + 1) * HEAD_DIM)
        q = q_ref[:, sl].astype(BF16)
        k = k_ref[:, sl].astype(BF16)
        v = v_ref[:, sl].astype(BF16)
        s = lax.dot_general(q, k, (((1,), (1,)), ((), ())), preferred_element_type=F32) * SCALE
        e = jnp.exp(s - jnp.max(s, axis=-1, keepdims=True))
        p = e / jnp.sum(e, axis=-1, keepdims=True)
        o = jnp.dot(p.astype(BF16), v, preferred_element_type=F32)
        o_ref[:, sl] = (o * _silu(z_ref[:, sl])).astype(o_ref.dtype)


def _mem_attention(proj, memkv, bsz, seq, mem_len, q_blk, z_blk, tq=512):
    nq = seq // tq
    xw = X_HEADS * HEAD_DIM
    row = lambda cb: pl.BlockSpec((tq, xw), lambda b, i: (b * nq + i, cb))
    return pl.pallas_call(
        _memattn_kernel,
        grid=(bsz, nq),
        in_specs=[row(q_blk),
                  pl.BlockSpec((mem_len, xw), lambda b, i: (b, 0)),
                  pl.BlockSpec((mem_len, xw), lambda b, i: (b, 1)),
                  row(z_blk)],
        out_specs=row(0),
        out_shape=jax.ShapeDtypeStruct((bsz * seq, xw), BF16),
        compiler_params=_cparams("parallel", "parallel"),
        name="mem_attention",
    )(proj, memkv, memkv, proj)


def _compress_kernel(*refs, n_half):
    *kv_refs, pos_ref, w1_ref, w2_ref, o_ref = refs
    half = CMP_BLOCK // 2
    groups = len(kv_refs)
    acc_a = jnp.zeros((groups * n_half, HEAD_DIM), F32)
    acc_b = jnp.zeros((groups * n_half, HEAD_DIM), F32)
    for t in range(half):
        x = jnp.concatenate([ref[pl.ds(t, n_half, stride=half), :] for ref in kv_refs], axis=0)
        xa = (x + pos_ref[t:t + 1, :]).astype(BF16)
        xb = (x + pos_ref[half + t:half + t + 1, :]).astype(BF16)
        acc_a = acc_a + jnp.dot(xa, w1_ref[t * HEAD_DIM:(t + 1) * HEAD_DIM, :],
                                preferred_element_type=F32)
        acc_b = acc_b + jnp.dot(xb, w1_ref[(half + t) * HEAD_DIM:(half + t + 1) * HEAD_DIM, :],
                                preferred_element_type=F32)
    w2 = w2_ref[...]
    for g in range(groups):
        rows = slice(g * n_half, (g + 1) * n_half)
        hid = _silu(acc_a[rows] + pltpu.roll(acc_b[rows], n_half - 1, 0))
        o_ref[g] = jnp.dot(hid.astype(BF16), w2, preferred_element_type=F32)


def _compress(proj, cmp_pos, cmp_w1, cmp_w2, l, bsz, seq, kv_off):
    g = NSA_KV_GROUPS
    kv_blk = kv_off // HEAD_DIM
    n_half = seq // (CMP_BLOCK // 2)
    kv_spec = lambda gi: pl.BlockSpec((seq, HEAD_DIM), lambda b, c: (b, kv_blk + c * g + gi))
    return pl.pallas_call(
        functools.partial(_compress_kernel, n_half=n_half),
        grid=(bsz, 2),
        in_specs=[kv_spec(gi) for gi in range(g)] + [
                  pl.BlockSpec((None, None, CMP_BLOCK, HEAD_DIM), lambda b, c: (l, c, 0, 0)),
                  pl.BlockSpec((None, None, CMP_BLOCK * HEAD_DIM, HEAD_DIM),
                               lambda b, c: (l, c, 0, 0)),
                  pl.BlockSpec((None, None, HEAD_DIM, HEAD_DIM), lambda b, c: (l, c, 0, 0))],
        out_specs=pl.BlockSpec((None, None, g, n_half, HEAD_DIM), lambda b, c: (b, c, 0, 0, 0)),
        out_shape=jax.ShapeDtypeStruct((bsz, 2, g, n_half, HEAD_DIM), F32),
        compiler_params=_cparams("parallel", "parallel"),
        name="nsa_compress",
    )(*([proj] * g), cmp_pos, cmp_w1, cmp_w2)


def _rope_tile(x, cos, sin_signed):
    return x * cos + pltpu.roll(x, HEAD_DIM // 2, 1) * sin_signed


def _nsa_prep_kernel(q_ref, ks_ref, vs_ref, kw_ref, vw_ref, cos_ref, sin_ref,
                     qb_ref, qr_ref, kso_ref, vst_ref, kwo_ref, vwt_ref):
    cos = cos_ref[...]
    sin = sin_ref[...]
    for h in range(NSA_HEADS):
        sl = slice(h * HEAD_DIM, (h + 1) * HEAD_DIM)
        x = q_ref[:, sl]
        qb_ref[:, sl] = x.astype(BF16)
        qr_ref[:, sl] = (_rope_tile(x, cos, sin) * (SCALE * LOG2E)).astype(BF16)
    n_tiles, _, width = vst_ref.shape
    for g in range(NSA_KV_GROUPS):
        sl = slice(g * HEAD_DIM, (g + 1) * HEAD_DIM)
        kso_ref[:, sl] = _rope_tile(ks_ref[:, sl], cos, sin).astype(BF16)
        kwo_ref[:, sl] = _rope_tile(kw_ref[:, sl], cos, sin).astype(BF16)
        vs_t = jnp.transpose(vs_ref[:, sl]).astype(BF16)
        vw_t = jnp.transpose(vw_ref[:, sl]).astype(BF16)
        for c in range(n_tiles):
            vst_ref[c, sl, :] = vs_t[:, c * width:(c + 1) * width]
            vwt_ref[c, sl, :] = vw_t[:, c * width:(c + 1) * width]


def _nsa_prep(proj, cos, sin_signed, bsz, seq, q_off, kv_off):
    tr = NSA_TQ
    nsa_w = NSA_HEADS * HEAD_DIM
    gw = NSA_KV_GROUPS * HEAD_DIM
    nr = seq // tr
    t = bsz * seq
    qblk = q_off // nsa_w
    kvb = kv_off // gw
    n_tiles = tr // NSA_TK
    rowq = lambda w, cb: pl.BlockSpec((tr, w), lambda b, i: (b * nr + i, cb))
    tab = pl.BlockSpec((tr, HEAD_DIM), lambda b, i: (i, 0))
    vt_spec = pl.BlockSpec((None, n_tiles, gw, NSA_TK), lambda b, i: (b, i, 0, 0))
    vt_shape = jax.ShapeDtypeStruct((bsz, seq // NSA_TK, gw, NSA_TK), BF16)
    return pl.pallas_call(
        _nsa_prep_kernel,
        grid=(bsz, nr),
        in_specs=[rowq(nsa_w, qblk), rowq(gw, kvb + 2), rowq(gw, kvb + 3),
                  rowq(gw, kvb + 4), rowq(gw, kvb + 5), tab, tab],
        out_specs=[rowq(nsa_w, 0), rowq(nsa_w, 0), rowq(gw, 0), vt_spec, rowq(gw, 0), vt_spec],
        out_shape=[jax.ShapeDtypeStruct((t, nsa_w), BF16), jax.ShapeDtypeStruct((t, nsa_w), BF16),
                   jax.ShapeDtypeStruct((t, gw), BF16), vt_shape,
                   jax.ShapeDtypeStruct((t, gw), BF16), vt_shape],
        compiler_params=_cparams("parallel", "parallel"),
        name="nsa_prep",
    )(proj, proj, proj, proj, proj, cos, sin_signed)


def _col_reduce(x, op, ways=4):
    nblk = x.shape[0] // 8
    accs = [x[8 * k:8 * (k + 1)] for k in range(min(ways, nblk))]
    for k in range(len(accs), nblk):
        accs[k % ways] = op(accs[k % ways], x[8 * k:8 * (k + 1)])
    while len(accs) > 1:
        accs = [op(accs[a], accs[a + 1]) for a in range(0, len(accs), 2)]
    return accs[0]


def _col_max(x):
    return jnp.max(_col_reduce(x, jnp.maximum), axis=0, keepdims=True)


def _col_sum(x):
    return jnp.sum(_col_reduce(x, jnp.add), axis=0, keepdims=True)


def _nsa_kernel(qb_ref, qr_ref, kc_ref, vc_ref, ks_ref, vst_ref, kw_ref, vwt_ref,
                ng_ref, nz_ref, ovl_ref, exp_ref, o_ref,
                qa_ref, ts0_ref, ts1_ref, tw0_ref, tw1_ref, xs0_ref, xs1_ref, xw0_ref, xw1_ref,
                ms_ref, mw_ref, accs_ref, accw_ref, ocmp_ref, ngt_ref,
                *, tq, tk, n_slc):
    assert tq == 2 * tk and WINDOW == tq
    gi = pl.program_id(1)
    i = pl.program_id(2)
    q0 = i * tq
    nt = (((1,), (1,)), ((), ()))
    pos_row = q0 + lax.broadcasted_iota(jnp.int32, (1, tq), 1)
    pos_all = jnp.concatenate([pos_row] * NSA_REP, axis=1)
    heads = [slice(r * HEAD_DIM, (r + 1) * HEAD_DIM) for r in range(NSA_REP)]
    cols = [slice(r * tq, (r + 1) * tq) for r in range(NSA_REP)]

    qb = jnp.concatenate([qb_ref[:, h] for h in heads], axis=0)
    kc = kc_ref[...].astype(BF16)
    vc_t = jnp.transpose(vc_ref[...]).astype(BF16)
    n_cmp = kc.shape[0]
    cmp_end = CMP_STRIDE * lax.broadcasted_iota(jnp.int32, (n_cmp, 1), 0) + (CMP_BLOCK - 1)
    cmask = cmp_end <= pos_all
    s = lax.dot_general(kc, qb, nt, preferred_element_type=F32) * SCALE
    s = jnp.where(cmask, s, NEG)
    e = jnp.exp(s - _col_max(s))
    p = e * (1.0 / _col_sum(e))
    p = jnp.where(cmask, p, 0.0).astype(BF16)
    ocmp_ref[...] = jnp.dot(vc_t, p, preferred_element_type=F32)

    qa_ref[:, 0:HEAD_DIM] = jnp.concatenate([qr_ref[:, h] for h in heads], axis=0)
    n_top = min(SLC_TOP_N, n_slc)

    @pl.when((i + 1) * tq <= n_top * SLC_BLOCK)
    def _():
        qa_ref[:, HEAD_DIM:] = jnp.zeros((NSA_REP * tq, LANES), BF16)

    @pl.when((i + 1) * tq > n_top * SLC_BLOCK)
    def _():
        imp_all = jnp.dot(ovl_ref[...], p, preferred_element_type=F32)
        val = imp_all[:, cols[0]]
        for c in cols[1:]:
            val = val + imp_all[:, c]
        blk = lax.broadcasted_iota(jnp.int32, (n_slc, 1), 0)
        cur = pos_row // SLC_BLOCK
        future = blk * SLC_BLOCK > pos_row
        forced = (blk == 0) | (blk == cur) | (blk == cur - 1)
        val = jnp.where(future, -jnp.inf, jnp.where(forced, jnp.inf, val))
        rank = jnp.zeros((n_slc, tq), jnp.int32)
        for j in range(n_slc):
            vj = val[j:j + 1, :]
            beats = (vj > val) | ((vj == val) & (blk > j))
            rank = rank + beats.astype(jnp.int32)
        sel_bias = jnp.where(rank < n_top, 0.0, NEG)
        sel_bias = jnp.concatenate([sel_bias, jnp.zeros((LANES - n_slc, tq), F32)], axis=0)
        sel_bias = jnp.transpose(sel_bias).astype(BF16)
        qa_ref[:, HEAD_DIM:] = jnp.concatenate([sel_bias] * NSA_REP, axis=0)

    ones_rows = (lax.broadcasted_iota(jnp.int32, (VT_PAD, tk), 0) == 0).astype(BF16)

    def scores(t_ref, x_ref, keys, queries, visible):
        t = lax.dot_general(keys, queries, nt, preferred_element_type=F32)
        if visible is not None:
            t = jnp.where(visible, t, NEG)
        t_ref[...] = t
        x_ref[...] = _col_max(t)

    def absorb(t_ref, x_ref, vt, m_ref, acc_ref):
        m_old = m_ref[...]
        m_new = jnp.maximum(m_old, x_ref[...])
        alpha = jnp.exp2(m_old - m_new)
        e = jnp.exp2(t_ref[...] - m_new).astype(BF16)
        vt_ones = jnp.concatenate([vt, ones_rows], axis=0)
        acc_ref[...] = alpha * acc_ref[...] + jnp.dot(vt_ones, e, preferred_element_type=F32)
        m_ref[...] = m_new

    def key_pos(c):
        return c * tk + lax.broadcasted_iota(jnp.int32, (tk, 1), 0)

    def slc_scores(t_ref, x_ref, c, causal):
        start = pl.multiple_of(c * tk, tk)
        keys = jnp.concatenate([ks_ref[pl.ds(start, tk), :], exp_ref[c]], axis=1)
        scores(t_ref, x_ref, keys, qa_ref[...], (key_pos(c) <= pos_all) if causal else None)

    def slc_absorb(t_ref, x_ref, c):
        absorb(t_ref, x_ref, vst_ref[c], ms_ref, accs_ref)

    slc_scores(ts0_ref, xs0_ref, 2 * i, True)
    slc_scores(ts1_ref, xs1_ref, 2 * i + 1, True)
    ms_ref[...] = jnp.maximum(xs0_ref[...], xs1_ref[...])
    accs_ref[...] = jnp.zeros(accs_ref.shape, F32)
    slc_absorb(ts0_ref, xs0_ref, 2 * i)

    def slc_pair(j, carry):
        slc_scores(ts0_ref, xs0_ref, 2 * j, False)
        slc_absorb(ts1_ref, xs1_ref, jnp.where(j == 0, 2 * i + 1, 2 * j - 1))
        slc_scores(ts1_ref, xs1_ref, 2 * j + 1, False)
        slc_absorb(ts0_ref, xs0_ref, 2 * j)
        return carry

    lax.fori_loop(0, i, slc_pair, 0)
    slc_absorb(ts1_ref, xs1_ref, jnp.where(i == 0, 1, 2 * i - 1))

    low = jnp.maximum(pos_all - WINDOW, -1)

    def win_scores(t_ref, x_ref, c, causal):
        kpos = key_pos(c)
        start = pl.multiple_of(jnp.maximum(c, 0) * tk, tk)
        scores(t_ref, x_ref, kw_ref[pl.ds(start, tk), :], qa_ref[:, 0:HEAD_DIM],
               (kpos <= pos_all) if causal else (kpos > low))

    def win_absorb(t_ref, x_ref, c):
        absorb(t_ref, x_ref, vwt_ref[jnp.maximum(c, 0)], mw_ref, accw_ref)

    mw_ref[...] = jnp.full(mw_ref.shape, NEG, F32)
    accw_ref[...] = jnp.zeros(accw_ref.shape, F32)
    win_scores(tw0_ref, xw0_ref, 2 * i, True)
    win_scores(tw1_ref, xw1_ref, 2 * i + 1, True)
    win_absorb(tw0_ref, xw0_ref, 2 * i)
    win_scores(tw0_ref, xw0_ref, 2 * i - 1, False)
    win_absorb(tw1_ref, xw1_ref, 2 * i + 1)
    win_scores(tw1_ref, xw1_ref, 2 * i - 2, False)
    win_absorb(tw0_ref, xw0_ref, 2 * i - 1)
    win_absorb(tw1_ref, xw1_ref, 2 * i - 2)

    ngt_ref[...] = jnp.transpose(ng_ref[...])
    inv_ls = 1.0 / accs_ref[HEAD_DIM:HEAD_DIM + 1, :]
    inv_lw = 1.0 / accw_ref[HEAD_DIM:HEAD_DIM + 1, :]
    for r in range(NSA_REP):
        row = (gi * NSA_REP + r) * NSA_BRANCHES
        g_cmp = _sigmoid(ngt_ref[pl.ds(row, 1), :])
        g_slc = _sigmoid(ngt_ref[pl.ds(row + 1, 1), :])
        g_win = _sigmoid(ngt_ref[pl.ds(row + 2, 1), :])
        o_t = (g_cmp * ocmp_ref[:, cols[r]]
               + (g_slc * inv_ls[:, cols[r]]) * accs_ref[0:HEAD_DIM, cols[r]]
               + (g_win * inv_lw[:, cols[r]]) * accw_ref[0:HEAD_DIM, cols[r]])
        o_ref[:, heads[r]] = (jnp.transpose(o_t) * _silu(nz_ref[:, heads[r]])).astype(o_ref.dtype)


def _nsa_attention(qb, qr, cmpkv, ks, vst, kw, vwt, proj, ovl, expand, bsz, seq, ng_blk, nz_blk):
    tq, tk = NSA_TQ, NSA_TK
    g = NSA_KV_GROUPS
    gw = NSA_REP * HEAD_DIM
    nq = seq // tq
    n_slc = seq // SLC_BLOCK
    n_half = cmpkv.shape[-2]
    stacked = NSA_REP * tq
    rowq = lambda cb: pl.BlockSpec((tq, gw), lambda b, gi, i: (b * nq + i, cb + gi))
    kvs = pl.BlockSpec((seq, HEAD_DIM), lambda b, gi, i: (b, gi))
    cmps = lambda c: pl.BlockSpec((None, None, None, n_half, HEAD_DIM),
                                  lambda b, gi, i: (b, c, gi, 0, 0))
    vts = pl.BlockSpec((None, seq // tk, HEAD_DIM, tk), lambda b, gi, i: (b, 0, gi, 0))
    score_buf = pltpu.VMEM((tk, stacked), F32)
    stat_row = pltpu.VMEM((1, stacked), F32)
    flash_acc = pltpu.VMEM((HEAD_DIM + VT_PAD, stacked), F32)
    return pl.pallas_call(
        functools.partial(_nsa_kernel, tq=tq, tk=tk, n_slc=n_slc),
        grid=(bsz, g, nq),
        in_specs=[rowq(0), rowq(0), cmps(0), cmps(1), kvs, vts, kvs, vts,
                  pl.BlockSpec((tq, LANES), lambda b, gi, i: (b * nq + i, ng_blk)),
                  rowq(nz_blk),
                  pl.BlockSpec((n_slc, n_half), lambda b, gi, i: (0, 0)),
                  pl.BlockSpec((seq // tk, tk, LANES), lambda b, gi, i: (0, 0, 0))],
        out_specs=rowq(0),
        out_shape=jax.ShapeDtypeStruct((bsz * seq, NSA_HEADS * HEAD_DIM), BF16),
        scratch_shapes=[pltpu.VMEM((stacked, 2 * HEAD_DIM), BF16),
                        score_buf, score_buf, score_buf, score_buf,
                        stat_row, stat_row, stat_row, stat_row,
                        stat_row, stat_row,
                        flash_acc, flash_acc, pltpu.VMEM((HEAD_DIM, stacked), F32),
                        pltpu.VMEM((LANES, tq), F32)],
        compiler_params=_cparams("parallel", "parallel", "arbitrary"),
        name="nsa_attention",
    )(qb, qr, cmpkv, cmpkv, ks, vst, kw, vwt, proj, proj, ovl, expand)


def _merge_kernel(ya_ref, yb_ref, yx_ref, wa_ref, wb_ref, wx_ref, g0_ref, g1_ref, g2_ref, o_ref):
    wa = wa_ref[...].astype(BF16)
    wb = wb_ref[...].astype(BF16)
    wx = wx_ref[...].astype(BF16)
    tm = o_ref.shape[0]
    step = tm // MERGE_ROW_CHUNKS
    for r0 in range(0, tm, step):
        rows = slice(r0, r0 + step)

        def branch(y_ref, w, g_ref):
            up = jnp.dot(y_ref[rows, :], w, preferred_element_type=F32)
            return _sigmoid(g_ref[rows, :]) * up

        u = branch(ya_ref, wa, g0_ref) + branch(yb_ref, wb, g1_ref) + branch(yx_ref, wx, g2_ref)
        o_ref[rows, :] = u.astype(o_ref.dtype)


def _merge(ya, yb, yx, wa, wb, wx, proj, l, mg_off, d_model, tm=2048, tn=256):
    t = ya.shape[0]
    nj = d_model // tn
    mgb = mg_off // tn
    act = lambda w: pl.BlockSpec((tm, w), lambda i, j: (i, 0), pipeline_mode=pl.Buffered(1))
    wsp = lambda w: pl.BlockSpec((None, w, tn), lambda i, j: (l, 0, j))
    gsp = lambda br: pl.BlockSpec((tm, tn), lambda i, j: (i, mgb + br * nj + j))
    return pl.pallas_call(
        _merge_kernel,
        grid=(t // tm, nj),
        in_specs=[act(ya.shape[1]), act(yb.shape[1]), act(yx.shape[1]),
                  wsp(wa.shape[1]), wsp(wb.shape[1]), wsp(wx.shape[1]),
                  gsp(0), gsp(1), gsp(2)],
        out_specs=pl.BlockSpec((tm, tn), lambda i, j: (i, j)),
        out_shape=jax.ShapeDtypeStruct((t, d_model), BF16),
        compiler_params=_cparams("parallel", "arbitrary"),
        name="gated_merge",
    )(ya, yb, yx, wa, wb, wx, proj, proj, proj)


def _rope_tables(seq):
    half = HEAD_DIM // 2
    inv_freq = ROPE_THETA ** (-jnp.arange(half, dtype=F32) / half)
    ang = jnp.arange(seq, dtype=jnp.int32).astype(F32)[:, None] * inv_freq[None, :]
    cos, sin = jnp.cos(ang), jnp.sin(ang)
    return jnp.concatenate([cos, cos], axis=-1), jnp.concatenate([-sin, sin], axis=-1)


def _overlap_table(seq, n_half):
    n_slc = seq // SLC_BLOCK
    cmp_start = np.arange(n_half) * CMP_STRIDE
    slc_start = np.arange(n_slc) * SLC_BLOCK
    ovl = ((cmp_start[None, :] < slc_start[:, None] + SLC_BLOCK)
           & (cmp_start[None, :] + CMP_BLOCK > slc_start[:, None]))
    ovl = ovl & (cmp_start[None, :] + CMP_BLOCK <= seq)
    return jnp.asarray(ovl.astype(np.float32), BF16)


def _expand_table(seq, tk):
    key_blk = np.arange(seq) // SLC_BLOCK
    e = (key_blk[:, None] == np.arange(LANES)[None, :]).astype(np.float32)
    return jnp.asarray(e.reshape(seq // tk, tk, LANES), BF16)


def kernel(x, mem, norm_g, w_in, conv_w, conv_b, cmp_pos, cmp_w1, cmp_w2, mem_norm_g, w_mem_kv,
           w_up_a, w_up_b, w_up_x, w_out, final_g):
    bsz, seq, d_model = x.shape
    mem_len = mem.shape[1]
    depth = w_in.shape[0]
    sc_w = conv_w.shape[-1]
    nsa_w = NSA_HEADS * HEAD_DIM
    kv_w = NSA_BRANCHES * 2 * NSA_KV_GROUPS * HEAD_DIM
    ng_w = NSA_BRANCHES * NSA_HEADS
    x_w = X_HEADS * HEAD_DIM
    proj_tn = 512

    conv_cols = 4 * sc_w
    q_off = 0
    kv_off = q_off + nsa_w
    ng_off = kv_off + kv_w
    nz_off = ng_off + proj_tn
    xq_off = nz_off + nsa_w
    xz_off = xq_off + x_w
    mg_off = xz_off + x_w
    assert seq % NSA_TQ == 0 and seq // SLC_BLOCK <= LANES and sc_w % 256 == 0 and 0 < ng_w < LANES
    assert conv_cols + mg_off + N_BRANCHES * d_model - proj_tn + ng_w == w_in.shape[-1]

    w_in_t = jnp.swapaxes(w_in, 1, 2)
    w_memkv_b = w_mem_kv.astype(BF16)
    cmp_w1_b = cmp_w1.astype(BF16)
    cmp_w2_b = cmp_w2.astype(BF16)
    norm_g3 = norm_g.reshape(depth, 1, d_model)
    mem_norm_g3 = mem_norm_g.reshape(depth, 1, d_model)

    cos, sin_signed = _rope_tables(seq)
    n_half = seq // CMP_STRIDE
    ovl = _overlap_table(seq, n_half)
    expand = _expand_table(seq, NSA_TK)

    t = bsz * seq
    xf = x.reshape(t, d_model)
    memf = mem.reshape(bsz * mem_len, d_model)

    hg, hr = _prenorm(xf, norm_g3, 0)
    for l in range(depth):
        y_a = _conv_branch(hg, hr, w_in_t, conv_w, conv_b, l, seq)
        proj = _in_proj(hg, hr, w_in_t, l, conv_cols, ng_off, ng_w, 2048, proj_tn)

        hm = _rmsnorm(memf, mem_norm_g3, l, BF16)
        memkv = _matmul(hm, w_memkv_b, l, F32, 512, 1024, name="mem_kv_proj")
        y_x = _mem_attention(proj, memkv, bsz, seq, mem_len, xq_off // x_w, xz_off // x_w)

        cmpkv = _compress(proj, cmp_pos, cmp_w1_b, cmp_w2_b, l, bsz, seq, kv_off)
        qb, qr, ks, vst, kw, vwt = _nsa_prep(proj, cos, sin_signed, bsz, seq, q_off, kv_off)
        y_b = _nsa_attention(qb, qr, cmpkv, ks, vst, kw, vwt, proj, ovl, expand, bsz, seq,
                             ng_off // LANES, nz_off // (NSA_REP * HEAD_DIM))

        u = _merge(y_a, y_b, y_x, w_up_a, w_up_b, w_up_x, proj, l, mg_off, d_model)
        if l + 1 < depth:
            xf, hg, hr = _out_proj(u, w_out, l, xf, 2048, 256, next_gain=norm_g3)
        else:
            xf = _out_proj(u, w_out, l, xf, 2048, 512)

    out = _rmsnorm(xf, final_g.reshape(1, 1, d_model), 0, F32)
    return out.reshape(bsz, seq, d_model)
```

```python
import functools

import numpy as np
import jax
import jax.numpy as jnp
from jax import lax
from jax.experimental import pallas as pl
from jax.experimental.pallas import tpu as pltpu

F32 = jnp.float32
BF16 = jnp.bfloat16

HEAD_DIM = 128
ROPE_THETA = 10000.0
EPS = 1e-6
NEG = -1e30
SC_KERNEL = 3
NSA_HEADS = 16
NSA_KV_GROUPS = 4
NSA_REP = NSA_HEADS // NSA_KV_GROUPS
NSA_BRANCHES = 3
CMP_BLOCK = 32
CMP_STRIDE = 16
SLC_BLOCK = 64
SLC_TOP_N = 16
WINDOW = 512
X_HEADS = 4
N_BRANCHES = 3
SCALE = HEAD_DIM ** -0.5
LOG2E = 1.4426950408889634

V7X_VMEM_BYTES = 64 * 1024 * 1024
VMEM_LIMIT_BYTES = V7X_VMEM_BYTES - 8 * 1024 * 1024
LANES = 128
F32_SUBLANES = 8

NSA_TQ = 512
NSA_TK = 256
VT_PAD = 16
MERGE_ROW_CHUNKS = 4
OUT_ROW_CHUNKS = 4
CONV_ROW_CHUNKS = 1


def _cparams(*sem):
    return pltpu.CompilerParams(dimension_semantics=sem, vmem_limit_bytes=VMEM_LIMIT_BYTES)


def _resident_rows_spec(tm, kdim):
    big = tm * kdim * 2 >= V7X_VMEM_BYTES // 4
    return pl.BlockSpec((tm, kdim), lambda i, j: (i, 0),
                        pipeline_mode=pl.Buffered(1) if big else None)


def _sigmoid(x):
    return 0.5 * jnp.tanh(0.5 * x) + 0.5


def _silu(x):
    return x * _sigmoid(x)


def _rmsnorm_kernel(x_ref, g_ref, o_ref):
    x = x_ref[...]
    ms = jnp.mean(x * x, axis=-1, keepdims=True)
    o_ref[...] = (x * lax.rsqrt(ms + EPS) * g_ref[...]).astype(o_ref.dtype)


def _rmsnorm(x, g3, l, out_dtype, tm=256):
    m, d = x.shape
    return pl.pallas_call(
        _rmsnorm_kernel,
        grid=(m // tm,),
        in_specs=[pl.BlockSpec((tm, d), lambda i: (i, 0)),
                  pl.BlockSpec((None, 1, d), lambda i: (l, 0, 0))],
        out_specs=pl.BlockSpec((tm, d), lambda i: (i, 0)),
        out_shape=jax.ShapeDtypeStruct((m, d), out_dtype),
        compiler_params=_cparams("parallel"),
        name="rmsnorm",
    )(x, g3)


def _prenorm_kernel(x_ref, g_ref, a_ref, r_ref):
    x = x_ref[...]
    r_ref[...] = lax.rsqrt(jnp.mean(x * x, axis=-1, keepdims=True) + EPS)
    a_ref[...] = (x * g_ref[...]).astype(a_ref.dtype)


def _prenorm(x, g3, l, tm=256):
    m, d = x.shape
    return pl.pallas_call(
        _prenorm_kernel,
        grid=(m // tm,),
        in_specs=[pl.BlockSpec((tm, d), lambda i: (i, 0)),
                  pl.BlockSpec((None, 1, d), lambda i: (l, 0, 0))],
        out_specs=[pl.BlockSpec((tm, d), lambda i: (i, 0)), pl.BlockSpec((tm, 1), lambda i: (i, 0))],
        out_shape=[jax.ShapeDtypeStruct((m, d), BF16), jax.ShapeDtypeStruct((m, 1), F32)],
        compiler_params=_cparams("parallel"),
        name="prenorm",
    )(x, g3)


def _mm_kernel(a_ref, b_ref, o_ref):
    o_ref[...] = jnp.dot(a_ref[...], b_ref[...], preferred_element_type=F32).astype(o_ref.dtype)


def _matmul(a, b, l, out_dtype, tm, tn, name):
    m, kdim = a.shape
    n = b.shape[-1]
    tm, tn = min(tm, m), min(tn, n)
    return pl.pallas_call(
        _mm_kernel,
        grid=(m // tm, n // tn),
        in_specs=[pl.BlockSpec((tm, kdim), lambda i, j: (i, 0)),
                  pl.BlockSpec((None, kdim, tn), lambda i, j: (l, 0, j))],
        out_specs=pl.BlockSpec((tm, tn), lambda i, j: (i, j)),
        out_shape=jax.ShapeDtypeStruct((m, n), out_dtype),
        compiler_params=_cparams("parallel", "arbitrary"),
        name=name,
    )(a, b)


def _in_proj_kernel(a_ref, w_ref, r_ref, o_ref):
    acc = lax.dot_general(a_ref[...], w_ref[0].astype(BF16), (((1,), (1,)), ((), ())),
                          preferred_element_type=F32)
    o_ref[...] = (acc * r_ref[...]).astype(o_ref.dtype)


def _in_proj(a, row_scale, wt, l, first, lo, gap, tm, tn):
    m, kdim = a.shape
    lo_tiles = lo // tn
    n = wt.shape[1] - first - gap + tn
    assert lo % tn == 0 and n % tn == 0 and gap < tn
    assert gap % F32_SUBLANES == 0 and first % F32_SUBLANES == 0

    def w_index(i, j):
        row = first + jnp.where(j <= lo_tiles, j * tn, (j - 1) * tn + gap)
        return l, pl.multiple_of(row, F32_SUBLANES), 0

    return pl.pallas_call(
        _in_proj_kernel,
        grid=(m // tm, n // tn),
        in_specs=[_resident_rows_spec(tm, kdim),
                  pl.BlockSpec((pl.Element(1), pl.Element(tn), pl.Element(kdim)), w_index),
                  pl.BlockSpec((tm, 1), lambda i, j: (i, 0))],
        out_specs=pl.BlockSpec((tm, tn), lambda i, j: (i, j)),
        out_shape=jax.ShapeDtypeStruct((m, n), F32),
        compiler_params=_cparams("parallel", "arbitrary"),
        name="in_proj",
    )(a, wt, row_scale)


def _out_proj_kernel(a_ref, w_ref, res_ref, o_ref):
    w = w_ref[...].astype(BF16)
    tm = o_ref.shape[0]
    step = tm // OUT_ROW_CHUNKS
    for r0 in range(0, tm, step):
        rows = slice(r0, r0 + step)
        acc = jnp.dot(a_ref[rows, :], w, preferred_element_type=F32)
        o_ref[rows, :] = res_ref[rows, :] + acc


def _out_proj_next_kernel(a_ref, w_ref, res_ref, g_ref, o_ref, an_ref, rn_ref, *, d_model):
    j = pl.program_id(1)
    last = pl.num_programs(1) - 1

    @pl.when(j == 0)
    def _():
        rn_ref[...] = jnp.zeros(rn_ref.shape, F32)

    w = w_ref[...].astype(BF16)
    g = g_ref[...]
    tm = o_ref.shape[0]
    step = tm // OUT_ROW_CHUNKS
    for r0 in range(0, tm, step):
        rows = slice(r0, r0 + step)
        xn = res_ref[rows, :] + jnp.dot(a_ref[rows, :], w, preferred_element_type=F32)
        o_ref[rows, :] = xn
        an_ref[rows, :] = (xn * g).astype(an_ref.dtype)
        ss = rn_ref[rows, :] + jnp.sum(xn * xn, axis=-1, keepdims=True)
        rn_ref[rows, :] = jnp.where(j == last, lax.rsqrt(ss * (1.0 / d_model) + EPS), ss)


def _out_proj(a, w, l, residual, tm, tn, next_gain=None):
    m, kdim = a.shape
    n = w.shape[-1]
    tile = pl.BlockSpec((tm, tn), lambda i, j: (i, j))
    in_specs = [_resident_rows_spec(tm, kdim),
                pl.BlockSpec((None, kdim, tn), lambda i, j: (l, 0, j)), tile]
    if next_gain is None:
        return pl.pallas_call(
            _out_proj_kernel,
            grid=(m // tm, n // tn),
            in_specs=in_specs,
            out_specs=tile,
            out_shape=jax.ShapeDtypeStruct((m, n), F32),
            compiler_params=_cparams("parallel", "arbitrary"),
            name="out_proj",
        )(a, w, residual)
    return pl.pallas_call(
        functools.partial(_out_proj_next_kernel, d_model=n),
        grid=(m // tm, n // tn),
        in_specs=in_specs + [pl.BlockSpec((None, 1, tn), lambda i, j: (l + 1, 0, j))],
        out_specs=[tile, tile, pl.BlockSpec((tm, 1), lambda i, j: (i, 0))],
        out_shape=[jax.ShapeDtypeStruct((m, n), F32), jax.ShapeDtypeStruct((m, n), BF16),
                   jax.ShapeDtypeStruct((m, 1), F32)],
        compiler_params=_cparams("parallel", "arbitrary"),
        name="out_proj_next",
    )(a, w, residual, next_gain)


def _conv_proj_kernel(a_ref, wh_ref, wb_ref, wc_ref, wz_ref, r_ref, cw_ref, cb_ref, o_ref):
    w = jnp.concatenate([wh_ref[...], wb_ref[...], wc_ref[...], wz_ref[...]], axis=0).astype(BF16)
    j = pl.program_id(1)
    cw = [cw_ref[k, pl.ds(j, 1), :] for k in range(SC_KERNEL)]
    cb = cb_ref[pl.ds(j, 1), :]
    seq, tc = o_ref.shape
    step = seq // CONV_ROW_CHUNKS
    row = lax.broadcasted_iota(jnp.int32, (step, tc), 0)
    u_prev = jnp.zeros((step, tc), F32)
    for r0 in range(0, seq, step):
        p = lax.dot_general(a_ref[r0:r0 + step, :], w, (((1,), (1,)), ((), ())),
                            preferred_element_type=F32) * r_ref[r0:r0 + step, :]
        h, b, c, z = (p[:, g * tc:(g + 1) * tc] for g in range(4))
        u = c * h
        u1 = jnp.where(row >= 1, pltpu.roll(u, 1, 0), pltpu.roll(u_prev, 1, 0))
        u2 = jnp.where(row >= 2, pltpu.roll(u, 2, 0), pltpu.roll(u_prev, 2, 0))
        y = cb + cw[0] * u2
        y = y + cw[1] * u1
        y = y + cw[2] * u
        o_ref[r0:r0 + step, :] = (b * y * _silu(z)).astype(o_ref.dtype)
        u_prev = u


def _conv_branch(a, row_scale, wt, conv_w, conv_b, l, seq, tc=LANES):
    m, kdim = a.shape
    depth, taps, width = conv_w.shape
    nb = width // tc
    conv_w = conv_w.reshape(depth, taps, nb, tc)
    conv_b = conv_b.reshape(depth, nb, tc)
    wsp = lambda g: pl.BlockSpec((None, tc, kdim), lambda b, j: (l, g * nb + j, 0))
    return pl.pallas_call(
        _conv_proj_kernel,
        grid=(m // seq, nb),
        in_specs=[_resident_rows_spec(seq, kdim), wsp(0), wsp(1), wsp(2), wsp(3),
                  pl.BlockSpec((seq, 1), lambda b, j: (b, 0)),
                  pl.BlockSpec((None, taps, nb, tc), lambda b, j: (l, 0, 0, 0)),
                  pl.BlockSpec((None, nb, tc), lambda b, j: (l, 0, 0))],
        out_specs=pl.BlockSpec((seq, tc), lambda b, j: (b, j)),
        out_shape=jax.ShapeDtypeStruct((m, width), BF16),
        compiler_params=_cparams("parallel", "arbitrary"),
        name="conv_branch",
    )(a, wt, wt, wt, wt, row_scale, conv_w, conv_b)


def _memattn_kernel(q_ref, k_ref, v_ref, z_ref, o_ref):
    for h in range(X_HEADS):
        sl = slice(h * HEAD_DIM, (h + 1) * HEAD_DIM)
        q = q_ref[:, sl].astype(BF16)
        k = k_ref[:, sl].astype(BF16)
        v = v_ref[:, sl].astype(BF16)
        s = lax.dot_general(q, k, (((1,), (1,)), ((), ())), preferred_element_type=F32) * SCALE
        e = jnp.exp(s - jnp.max(s, axis=-1, keepdims=True))
        p = e / jnp.sum(e, axis=-1, keepdims=True)
        o = jnp.dot(p.astype(BF16), v, preferred_element_type=F32)
        o_ref[:, sl] = (o * _silu(z_ref[:, sl])).astype(o_ref.dtype)


def _mem_attention(proj, memkv, bsz, seq, mem_len, q_blk, z_blk, tq=512):
    nq = seq // tq
    xw = X_HEADS * HEAD_DIM
    row = lambda cb: pl.BlockSpec((tq, xw), lambda b, i: (b * nq + i, cb))
    return pl.pallas_call(
        _memattn_kernel,
        grid=(bsz, nq),
        in_specs=[row(q_blk),
                  pl.BlockSpec((mem_len, xw), lambda b, i: (b, 0)),
                  pl.BlockSpec((mem_len, xw), lambda b, i: (b, 1)),
                  row(z_blk)],
        out_specs=row(0),
        out_shape=jax.ShapeDtypeStruct((bsz * seq, xw), BF16),
        compiler_params=_cparams("parallel", "parallel"),
        name="mem_attention",
    )(proj, memkv, memkv, proj)


def _compress_kernel(*refs, n_half):
    *kv_refs, pos_ref, w1_ref, w2_ref, o_ref = refs
    half = CMP_BLOCK // 2
    groups = len(kv_refs)
    acc_a = jnp.zeros((groups * n_half, HEAD_DIM), F32)
    acc_b = jnp.zeros((groups * n_half, HEAD_DIM), F32)
    for t in range(half):
        x = jnp.concatenate([ref[pl.ds(t, n_half, stride=half), :] for ref in kv_refs], axis=0)
        xa = (x + pos_ref[t:t + 1, :]).astype(BF16)
        xb = (x + pos_ref[half + t:half + t + 1, :]).astype(BF16)
        acc_a = acc_a + jnp.dot(xa, w1_ref[t * HEAD_DIM:(t + 1) * HEAD_DIM, :],
                                preferred_element_type=F32)
        acc_b = acc_b + jnp.dot(xb, w1_ref[(half + t) * HEAD_DIM:(half + t + 1) * HEAD_DIM, :],
                                preferred_element_type=F32)
    w2 = w2_ref[...]
    for g in range(groups):
        rows = slice(g * n_half, (g + 1) * n_half)
        hid = _silu(acc_a[rows] + pltpu.roll(acc_b[rows], n_half - 1, 0))
        o_ref[g] = jnp.dot(hid.astype(BF16), w2, preferred_element_type=F32)


def _compress(proj, cmp_pos, cmp_w1, cmp_w2, l, bsz, seq, kv_off):
    g = NSA_KV_GROUPS
    kv_blk = kv_off // HEAD_DIM
    n_half = seq // (CMP_BLOCK // 2)
    kv_spec = lambda gi: pl.BlockSpec((seq, HEAD_DIM), lambda b, c: (b, kv_blk + c * g + gi))
    return pl.pallas_call(
        functools.partial(_compress_kernel, n_half=n_half),
        grid=(bsz, 2),
        in_specs=[kv_spec(gi) for gi in range(g)] + [
                  pl.BlockSpec((None, None, CMP_BLOCK, HEAD_DIM), lambda b, c: (l, c, 0, 0)),
                  pl.BlockSpec((None, None, CMP_BLOCK * HEAD_DIM, HEAD_DIM),
                               lambda b, c: (l, c, 0, 0)),
                  pl.BlockSpec((None, None, HEAD_DIM, HEAD_DIM), lambda b, c: (l, c, 0, 0))],
        out_specs=pl.BlockSpec((None, None, g, n_half, HEAD_DIM), lambda b, c: (b, c, 0, 0, 0)),
        out_shape=jax.ShapeDtypeStruct((bsz, 2, g, n_half, HEAD_DIM), F32),
        compiler_params=_cparams("parallel", "parallel"),
        name="nsa_compress",
    )(*([proj] * g), cmp_pos, cmp_w1, cmp_w2)


def _rope_tile(x, cos, sin_signed):
    return x * cos + pltpu.roll(x, HEAD_DIM // 2, 1) * sin_signed


def _nsa_prep_kernel(q_ref, ks_ref, vs_ref, kw_ref, vw_ref, cos_ref, sin_ref,
                     qb_ref, qr_ref, kso_ref, vst_ref, kwo_ref, vwt_ref):
    cos = cos_ref[...]
    sin = sin_ref[...]
    for h in range(NSA_HEADS):
        sl = slice(h * HEAD_DIM, (h + 1) * HEAD_DIM)
        x = q_ref[:, sl]
        qb_ref[:, sl] = x.astype(BF16)
        qr_ref[:, sl] = (_rope_tile(x, cos, sin) * (SCALE * LOG2E)).astype(BF16)
    n_tiles, _, width = vst_ref.shape
    for g in range(NSA_KV_GROUPS):
        sl = slice(g * HEAD_DIM, (g + 1) * HEAD_DIM)
        kso_ref[:, sl] = _rope_tile(ks_ref[:, sl], cos, sin).astype(BF16)
        kwo_ref[:, sl] = _rope_tile(kw_ref[:, sl], cos, sin).astype(BF16)
        vs_t = jnp.transpose(vs_ref[:, sl]).astype(BF16)
        vw_t = jnp.transpose(vw_ref[:, sl]).astype(BF16)
        for c in range(n_tiles):
            vst_ref[c, sl, :] = vs_t[:, c * width:(c + 1) * width]
            vwt_ref[c, sl, :] = vw_t[:, c * width:(c + 1) * width]


def _nsa_prep(proj, cos, sin_signed, bsz, seq, q_off, kv_off):
    tr = NSA_TQ
    nsa_w = NSA_HEADS * HEAD_DIM
    gw = NSA_KV_GROUPS * HEAD_DIM
    nr = seq // tr
    t = bsz * seq
    qblk = q_off // nsa_w
    kvb = kv_off // gw
    n_tiles = tr // NSA_TK
    rowq = lambda w, cb: pl.BlockSpec((tr, w), lambda b, i: (b * nr + i, cb))
    tab = pl.BlockSpec((tr, HEAD_DIM), lambda b, i: (i, 0))
    vt_spec = pl.BlockSpec((None, n_tiles, gw, NSA_TK), lambda b, i: (b, i, 0, 0))
    vt_shape = jax.ShapeDtypeStruct((bsz, seq // NSA_TK, gw, NSA_TK), BF16)
    return pl.pallas_call(
        _nsa_prep_kernel,
        grid=(bsz, nr),
        in_specs=[rowq(nsa_w, qblk), rowq(gw, kvb + 2), rowq(gw, kvb + 3),
                  rowq(gw, kvb + 4), rowq(gw, kvb + 5), tab, tab],
        out_specs=[rowq(nsa_w, 0), rowq(nsa_w, 0), rowq(gw, 0), vt_spec, rowq(gw, 0), vt_spec],
        out_shape=[jax.ShapeDtypeStruct((t, nsa_w), BF16), jax.ShapeDtypeStruct((t, nsa_w), BF16),
                   jax.ShapeDtypeStruct((t, gw), BF16), vt_shape,
                   jax.ShapeDtypeStruct((t, gw), BF16), vt_shape],
        compiler_params=_cparams("parallel", "parallel"),
        name="nsa_prep",
    )(proj, proj, proj, proj, proj, cos, sin_signed)


def _col_reduce(x, op, ways=4):
    nblk = x.shape[0] // 8
    accs = [x[8 * k:8 * (k + 1)] for k in range(min(ways, nblk))]
    for k in range(len(accs), nblk):
        accs[k % ways] = op(accs[k % ways], x[8 * k:8 * (k + 1)])
    while len(accs) > 1:
        accs = [op(accs[a], accs[a + 1]) for a in range(0, len(accs), 2)]
    return accs[0]


def _col_max(x):
    return jnp.max(_col_reduce(x, jnp.maximum), axis=0, keepdims=True)


def _col_sum(x):
    return jnp.sum(_col_reduce(x, jnp.add), axis=0, keepdims=True)


def _nsa_kernel(qb_ref, qr_ref, kc_ref, vc_ref, ks_ref, vst_ref, kw_ref, vwt_ref,
                ng_ref, nz_ref, ovl_ref, exp_ref, o_ref,
                qa_ref, ts0_ref, ts1_ref, xs0_ref, xs1_ref,
                tw0_ref, tw1_ref, tw2_ref, tw3_ref, xw0_ref, xw1_ref, xw2_ref, xw3_ref,
                ms_ref, mw_ref, accs_ref, accw_ref, ocmp_ref, ngt_ref,
                *, tq, tk, n_slc):
    assert tq == 2 * tk and WINDOW == tq
    gi = pl.program_id(1)
    i = pl.program_id(2)
    q0 = i * tq
    nt = (((1,), (1,)), ((), ()))
    pos_row = q0 + lax.broadcasted_iota(jnp.int32, (1, tq), 1)
    halves = [slice(hq * NSA_REP * tk, (hq + 1) * NSA_REP * tk) for hq in range(2)]
    pos_all = jnp.concatenate([pos_row[:, hq * tk:(hq + 1) * tk] for hq in range(2)
                               for _ in range(NSA_REP)], axis=1)
    heads = [slice(r * HEAD_DIM, (r + 1) * HEAD_DIM) for r in range(NSA_REP)]

    def stack_rows(ref):
        return jnp.concatenate([ref[hq * tk:(hq + 1) * tk, h] for hq in range(2) for h in heads],
                               axis=0)

    def head_cols(x, r):
        return jnp.concatenate([x[:, (hq * NSA_REP + r) * tk:(hq * NSA_REP + r + 1) * tk]
                                for hq in range(2)], axis=1)

    qb = stack_rows(qb_ref)
    kc = kc_ref[...].astype(BF16)
    vc_t = jnp.transpose(vc_ref[...]).astype(BF16)
    n_cmp = kc.shape[0]
    cmp_end = CMP_STRIDE * lax.broadcasted_iota(jnp.int32, (n_cmp, 1), 0) + (CMP_BLOCK - 1)
    cmask = cmp_end <= pos_all
    s = lax.dot_general(kc, qb, nt, preferred_element_type=F32) * SCALE
    s = jnp.where(cmask, s, NEG)
    e = jnp.exp(s - _col_max(s))
    p = e * (1.0 / _col_sum(e))
    p = jnp.where(cmask, p, 0.0).astype(BF16)
    ocmp_ref[...] = jnp.dot(vc_t, p, preferred_element_type=F32)

    qa_ref[:, 0:HEAD_DIM] = stack_rows(qr_ref)
    n_top = min(SLC_TOP_N, n_slc)

    @pl.when((i + 1) * tq <= n_top * SLC_BLOCK)
    def _():
        qa_ref[:, HEAD_DIM:] = jnp.zeros((NSA_REP * tq, LANES), BF16)

    @pl.when((i + 1) * tq > n_top * SLC_BLOCK)
    def _():
        imp_all = jnp.dot(ovl_ref[...], p, preferred_element_type=F32)
        val = head_cols(imp_all, 0)
        for r in range(1, NSA_REP):
            val = val + head_cols(imp_all, r)
        blk = lax.broadcasted_iota(jnp.int32, (n_slc, 1), 0)
        cur = pos_row // SLC_BLOCK
        future = blk * SLC_BLOCK > pos_row
        forced = (blk == 0) | (blk == cur) | (blk == cur - 1)
        val = jnp.where(future, -jnp.inf, jnp.where(forced, jnp.inf, val))
        rank = jnp.zeros((n_slc, tq), jnp.int32)
        for j in range(n_slc):
            vj = val[j:j + 1, :]
            beats = (vj > val) | ((vj == val) & (blk > j))
            rank = rank + beats.astype(jnp.int32)
        sel_bias = jnp.where(rank < n_top, 0.0, NEG)
        sel_bias = jnp.concatenate([sel_bias, jnp.zeros((LANES - n_slc, tq), F32)], axis=0)
        sel_bias = jnp.transpose(sel_bias).astype(BF16)
        qa_ref[:, HEAD_DIM:] = jnp.concatenate([sel_bias[hq * tk:(hq + 1) * tk] for hq in range(2)
                                                for _ in range(NSA_REP)], axis=0)

    ones_rows = (lax.broadcasted_iota(jnp.int32, (VT_PAD, tk), 0) == 0).astype(BF16)

    def scores(t_ref, x_ref, keys, queries, visible):
        t = lax.dot_general(keys, queries, nt, preferred_element_type=F32)
        if visible is not None:
            t = jnp.where(visible, t, NEG)
        t_ref[...] = t
        x_ref[...] = _col_max(t)

    def absorb(t_ref, x_ref, vt, m_ref, acc_ref):
        m_old = m_ref[...]
        m_new = jnp.maximum(m_old, x_ref[...])
        alpha = jnp.exp2(m_old - m_new)
        e = jnp.exp2(t_ref[...] - m_new).astype(BF16)
        vt_ones = jnp.concatenate([vt, ones_rows], axis=0)
        acc_ref[...] = alpha * acc_ref[...] + jnp.dot(vt_ones, e, preferred_element_type=F32)
        m_ref[...] = m_new

    def key_pos(c):
        return c * tk + lax.broadcasted_iota(jnp.int32, (tk, 1), 0)

    def slc_scores(t_ref, x_ref, c, causal):
        start = pl.multiple_of(c * tk, tk)
        keys = jnp.concatenate([ks_ref[pl.ds(start, tk), :], exp_ref[c]], axis=1)
        scores(t_ref, x_ref, keys, qa_ref[...], (key_pos(c) <= pos_all) if causal else None)

    def slc_absorb(t_ref, x_ref, c):
        absorb(t_ref, x_ref, vst_ref[c], ms_ref, accs_ref)

    slc_scores(ts0_ref, xs0_ref, 2 * i, True)
    slc_scores(ts1_ref, xs1_ref, 2 * i + 1, True)
    ms_ref[...] = jnp.maximum(xs0_ref[...], xs1_ref[...])
    accs_ref[...] = jnp.zeros(accs_ref.shape, F32)
    slc_absorb(ts0_ref, xs0_ref, 2 * i)

    def slc_pair(j, carry):
        slc_scores(ts0_ref, xs0_ref, 2 * j, False)
        slc_absorb(ts1_ref, xs1_ref, jnp.where(j == 0, 2 * i + 1, 2 * j - 1))
        slc_scores(ts1_ref, xs1_ref, 2 * j + 1, False)
        slc_absorb(ts0_ref, xs0_ref, 2 * j)
        return carry

    lax.fori_loop(0, i, slc_pair, 0)
    slc_absorb(ts1_ref, xs1_ref, jnp.where(i == 0, 1, 2 * i - 1))

    low = jnp.maximum(pos_all - WINDOW, -1)
    wbuf = [(tw0_ref, xw0_ref), (tw1_ref, xw1_ref), (tw2_ref, xw2_ref), (tw3_ref, xw3_ref)]

    def win_scores(buf, hq, back):
        c = 2 * i + hq - back
        kpos = key_pos(c)
        start = pl.multiple_of(jnp.maximum(c, 0) * tk, tk)
        if back == 0:
            visible = kpos <= pos_all[:, halves[hq]]
        elif back == hq:
            visible = None
        else:
            visible = kpos > low[:, halves[hq]]
        scores(*wbuf[buf], kw_ref[pl.ds(start, tk), :], qa_ref[halves[hq], 0:HEAD_DIM], visible)

    def win_absorb(buf, hq, back):
        c = jnp.maximum(2 * i + hq - back, 0)
        absorb(*wbuf[buf], vwt_ref[c], mw_ref.at[:, halves[hq]], accw_ref.at[:, halves[hq]])

    mw_ref[...] = jnp.full(mw_ref.shape, NEG, F32)
    accw_ref[...] = jnp.zeros(accw_ref.shape, F32)
    win_scores(0, 0, 0)
    win_scores(1, 1, 0)
    win_absorb(0, 0, 0)
    win_scores(2, 0, 1)
    win_absorb(1, 1, 0)
    win_scores(3, 1, 1)
    win_absorb(2, 0, 1)
    win_scores(0, 0, 2)
    win_absorb(3, 1, 1)
    win_scores(1, 1, 2)
    win_absorb(0, 0, 2)
    win_absorb(1, 1, 2)

    ngt_ref[...] = jnp.transpose(ng_ref[...])
    inv_ls = 1.0 / accs_ref[HEAD_DIM:HEAD_DIM + 1, :]
    inv_lw = 1.0 / accw_ref[HEAD_DIM:HEAD_DIM + 1, :]
    o_cmp = ocmp_ref[...]
    o_slc = accs_ref[0:HEAD_DIM, :] * inv_ls
    o_win = accw_ref[0:HEAD_DIM, :] * inv_lw
    for r in range(NSA_REP):
        row = (gi * NSA_REP + r) * NSA_BRANCHES
        g_cmp = _sigmoid(ngt_ref[pl.ds(row, 1), :])
        g_slc = _sigmoid(ngt_ref[pl.ds(row + 1, 1), :])
        g_win = _sigmoid(ngt_ref[pl.ds(row + 2, 1), :])
        o_t = (g_cmp * head_cols(o_cmp, r) + g_slc * head_cols(o_slc, r)
               + g_win * head_cols(o_win, r))
        o_ref[:, heads[r]] = (jnp.transpose(o_t) * _silu(nz_ref[:, heads[r]])).astype(o_ref.dtype)


def _nsa_attention(qb, qr, cmpkv, ks, vst, kw, vwt, proj, ovl, expand, bsz, seq, ng_blk, nz_blk):
    tq, tk = NSA_TQ, NSA_TK
    g = NSA_KV_GROUPS
    gw = NSA_REP * HEAD_DIM
    nq = seq // tq
    n_slc = seq // SLC_BLOCK
    n_half = cmpkv.shape[-2]
    stacked = NSA_REP * tq
    rowq = lambda cb: pl.BlockSpec((tq, gw), lambda b, gi, i: (b * nq + i, cb + gi))
    kvs = pl.BlockSpec((seq, HEAD_DIM), lambda b, gi, i: (b, gi))
    cmps = lambda c: pl.BlockSpec((None, None, None, n_half, HEAD_DIM),
                                  lambda b, gi, i: (b, c, gi, 0, 0))
    vts = pl.BlockSpec((None, seq // tk, HEAD_DIM, tk), lambda b, gi, i: (b, 0, gi, 0))
    score_buf = pltpu.VMEM((tk, stacked), F32)
    stat_row = pltpu.VMEM((1, stacked), F32)
    half_score_buf = pltpu.VMEM((tk, stacked // 2), F32)
    half_stat_row = pltpu.VMEM((1, stacked // 2), F32)
    flash_acc = pltpu.VMEM((HEAD_DIM + VT_PAD, stacked), F32)
    return pl.pallas_call(
        functools.partial(_nsa_kernel, tq=tq, tk=tk, n_slc=n_slc),
        grid=(bsz, g, nq),
        in_specs=[rowq(0), rowq(0), cmps(0), cmps(1), kvs, vts, kvs, vts,
                  pl.BlockSpec((tq, LANES), lambda b, gi, i: (b * nq + i, ng_blk)),
                  rowq(nz_blk),
                  pl.BlockSpec((n_slc, n_half), lambda b, gi, i: (0, 0)),
                  pl.BlockSpec((seq // tk, tk, LANES), lambda b, gi, i: (0, 0, 0))],
        out_specs=rowq(0),
        out_shape=jax.ShapeDtypeStruct((bsz * seq, NSA_HEADS * HEAD_DIM), BF16),
        scratch_shapes=[pltpu.VMEM((stacked, 2 * HEAD_DIM), BF16),
                        score_buf, score_buf, stat_row, stat_row,
                        half_score_buf, half_score_buf, half_score_buf, half_score_buf,
                        half_stat_row, half_stat_row, half_stat_row, half_stat_row,
                        stat_row, stat_row,
                        flash_acc, flash_acc, pltpu.VMEM((HEAD_DIM, stacked), F32),
                        pltpu.VMEM((LANES, tq), F32)],
        compiler_params=_cparams("parallel", "parallel", "arbitrary"),
        name="nsa_attention",
    )(qb, qr, cmpkv, cmpkv, ks, vst, kw, vwt, proj, proj, ovl, expand)


def _merge_kernel(ya_ref, yb_ref, yx_ref, wa_ref, wb_ref, wx_ref, g0_ref, g1_ref, g2_ref, o_ref):
    wa = wa_ref[...].astype(BF16)
    wb = wb_ref[...].astype(BF16)
    wx = wx_ref[...].astype(BF16)
    tm = o_ref.shape[0]
    step = tm // MERGE_ROW_CHUNKS
    for r0 in range(0, tm, step):
        rows = slice(r0, r0 + step)

        def branch(y_ref, w, g_ref):
            up = jnp.dot(y_ref[rows, :], w, preferred_element_type=F32)
            return _sigmoid(g_ref[rows, :]) * up

        u = branch(ya_ref, wa, g0_ref) + branch(yb_ref, wb, g1_ref) + branch(yx_ref, wx, g2_ref)
        o_ref[rows, :] = u.astype(o_ref.dtype)


def _merge(ya, yb, yx, wa, wb, wx, proj, l, mg_off, d_model, tm=2048, tn=256):
    t = ya.shape[0]
    nj = d_model // tn
    mgb = mg_off // tn
    act = lambda w: pl.BlockSpec((tm, w), lambda i, j: (i, 0), pipeline_mode=pl.Buffered(1))
    wsp = lambda w: pl.BlockSpec((None, w, tn), lambda i, j: (l, 0, j))
    gsp = lambda br: pl.BlockSpec((tm, tn), lambda i, j: (i, mgb + br * nj + j))
    return pl.pallas_call(
        _merge_kernel,
        grid=(t // tm, nj),
        in_specs=[act(ya.shape[1]), act(yb.shape[1]), act(yx.shape[1]),
                  wsp(wa.shape[1]), wsp(wb.shape[1]), wsp(wx.shape[1]),
                  gsp(0), gsp(1), gsp(2)],
        out_specs=pl.BlockSpec((tm, tn), lambda i, j: (i, j)),
        out_shape=jax.ShapeDtypeStruct((t, d_model), BF16),
        compiler_params=_cparams("parallel", "arbitrary"),
        name="gated_merge",
    )(ya, yb, yx, wa, wb, wx, proj, proj, proj)


def _rope_tables(seq):
    half = HEAD_DIM // 2
    inv_freq = ROPE_THETA ** (-jnp.arange(half, dtype=F32) / half)
    ang = jnp.arange(seq, dtype=jnp.int32).astype(F32)[:, None] * inv_freq[None, :]
    cos, sin = jnp.cos(ang), jnp.sin(ang)
    return jnp.concatenate([cos, cos], axis=-1), jnp.concatenate([-sin, sin], axis=-1)


def _overlap_table(seq, n_half):
    n_slc = seq // SLC_BLOCK
    cmp_start = np.arange(n_half) * CMP_STRIDE
    slc_start = np.arange(n_slc) * SLC_BLOCK
    ovl = ((cmp_start[None, :] < slc_start[:, None] + SLC_BLOCK)
           & (cmp_start[None, :] + CMP_BLOCK > slc_start[:, None]))
    ovl = ovl & (cmp_start[None, :] + CMP_BLOCK <= seq)
    return jnp.asarray(ovl.astype(np.float32), BF16)


def _expand_table(seq, tk):
    key_blk = np.arange(seq) // SLC_BLOCK
    e = (key_blk[:, None] == np.arange(LANES)[None, :]).astype(np.float32)
    return jnp.asarray(e.reshape(seq // tk, tk, LANES), BF16)


def kernel(x, mem, norm_g, w_in, conv_w, conv_b, cmp_pos, cmp_w1, cmp_w2, mem_norm_g, w_mem_kv,
           w_up_a, w_up_b, w_up_x, w_out, final_g):
    bsz, seq, d_model = x.shape
    mem_len = mem.shape[1]
    depth = w_in.shape[0]
    sc_w = conv_w.shape[-1]
    nsa_w = NSA_HEADS * HEAD_DIM
    kv_w = NSA_BRANCHES * 2 * NSA_KV_GROUPS * HEAD_DIM
    ng_w = NSA_BRANCHES * NSA_HEADS
    x_w = X_HEADS * HEAD_DIM
    proj_tn = 512

    conv_cols = 4 * sc_w
    q_off = 0
    kv_off = q_off + nsa_w
    ng_off = kv_off + kv_w
    nz_off = ng_off + proj_tn
    xq_off = nz_off + nsa_w
    xz_off = xq_off + x_w
    mg_off = xz_off + x_w
    assert seq % NSA_TQ == 0 and seq // SLC_BLOCK <= LANES and sc_w % 256 == 0 and 0 < ng_w < LANES
    assert conv_cols + mg_off + N_BRANCHES * d_model - proj_tn + ng_w == w_in.shape[-1]

    w_in_t = jnp.swapaxes(w_in, 1, 2)
    w_memkv_b = w_mem_kv.astype(BF16)
    cmp_w1_b = cmp_w1.astype(BF16)
    cmp_w2_b = cmp_w2.astype(BF16)
    norm_g3 = norm_g.reshape(depth, 1, d_model)
    mem_norm_g3 = mem_norm_g.reshape(depth, 1, d_model)

    cos, sin_signed = _rope_tables(seq)
    n_half = seq // CMP_STRIDE
    ovl = _overlap_table(seq, n_half)
    expand = _expand_table(seq, NSA_TK)

    t = bsz * seq
    xf = x.reshape(t, d_model)
    memf = mem.reshape(bsz * mem_len, d_model)

    hg, hr = _prenorm(xf, norm_g3, 0)
    for l in range(depth):
        y_a = _conv_branch(hg, hr, w_in_t, conv_w, conv_b, l, seq)
        proj = _in_proj(hg, hr, w_in_t, l, conv_cols, ng_off, ng_w, 2048, proj_tn)

        hm = _rmsnorm(memf, mem_norm_g3, l, BF16)
        memkv = _matmul(hm, w_memkv_b, l, F32, 512, 1024, name="mem_kv_proj")
        y_x = _mem_attention(proj, memkv, bsz, seq, mem_len, xq_off // x_w, xz_off // x_w)

        cmpkv = _compress(proj, cmp_pos, cmp_w1_b, cmp_w2_b, l, bsz, seq, kv_off)
        qb, qr, ks, vst, kw, vwt = _nsa_prep(proj, cos, sin_signed, bsz, seq, q_off, kv_off)
        y_b = _nsa_attention(qb, qr, cmpkv, ks, vst, kw, vwt, proj, ovl, expand, bsz, seq,
                             ng_off // LANES, nz_off // (NSA_REP * HEAD_DIM))

        u = _merge(y_a, y_b, y_x, w_up_a, w_up_b, w_up_x, proj, l, mg_off, d_model)
        if l + 1 < depth:
            xf, hg, hr = _out_proj(u, w_out, l, xf, 2048, 256, next_gain=norm_g3)
        else:
            xf = _out_proj(u, w_out, l, xf, 2048, 512)

    out = _rmsnorm(xf, final_g.reshape(1, 1, d_model), 0, F32)
    return out.reshape(bsz, seq, d_model)
```

```python
import functools

import numpy as np
import jax
import jax.numpy as jnp
from jax import lax
from jax.experimental import pallas as pl
from jax.experimental.pallas import tpu as pltpu

F32 = jnp.float32
BF16 = jnp.bfloat16

HEAD_DIM = 128
ROPE_THETA = 10000.0
EPS = 1e-6
NEG = -1e30
SC_KERNEL = 3
NSA_HEADS = 16
NSA_KV_GROUPS = 4
NSA_REP = NSA_HEADS // NSA_KV_GROUPS
NSA_BRANCHES = 3
CMP_BLOCK = 32
CMP_STRIDE = 16
SLC_BLOCK = 64
SLC_TOP_N = 16
WINDOW = 512
X_HEADS = 4
N_BRANCHES = 3
SCALE = HEAD_DIM ** -0.5
LOG2E = 1.4426950408889634

V7X_VMEM_BYTES = 64 * 1024 * 1024
VMEM_LIMIT_BYTES = V7X_VMEM_BYTES - 8 * 1024 * 1024
LANES = 128
F32_SUBLANES = 8

NSA_TQ = 512
NSA_TK = 256
VT_PAD = 16
MERGE_ROW_CHUNKS = 4
OUT_ROW_CHUNKS = 4
CONV_ROW_CHUNKS = 4


def _cparams(*sem):
    return pltpu.CompilerParams(dimension_semantics=sem, vmem_limit_bytes=VMEM_LIMIT_BYTES)


def _resident_rows_spec(tm, kdim):
    big = tm * kdim * 2 >= V7X_VMEM_BYTES // 4
    return pl.BlockSpec((tm, kdim), lambda i, j: (i, 0),
                        pipeline_mode=pl.Buffered(1) if big else None)


def _sigmoid(x):
    return 0.5 * jnp.tanh(0.5 * x) + 0.5


def _silu(x):
    return x * _sigmoid(x)


def _rmsnorm_kernel(x_ref, g_ref, o_ref):
    x = x_ref[...]
    ms = jnp.mean(x * x, axis=-1, keepdims=True)
    o_ref[...] = (x * lax.rsqrt(ms + EPS) * g_ref[...]).astype(o_ref.dtype)


def _rmsnorm(x, g3, l, out_dtype, tm=256):
    m, d = x.shape
    return pl.pallas_call(
        _rmsnorm_kernel,
        grid=(m // tm,),
        in_specs=[pl.BlockSpec((tm, d), lambda i: (i, 0)),
                  pl.BlockSpec((None, 1, d), lambda i: (l, 0, 0))],
        out_specs=pl.BlockSpec((tm, d), lambda i: (i, 0)),
        out_shape=jax.ShapeDtypeStruct((m, d), out_dtype),
        compiler_params=_cparams("parallel"),
        name="rmsnorm",
    )(x, g3)


def _prenorm_kernel(x_ref, g_ref, a_ref, r_ref):
    x = x_ref[...]
    r_ref[...] = lax.rsqrt(jnp.mean(x * x, axis=-1, keepdims=True) + EPS)
    a_ref[...] = (x * g_ref[...]).astype(a_ref.dtype)


def _prenorm(x, g3, l, tm=256):
    m, d = x.shape
    return pl.pallas_call(
        _prenorm_kernel,
        grid=(m // tm,),
        in_specs=[pl.BlockSpec((tm, d), lambda i: (i, 0)),
                  pl.BlockSpec((None, 1, d), lambda i: (l, 0, 0))],
        out_specs=[pl.BlockSpec((tm, d), lambda i: (i, 0)), pl.BlockSpec((tm, 1), lambda i: (i, 0))],
        out_shape=[jax.ShapeDtypeStruct((m, d), BF16), jax.ShapeDtypeStruct((m, 1), F32)],
        compiler_params=_cparams("parallel"),
        name="prenorm",
    )(x, g3)


def _mm_kernel(a_ref, b_ref, o_ref):
    o_ref[...] = jnp.dot(a_ref[...], b_ref[...], preferred_element_type=F32).astype(o_ref.dtype)


def _matmul(a, b, l, out_dtype, tm, tn, name):
    m, kdim = a.shape
    n = b.shape[-1]
    tm, tn = min(tm, m), min(tn, n)
    return pl.pallas_call(
        _mm_kernel,
        grid=(m // tm, n // tn),
        in_specs=[pl.BlockSpec((tm, kdim), lambda i, j: (i, 0)),
                  pl.BlockSpec((None, kdim, tn), lambda i, j: (l, 0, j))],
        out_specs=pl.BlockSpec((tm, tn), lambda i, j: (i, j)),
        out_shape=jax.ShapeDtypeStruct((m, n), out_dtype),
        compiler_params=_cparams("parallel", "arbitrary"),
        name=name,
    )(a, b)


def _in_proj_kernel(a_ref, w_ref, r_ref, o_ref):
    acc = lax.dot_general(a_ref[...], w_ref[0].astype(BF16), (((1,), (1,)), ((), ())),
                          preferred_element_type=F32)
    o_ref[...] = (acc * r_ref[...]).astype(o_ref.dtype)


def _in_proj(a, row_scale, wt, l, first, lo, gap, tm, tn):
    m, kdim = a.shape
    lo_tiles = lo // tn
    n = wt.shape[1] - first - gap + tn
    assert lo % tn == 0 and n % tn == 0 and gap < tn
    assert gap % F32_SUBLANES == 0 and first % F32_SUBLANES == 0

    def w_index(i, j):
        row = first + jnp.where(j <= lo_tiles, j * tn, (j - 1) * tn + gap)
        return l, pl.multiple_of(row, F32_SUBLANES), 0

    return pl.pallas_call(
        _in_proj_kernel,
        grid=(m // tm, n // tn),
        in_specs=[_resident_rows_spec(tm, kdim),
                  pl.BlockSpec((pl.Element(1), pl.Element(tn), pl.Element(kdim)), w_index),
                  pl.BlockSpec((tm, 1), lambda i, j: (i, 0))],
        out_specs=pl.BlockSpec((tm, tn), lambda i, j: (i, j)),
        out_shape=jax.ShapeDtypeStruct((m, n), F32),
        compiler_params=_cparams("parallel", "arbitrary"),
        name="in_proj",
    )(a, wt, row_scale)


def _out_proj_kernel(a_ref, w_ref, res_ref, o_ref):
    w = w_ref[...].astype(BF16)
    tm = o_ref.shape[0]
    step = tm // OUT_ROW_CHUNKS
    for r0 in range(0, tm, step):
        rows = slice(r0, r0 + step)
        acc = jnp.dot(a_ref[rows, :], w, preferred_element_type=F32)
        o_ref[rows, :] = res_ref[rows, :] + acc


def _out_proj_next_kernel(a_ref, w_ref, res_ref, g_ref, o_ref, an_ref, rn_ref, *, d_model):
    j = pl.program_id(1)
    last = pl.num_programs(1) - 1

    @pl.when(j == 0)
    def _():
        rn_ref[...] = jnp.zeros(rn_ref.shape, F32)

    w = w_ref[...].astype(BF16)
    g = g_ref[...]
    tm = o_ref.shape[0]
    step = tm // OUT_ROW_CHUNKS
    for r0 in range(0, tm, step):
        rows = slice(r0, r0 + step)
        xn = res_ref[rows, :] + jnp.dot(a_ref[rows, :], w, preferred_element_type=F32)
        o_ref[rows, :] = xn
        an_ref[rows, :] = (xn * g).astype(an_ref.dtype)
        ss = rn_ref[rows, :] + jnp.sum(xn * xn, axis=-1, keepdims=True)
        rn_ref[rows, :] = jnp.where(j == last, lax.rsqrt(ss * (1.0 / d_model) + EPS), ss)


def _out_proj(a, w, l, residual, tm, tn, next_gain=None):
    m, kdim = a.shape
    n = w.shape[-1]
    tile = pl.BlockSpec((tm, tn), lambda i, j: (i, j))
    in_specs = [_resident_rows_spec(tm, kdim),
                pl.BlockSpec((None, kdim, tn), lambda i, j: (l, 0, j)), tile]
    if next_gain is None:
        return pl.pallas_call(
            _out_proj_kernel,
            grid=(m // tm, n // tn),
            in_specs=in_specs,
            out_specs=tile,
            out_shape=jax.ShapeDtypeStruct((m, n), F32),
            compiler_params=_cparams("parallel", "arbitrary"),
            name="out_proj",
        )(a, w, residual)
    return pl.pallas_call(
        functools.partial(_out_proj_next_kernel, d_model=n),
        grid=(m // tm, n // tn),
        in_specs=in_specs + [pl.BlockSpec((None, 1, tn), lambda i, j: (l + 1, 0, j))],
        out_specs=[tile, tile, pl.BlockSpec((tm, 1), lambda i, j: (i, 0))],
        out_shape=[jax.ShapeDtypeStruct((m, n), F32), jax.ShapeDtypeStruct((m, n), BF16),
                   jax.ShapeDtypeStruct((m, 1), F32)],
        compiler_params=_cparams("parallel", "arbitrary"),
        name="out_proj_next",
    )(a, w, residual, next_gain)


def _conv_proj_kernel(a_ref, wh_ref, wb_ref, wc_ref, wz_ref, r_ref, cw_ref, cb_ref, o_ref):
    w = jnp.concatenate([wh_ref[...], wb_ref[...], wc_ref[...], wz_ref[...]], axis=0).astype(BF16)
    j = pl.program_id(1)
    cw = [cw_ref[k, pl.ds(j, 1), :] for k in range(SC_KERNEL)]
    cb = cb_ref[pl.ds(j, 1), :]
    seq, tc = o_ref.shape
    step = seq // CONV_ROW_CHUNKS
    row = lax.broadcasted_iota(jnp.int32, (step, tc), 0)
    u_prev = jnp.zeros((step, tc), F32)
    for r0 in range(0, seq, step):
        p = lax.dot_general(a_ref[r0:r0 + step, :], w, (((1,), (1,)), ((), ())),
                            preferred_element_type=F32) * r_ref[r0:r0 + step, :]
        h, b, c, z = (p[:, g * tc:(g + 1) * tc] for g in range(4))
        u = c * h
        u1 = jnp.where(row >= 1, pltpu.roll(u, 1, 0), pltpu.roll(u_prev, 1, 0))
        u2 = jnp.where(row >= 2, pltpu.roll(u, 2, 0), pltpu.roll(u_prev, 2, 0))
        y = cb + cw[0] * u2
        y = y + cw[1] * u1
        y = y + cw[2] * u
        o_ref[r0:r0 + step, :] = (b * y * _silu(z)).astype(o_ref.dtype)
        u_prev = u


def _conv_branch(a, row_scale, wt, conv_w, conv_b, l, seq, tc=LANES):
    m, kdim = a.shape
    depth, taps, width = conv_w.shape
    nb = width // tc
    conv_w = conv_w.reshape(depth, taps, nb, tc)
    conv_b = conv_b.reshape(depth, nb, tc)
    wsp = lambda g: pl.BlockSpec((None, tc, kdim), lambda b, j: (l, g * nb + j, 0))
    return pl.pallas_call(
        _conv_proj_kernel,
        grid=(m // seq, nb),
        in_specs=[_resident_rows_spec(seq, kdim), wsp(0), wsp(1), wsp(2), wsp(3),
                  pl.BlockSpec((seq, 1), lambda b, j: (b, 0)),
                  pl.BlockSpec((None, taps, nb, tc), lambda b, j: (l, 0, 0, 0)),
                  pl.BlockSpec((None, nb, tc), lambda b, j: (l, 0, 0))],
        out_specs=pl.BlockSpec((seq, tc), lambda b, j: (b, j)),
        out_shape=jax.ShapeDtypeStruct((m, width), BF16),
        compiler_params=_cparams("parallel", "arbitrary"),
        name="conv_branch",
    )(a, wt, wt, wt, wt, row_scale, conv_w, conv_b)


def _memattn_kernel(q_ref, k_ref, v_ref, z_ref, o_ref):
    for h in range(X_HEADS):
        sl = slice(h * HEAD_DIM, (h + 1) * HEAD_DIM)
        q = q_ref[:, sl].astype(BF16)
        k = k_ref[:, sl].astype(BF16)
        v = v_ref[:, sl].astype(BF16)
        s = lax.dot_general(q, k, (((1,), (1,)), ((), ())), preferred_element_type=F32) * SCALE
        e = jnp.exp(s - jnp.max(s, axis=-1, keepdims=True))
        p = e / jnp.sum(e, axis=-1, keepdims=True)
        o = jnp.dot(p.astype(BF16), v, preferred_element_type=F32)
        o_ref[:, sl] = (o * _silu(z_ref[:, sl])).astype(o_ref.dtype)


def _mem_attention(proj, memkv, bsz, seq, mem_len, q_blk, z_blk, tq=512):
    nq = seq // tq
    xw = X_HEADS * HEAD_DIM
    row = lambda cb: pl.BlockSpec((tq, xw), lambda b, i: (b * nq + i, cb))
    return pl.pallas_call(
        _memattn_kernel,
        grid=(bsz, nq),
        in_specs=[row(q_blk),
                  pl.BlockSpec((mem_len, xw), lambda b, i: (b, 0)),
                  pl.BlockSpec((mem_len, xw), lambda b, i: (b, 1)),
                  row(z_blk)],
        out_specs=row(0),
        out_shape=jax.ShapeDtypeStruct((bsz * seq, xw), BF16),
        compiler_params=_cparams("parallel", "parallel"),
        name="mem_attention",
    )(proj, memkv, memkv, proj)


def _compress_kernel(*refs, n_half):
    *kv_refs, pos_ref, w1_ref, w2_ref, o_ref = refs
    half = CMP_BLOCK // 2
    groups = len(kv_refs)
    acc_a = jnp.zeros((groups * n_half, HEAD_DIM), F32)
    acc_b = jnp.zeros((groups * n_half, HEAD_DIM), F32)
    for t in range(half):
        x = jnp.concatenate([ref[pl.ds(t, n_half, stride=half), :] for ref in kv_refs], axis=0)
        xa = (x + pos_ref[t:t + 1, :]).astype(BF16)
        xb = (x + pos_ref[half + t:half + t + 1, :]).astype(BF16)
        acc_a = acc_a + jnp.dot(xa, w1_ref[t * HEAD_DIM:(t + 1) * HEAD_DIM, :],
                                preferred_element_type=F32)
        acc_b = acc_b + jnp.dot(xb, w1_ref[(half + t) * HEAD_DIM:(half + t + 1) * HEAD_DIM, :],
                                preferred_element_type=F32)
    w2 = w2_ref[...]
    for g in range(groups):
        rows = slice(g * n_half, (g + 1) * n_half)
        hid = _silu(acc_a[rows] + pltpu.roll(acc_b[rows], n_half - 1, 0))
        o_ref[g] = jnp.dot(hid.astype(BF16), w2, preferred_element_type=F32)


def _compress(proj, cmp_pos, cmp_w1, cmp_w2, l, bsz, seq, kv_off):
    g = NSA_KV_GROUPS
    kv_blk = kv_off // HEAD_DIM
    n_half = seq // (CMP_BLOCK // 2)
    kv_spec = lambda gi: pl.BlockSpec((seq, HEAD_DIM), lambda b, c: (b, kv_blk + c * g + gi))
    return pl.pallas_call(
        functools.partial(_compress_kernel, n_half=n_half),
        grid=(bsz, 2),
        in_specs=[kv_spec(gi) for gi in range(g)] + [
                  pl.BlockSpec((None, None, CMP_BLOCK, HEAD_DIM), lambda b, c: (l, c, 0, 0)),
                  pl.BlockSpec((None, None, CMP_BLOCK * HEAD_DIM, HEAD_DIM),
                               lambda b, c: (l, c, 0, 0)),
                  pl.BlockSpec((None, None, HEAD_DIM, HEAD_DIM), lambda b, c: (l, c, 0, 0))],
        out_specs=pl.BlockSpec((None, None, g, n_half, HEAD_DIM), lambda b, c: (b, c, 0, 0, 0)),
        out_shape=jax.ShapeDtypeStruct((bsz, 2, g, n_half, HEAD_DIM), F32),
        compiler_params=_cparams("parallel", "parallel"),
        name="nsa_compress",
    )(*([proj] * g), cmp_pos, cmp_w1, cmp_w2)


def _rope_tile(x, cos, sin_signed):
    return x * cos + pltpu.roll(x, HEAD_DIM // 2, 1) * sin_signed


def _nsa_prep_kernel(q_ref, ks_ref, vs_ref, kw_ref, vw_ref, cos_ref, sin_ref,
                     qr_ref, kso_ref, vst_ref, kwo_ref, vwt_ref):
    cos = cos_ref[...]
    sin = sin_ref[...]
    for h in range(NSA_HEADS):
        sl = slice(h * HEAD_DIM, (h + 1) * HEAD_DIM)
        qr_ref[:, sl] = (_rope_tile(q_ref[:, sl], cos, sin) * (SCALE * LOG2E)).astype(BF16)
    n_tiles, _, width = vst_ref.shape
    for g in range(NSA_KV_GROUPS):
        sl = slice(g * HEAD_DIM, (g + 1) * HEAD_DIM)
        kso_ref[:, sl] = _rope_tile(ks_ref[:, sl], cos, sin).astype(BF16)
        kwo_ref[:, sl] = _rope_tile(kw_ref[:, sl], cos, sin).astype(BF16)
        vs_t = jnp.transpose(vs_ref[:, sl]).astype(BF16)
        vw_t = jnp.transpose(vw_ref[:, sl]).astype(BF16)
        for c in range(n_tiles):
            vst_ref[c, sl, :] = vs_t[:, c * width:(c + 1) * width]
            vwt_ref[c, sl, :] = vw_t[:, c * width:(c + 1) * width]


def _nsa_prep(proj, cos, sin_signed, bsz, seq, q_off, kv_off):
    tr = NSA_TQ
    nsa_w = NSA_HEADS * HEAD_DIM
    gw = NSA_KV_GROUPS * HEAD_DIM
    nr = seq // tr
    t = bsz * seq
    qblk = q_off // nsa_w
    kvb = kv_off // gw
    n_tiles = tr // NSA_TK
    rowq = lambda w, cb: pl.BlockSpec((tr, w), lambda b, i: (b * nr + i, cb))
    tab = pl.BlockSpec((tr, HEAD_DIM), lambda b, i: (i, 0))
    vt_spec = pl.BlockSpec((None, n_tiles, gw, NSA_TK), lambda b, i: (b, i, 0, 0))
    vt_shape = jax.ShapeDtypeStruct((bsz, seq // NSA_TK, gw, NSA_TK), BF16)
    return pl.pallas_call(
        _nsa_prep_kernel,
        grid=(bsz, nr),
        in_specs=[rowq(nsa_w, qblk), rowq(gw, kvb + 2), rowq(gw, kvb + 3),
                  rowq(gw, kvb + 4), rowq(gw, kvb + 5), tab, tab],
        out_specs=[rowq(nsa_w, 0), rowq(gw, 0), vt_spec, rowq(gw, 0), vt_spec],
        out_shape=[jax.ShapeDtypeStruct((t, nsa_w), BF16),
                   jax.ShapeDtypeStruct((t, gw), BF16), vt_shape,
                   jax.ShapeDtypeStruct((t, gw), BF16), vt_shape],
        compiler_params=_cparams("parallel", "parallel"),
        name="nsa_prep",
    )(proj, proj, proj, proj, proj, cos, sin_signed)


def _col_reduce(x, op, ways=4):
    nblk = x.shape[0] // 8
    accs = [x[8 * k:8 * (k + 1)] for k in range(min(ways, nblk))]
    for k in range(len(accs), nblk):
        accs[k % ways] = op(accs[k % ways], x[8 * k:8 * (k + 1)])
    while len(accs) > 1:
        accs = [op(accs[a], accs[a + 1]) for a in range(0, len(accs), 2)]
    return accs[0]


def _col_max(x):
    return jnp.max(_col_reduce(x, jnp.maximum), axis=0, keepdims=True)


def _col_sum(x):
    return jnp.sum(_col_reduce(x, jnp.add), axis=0, keepdims=True)


def _nsa_kernel(q_ref, qr_ref, kc_ref, vc_ref, ks_ref, vst_ref, kw_ref, vwt_ref,
                ng_ref, nz_ref, ovl_ref, exp_ref, o_ref,
                qa_ref, ts0_ref, ts1_ref, xs0_ref, xs1_ref,
                tw0_ref, tw1_ref, tw2_ref, tw3_ref, xw0_ref, xw1_ref, xw2_ref, xw3_ref,
                ms_ref, mw_ref, accs_ref, accw_ref, ocmp_ref, ngt_ref,
                *, tq, tk, n_slc):
    assert tq == 2 * tk and WINDOW == tq
    gi = pl.program_id(1)
    i = pl.program_id(2)
    q0 = i * tq
    nt = (((1,), (1,)), ((), ()))
    pos_row = q0 + lax.broadcasted_iota(jnp.int32, (1, tq), 1)
    halves = [slice(hq * NSA_REP * tk, (hq + 1) * NSA_REP * tk) for hq in range(2)]
    pos_all = jnp.concatenate([pos_row[:, hq * tk:(hq + 1) * tk] for hq in range(2)
                               for _ in range(NSA_REP)], axis=1)
    heads = [slice(r * HEAD_DIM, (r + 1) * HEAD_DIM) for r in range(NSA_REP)]

    def stack_rows(ref):
        return jnp.concatenate([ref[hq * tk:(hq + 1) * tk, h] for hq in range(2) for h in heads],
                               axis=0)

    def head_cols(x, r):
        return jnp.concatenate([x[:, (hq * NSA_REP + r) * tk:(hq * NSA_REP + r + 1) * tk]
                                for hq in range(2)], axis=1)

    qb = stack_rows(q_ref).astype(BF16)
    kc = kc_ref[...].astype(BF16)
    vc_t = jnp.transpose(vc_ref[...]).astype(BF16)
    n_cmp = kc.shape[0]
    cmp_end = CMP_STRIDE * lax.broadcasted_iota(jnp.int32, (n_cmp, 1), 0) + (CMP_BLOCK - 1)
    cmask = cmp_end <= pos_all
    s = lax.dot_general(kc, qb, nt, preferred_element_type=F32) * SCALE
    s = jnp.where(cmask, s, NEG)
    e = jnp.exp(s - _col_max(s))
    p = e * (1.0 / _col_sum(e))
    p = jnp.where(cmask, p, 0.0).astype(BF16)
    ocmp_ref[...] = jnp.dot(vc_t, p, preferred_element_type=F32)

    qa_ref[:, 0:HEAD_DIM] = stack_rows(qr_ref)
    n_top = min(SLC_TOP_N, n_slc)

    @pl.when((i + 1) * tq <= n_top * SLC_BLOCK)
    def _():
        qa_ref[:, HEAD_DIM:] = jnp.zeros((NSA_REP * tq, LANES), BF16)

    @pl.when((i + 1) * tq > n_top * SLC_BLOCK)
    def _():
        imp_all = jnp.dot(ovl_ref[...], p, preferred_element_type=F32)
        val = head_cols(imp_all, 0)
        for r in range(1, NSA_REP):
            val = val + head_cols(imp_all, r)
        blk = lax.broadcasted_iota(jnp.int32, (n_slc, 1), 0)
        cur = pos_row // SLC_BLOCK
        future = blk * SLC_BLOCK > pos_row
        forced = (blk == 0) | (blk == cur) | (blk == cur - 1)
        val = jnp.where(future, -jnp.inf, jnp.where(forced, jnp.inf, val))
        rank = jnp.zeros((n_slc, tq), jnp.int32)
        for j in range(n_slc):
            vj = val[j:j + 1, :]
            beats = (vj > val) | ((vj == val) & (blk > j))
            rank = rank + beats.astype(jnp.int32)
        sel_bias = jnp.where(rank < n_top, 0.0, NEG)
        sel_bias = jnp.concatenate([sel_bias, jnp.zeros((LANES - n_slc, tq), F32)], axis=0)
        sel_bias = jnp.transpose(sel_bias).astype(BF16)
        qa_ref[:, HEAD_DIM:] = jnp.concatenate([sel_bias[hq * tk:(hq + 1) * tk] for hq in range(2)
                                                for _ in range(NSA_REP)], axis=0)

    ones_rows = (lax.broadcasted_iota(jnp.int32, (VT_PAD, tk), 0) == 0).astype(BF16)

    def scores(t_ref, x_ref, keys, queries, visible):
        t = lax.dot_general(keys, queries, nt, preferred_element_type=F32)
        if visible is not None:
            t = jnp.where(visible, t, NEG)
        t_ref[...] = t
        x_ref[...] = _col_max(t)

    def absorb(t_ref, x_ref, vt, m_ref, acc_ref):
        m_old = m_ref[...]
        m_new = jnp.maximum(m_old, x_ref[...])
        alpha = jnp.exp2(m_old - m_new)
        e = jnp.exp2(t_ref[...] - m_new).astype(BF16)
        vt_ones = jnp.concatenate([vt, ones_rows], axis=0)
        acc_ref[...] = alpha * acc_ref[...] + jnp.dot(vt_ones, e, preferred_element_type=F32)
        m_ref[...] = m_new

    def key_pos(c):
        return c * tk + lax.broadcasted_iota(jnp.int32, (tk, 1), 0)

    def slc_scores(t_ref, x_ref, c, causal):
        start = pl.multiple_of(c * tk, tk)
        keys = jnp.concatenate([ks_ref[pl.ds(start, tk), :], exp_ref[c]], axis=1)
        scores(t_ref, x_ref, keys, qa_ref[...], (key_pos(c) <= pos_all) if causal else None)

    def slc_absorb(t_ref, x_ref, c):
        absorb(t_ref, x_ref, vst_ref[c], ms_ref, accs_ref)

    slc_scores(ts0_ref, xs0_ref, 2 * i, True)
    slc_scores(ts1_ref, xs1_ref, 2 * i + 1, True)
    ms_ref[...] = jnp.maximum(xs0_ref[...], xs1_ref[...])
    accs_ref[...] = jnp.zeros(accs_ref.shape, F32)
    slc_absorb(ts0_ref, xs0_ref, 2 * i)

    def slc_pair(j, carry):
        slc_scores(ts0_ref, xs0_ref, 2 * j, False)
        slc_absorb(ts1_ref, xs1_ref, jnp.where(j == 0, 2 * i + 1, 2 * j - 1))
        slc_scores(ts1_ref, xs1_ref, 2 * j + 1, False)
        slc_absorb(ts0_ref, xs0_ref, 2 * j)
        return carry

    lax.fori_loop(0, i, slc_pair, 0)
    slc_absorb(ts1_ref, xs1_ref, jnp.where(i == 0, 1, 2 * i - 1))

    low = jnp.maximum(pos_all - WINDOW, -1)
    wbuf = [(tw0_ref, xw0_ref), (tw1_ref, xw1_ref), (tw2_ref, xw2_ref), (tw3_ref, xw3_ref)]

    def win_scores(buf, hq, back):
        c = 2 * i + hq - back
        kpos = key_pos(c)
        start = pl.multiple_of(jnp.maximum(c, 0) * tk, tk)
        if back == 0:
            visible = kpos <= pos_all[:, halves[hq]]
        elif back == hq:
            visible = None
        else:
            visible = kpos > low[:, halves[hq]]
        scores(*wbuf[buf], kw_ref[pl.ds(start, tk), :], qa_ref[halves[hq], 0:HEAD_DIM], visible)

    def win_absorb(buf, hq, back):
        c = jnp.maximum(2 * i + hq - back, 0)
        absorb(*wbuf[buf], vwt_ref[c], mw_ref.at[:, halves[hq]], accw_ref.at[:, halves[hq]])

    mw_ref[...] = jnp.full(mw_ref.shape, NEG, F32)
    accw_ref[...] = jnp.zeros(accw_ref.shape, F32)
    win_scores(0, 0, 0)
    win_scores(1, 1, 0)
    win_absorb(0, 0, 0)
    win_scores(2, 0, 1)
    win_absorb(1, 1, 0)
    win_scores(3, 1, 1)
    win_absorb(2, 0, 1)
    win_scores(0, 0, 2)
    win_absorb(3, 1, 1)
    win_scores(1, 1, 2)
    win_absorb(0, 0, 2)
    win_absorb(1, 1, 2)

    ngt_ref[...] = jnp.transpose(ng_ref[...])
    inv_ls = 1.0 / accs_ref[HEAD_DIM:HEAD_DIM + 1, :]
    inv_lw = 1.0 / accw_ref[HEAD_DIM:HEAD_DIM + 1, :]
    o_cmp = ocmp_ref[...]
    o_slc = accs_ref[0:HEAD_DIM, :] * inv_ls
    o_win = accw_ref[0:HEAD_DIM, :] * inv_lw
    for r in range(NSA_REP):
        row = (gi * NSA_REP + r) * NSA_BRANCHES
        g_cmp = _sigmoid(ngt_ref[pl.ds(row, 1), :])
        g_slc = _sigmoid(ngt_ref[pl.ds(row + 1, 1), :])
        g_win = _sigmoid(ngt_ref[pl.ds(row + 2, 1), :])
        o_t = (g_cmp * head_cols(o_cmp, r) + g_slc * head_cols(o_slc, r)
               + g_win * head_cols(o_win, r))
        o_ref[:, heads[r]] = (jnp.transpose(o_t) * _silu(nz_ref[:, heads[r]])).astype(o_ref.dtype)


def _nsa_attention(qr, cmpkv, ks, vst, kw, vwt, proj, ovl, expand, bsz, seq, q_blk, ng_blk, nz_blk):
    tq, tk = NSA_TQ, NSA_TK
    g = NSA_KV_GROUPS
    gw = NSA_REP * HEAD_DIM
    nq = seq // tq
    n_slc = seq // SLC_BLOCK
    n_half = cmpkv.shape[-2]
    stacked = NSA_REP * tq
    rowq = lambda cb: pl.BlockSpec((tq, gw), lambda b, gi, i: (b * nq + i, cb + gi))
    kvs = pl.BlockSpec((seq, HEAD_DIM), lambda b, gi, i: (b, gi))
    cmps = lambda c: pl.BlockSpec((None, None, None, n_half, HEAD_DIM),
                                  lambda b, gi, i: (b, c, gi, 0, 0))
    vts = pl.BlockSpec((None, seq // tk, HEAD_DIM, tk), lambda b, gi, i: (b, 0, gi, 0))
    score_buf = pltpu.VMEM((tk, stacked), F32)
    stat_row = pltpu.VMEM((1, stacked), F32)
    half_score_buf = pltpu.VMEM((tk, stacked // 2), F32)
    half_stat_row = pltpu.VMEM((1, stacked // 2), F32)
    flash_acc = pltpu.VMEM((HEAD_DIM + VT_PAD, stacked), F32)
    return pl.pallas_call(
        functools.partial(_nsa_kernel, tq=tq, tk=tk, n_slc=n_slc),
        grid=(bsz, g, nq),
        in_specs=[rowq(q_blk), rowq(0), cmps(0), cmps(1), kvs, vts, kvs, vts,
                  pl.BlockSpec((tq, LANES), lambda b, gi, i: (b * nq + i, ng_blk)),
                  rowq(nz_blk),
                  pl.BlockSpec((n_slc, n_half), lambda b, gi, i: (0, 0)),
                  pl.BlockSpec((seq // tk, tk, LANES), lambda b, gi, i: (0, 0, 0))],
        out_specs=rowq(0),
        out_shape=jax.ShapeDtypeStruct((bsz * seq, NSA_HEADS * HEAD_DIM), BF16),
        scratch_shapes=[pltpu.VMEM((stacked, 2 * HEAD_DIM), BF16),
                        score_buf, score_buf, stat_row, stat_row,
                        half_score_buf, half_score_buf, half_score_buf, half_score_buf,
                        half_stat_row, half_stat_row, half_stat_row, half_stat_row,
                        stat_row, stat_row,
                        flash_acc, flash_acc, pltpu.VMEM((HEAD_DIM, stacked), F32),
                        pltpu.VMEM((LANES, tq), F32)],
        compiler_params=_cparams("parallel", "parallel", "arbitrary"),
        name="nsa_attention",
    )(proj, qr, cmpkv, cmpkv, ks, vst, kw, vwt, proj, proj, ovl, expand)


def _merge_kernel(ya_ref, yb_ref, yx_ref, wa_ref, wb_ref, wx_ref, g0_ref, g1_ref, g2_ref, o_ref):
    wa = wa_ref[...].astype(BF16)
    wb = wb_ref[...].astype(BF16)
    wx = wx_ref[...].astype(BF16)
    tm = o_ref.shape[0]
    step = tm // MERGE_ROW_CHUNKS
    for r0 in range(0, tm, step):
        rows = slice(r0, r0 + step)

        def branch(y_ref, w, g_ref):
            up = jnp.dot(y_ref[rows, :], w, preferred_element_type=F32)
            return _sigmoid(g_ref[rows, :]) * up

        u = branch(ya_ref, wa, g0_ref) + branch(yb_ref, wb, g1_ref) + branch(yx_ref, wx, g2_ref)
        o_ref[rows, :] = u.astype(o_ref.dtype)


def _merge(ya, yb, yx, wa, wb, wx, proj, l, mg_off, d_model, tm=2048, tn=256):
    t = ya.shape[0]
    nj = d_model // tn
    mgb = mg_off // tn
    act = lambda w: pl.BlockSpec((tm, w), lambda i, j: (i, 0), pipeline_mode=pl.Buffered(1))
    wsp = lambda w: pl.BlockSpec((None, w, tn), lambda i, j: (l, 0, j))
    gsp = lambda br: pl.BlockSpec((tm, tn), lambda i, j: (i, mgb + br * nj + j))
    return pl.pallas_call(
        _merge_kernel,
        grid=(t // tm, nj),
        in_specs=[act(ya.shape[1]), act(yb.shape[1]), act(yx.shape[1]),
                  wsp(wa.shape[1]), wsp(wb.shape[1]), wsp(wx.shape[1]),
                  gsp(0), gsp(1), gsp(2)],
        out_specs=pl.BlockSpec((tm, tn), lambda i, j: (i, j)),
        out_shape=jax.ShapeDtypeStruct((t, d_model), BF16),
        compiler_params=_cparams("parallel", "arbitrary"),
        name="gated_merge",
    )(ya, yb, yx, wa, wb, wx, proj, proj, proj)


def _rope_tables(seq):
    half = HEAD_DIM // 2
    inv_freq = ROPE_THETA ** (-jnp.arange(half, dtype=F32) / half)
    ang = jnp.arange(seq, dtype=jnp.int32).astype(F32)[:, None] * inv_freq[None, :]
    cos, sin = jnp.cos(ang), jnp.sin(ang)
    return jnp.concatenate([cos, cos], axis=-1), jnp.concatenate([-sin, sin], axis=-1)


def _overlap_table(seq, n_half):
    n_slc = seq // SLC_BLOCK
    cmp_start = np.arange(n_half) * CMP_STRIDE
    slc_start = np.arange(n_slc) * SLC_BLOCK
    ovl = ((cmp_start[None, :] < slc_start[:, None] + SLC_BLOCK)
           & (cmp_start[None, :] + CMP_BLOCK > slc_start[:, None]))
    ovl = ovl & (cmp_start[None, :] + CMP_BLOCK <= seq)
    return jnp.asarray(ovl.astype(np.float32), BF16)


def _expand_table(seq, tk):
    key_blk = np.arange(seq) // SLC_BLOCK
    e = (key_blk[:, None] == np.arange(LANES)[None, :]).astype(np.float32)
    return jnp.asarray(e.reshape(seq // tk, tk, LANES), BF16)


def kernel(x, mem, norm_g, w_in, conv_w, conv_b, cmp_pos, cmp_w1, cmp_w2, mem_norm_g, w_mem_kv,
           w_up_a, w_up_b, w_up_x, w_out, final_g):
    bsz, seq, d_model = x.shape
    mem_len = mem.shape[1]
    depth = w_in.shape[0]
    sc_w = conv_w.shape[-1]
    nsa_w = NSA_HEADS * HEAD_DIM
    kv_w = NSA_BRANCHES * 2 * NSA_KV_GROUPS * HEAD_DIM
    ng_w = NSA_BRANCHES * NSA_HEADS
    x_w = X_HEADS * HEAD_DIM
    proj_tn = 512

    conv_cols = 4 * sc_w
    q_off = 0
    kv_off = q_off + nsa_w
    ng_off = kv_off + kv_w
    nz_off = ng_off + proj_tn
    xq_off = nz_off + nsa_w
    xz_off = xq_off + x_w
    mg_off = xz_off + x_w
    assert seq % NSA_TQ == 0 and seq // SLC_BLOCK <= LANES and sc_w % 256 == 0 and 0 < ng_w < LANES
    assert conv_cols + mg_off + N_BRANCHES * d_model - proj_tn + ng_w == w_in.shape[-1]

    w_in_t = jnp.swapaxes(w_in, 1, 2)
    w_memkv_b = w_mem_kv.astype(BF16)
    cmp_w1_b = cmp_w1.astype(BF16)
    cmp_w2_b = cmp_w2.astype(BF16)
    norm_g3 = norm_g.reshape(depth, 1, d_model)
    mem_norm_g3 = mem_norm_g.reshape(depth, 1, d_model)

    cos, sin_signed = _rope_tables(seq)
    n_half = seq // CMP_STRIDE
    ovl = _overlap_table(seq, n_half)
    expand = _expand_table(seq, NSA_TK)

    t = bsz * seq
    xf = x.reshape(t, d_model)
    memf = mem.reshape(bsz * mem_len, d_model)

    hg, hr = _prenorm(xf, norm_g3, 0)
    for l in range(depth):
        y_a = _conv_branch(hg, hr, w_in_t, conv_w, conv_b, l, seq)
        proj = _in_proj(hg, hr, w_in_t, l, conv_cols, ng_off, ng_w, 2048, proj_tn)

        hm = _rmsnorm(memf, mem_norm_g3, l, BF16)
        memkv = _matmul(hm, w_memkv_b, l, F32, 512, 1024, name="mem_kv_proj")
        y_x = _mem_attention(proj, memkv, bsz, seq, mem_len, xq_off // x_w, xz_off // x_w)

        cmpkv = _compress(proj, cmp_pos, cmp_w1_b, cmp_w2_b, l, bsz, seq, kv_off)
        qr, ks, vst, kw, vwt = _nsa_prep(proj, cos, sin_signed, bsz, seq, q_off, kv_off)
        group_w = NSA_REP * HEAD_DIM
        y_b = _nsa_attention(qr, cmpkv, ks, vst, kw, vwt, proj, ovl, expand, bsz, seq,
                             q_off // group_w, ng_off // LANES, nz_off // group_w)

        u = _merge(y_a, y_b, y_x, w_up_a, w_up_b, w_up_x, proj, l, mg_off, d_model)
        if l + 1 < depth:
            xf, hg, hr = _out_proj(u, w_out, l, xf, 2048, 256, next_gain=norm_g3)
        else:
            xf = _out_proj(u, w_out, l, xf, 2048, 512)

    out = _rmsnorm(xf, final_g.reshape(1, 1, d_model), 0, F32)
    return out.reshape(bsz, seq, d_model)
```
